```python
import math
import jax, jax.numpy as jnp
from jax import lax
import numpy as np

D_MODEL = 1024
BATCH = 4
SEQ = 4096
DEPTH = 2

CHUNK = 64
Q_BLOCK = 128
HEAD_DIM = 64
FOX_HEADS = 8
FOX_DIM = FOX_HEADS * HEAD_DIM
GLA_HEADS = 4
GLA_DK = 64
GLA_DV = 128
GLA_QK = GLA_HEADS * GLA_DK
GLA_V = GLA_HEADS * GLA_DV
GLA_RANK = 16
GLA_TAU = 16.0
MIX_WIDTH = FOX_DIM + GLA_V
IN_COLS = 3 * FOX_DIM + FOX_HEADS + 2 * GLA_QK + 2 * GLA_V + GLA_RANK
S5_GROUP = 16
S5_GROUPS = D_MODEL // S5_GROUP
S5_STATE = 64
D_FF = 2816
N_EXPERTS = 8
TOP_K = 2
EXPERT_FF = 2816
N_EVEN = (DEPTH + 1) // 2
N_ODD = DEPTH // 2
EPS = 1e-6

kernel_name = "fox_gla_s5_moe_adaln_hybrid"


def rmsnorm(x, w):
    xf = x.astype(jnp.float32)
    y = xf * lax.rsqrt(jnp.mean(xf * xf, axis=-1, keepdims=True) + EPS)
    return y * w.astype(jnp.float32)


def adaln(c, w, b):
    m = jax.nn.silu(c.astype(jnp.float32)) @ w + b
    shift, scale, gate = jnp.split(m, 3, axis=-1)
    return shift[:, None, :], scale[:, None, :], gate[:, None, :]


def fox_attention(q, k, v, logf):
    B, H, S, dh = q.shape
    cum = jnp.cumsum(logf, axis=-1)
    nb = S // Q_BLOCK
    qb = q.reshape(B, H, nb, Q_BLOCK, dh).transpose(2, 0, 1, 3, 4)
    cb = cum.reshape(B, H, nb, Q_BLOCK).transpose(2, 0, 1, 3)
    kpos = jnp.arange(S)
    scale = dh ** -0.5

    def block(args):
        i, qi, ci = args
        s = jnp.einsum('bhqd,bhkd->bhqk', qi, k) * scale + ci[..., None] - cum[:, :, None, :]
        qpos = i * Q_BLOCK + jnp.arange(Q_BLOCK)
        s = jnp.where(kpos[None, :] <= qpos[:, None], s, -jnp.inf)
        p = jax.nn.softmax(s, axis=-1)
        return jnp.einsum('bhqk,bhkd->bhqd', p, v)

    out = lax.map(block, (jnp.arange(nb), qb, cb))
    return out.transpose(1, 2, 0, 3, 4).reshape(B, H, S, dh)


def gla_attention(q, k, v, log_a):
    B, H, S, dk = q.shape
    dv = v.shape[-1]
    nc = S // CHUNK

    def to_chunks(t):
        return t.reshape(B, H, nc, CHUNK, t.shape[-1]).transpose(2, 0, 1, 3, 4)

    qc, kc, vc, ac = (to_chunks(q * dk ** -0.5), to_chunks(k), to_chunks(v), to_chunks(log_a))
    bc = jnp.cumsum(ac, axis=3)
    causal = jnp.tril(jnp.ones((CHUNK, CHUNK), dtype=bool))

    def step(state, inp):
        qi, ki, vi, bi = inp
        diff = bi[:, :, :, None, :] - bi[:, :, None, :, :]
        decay = jnp.exp(jnp.where(causal[None, None, :, :, None], diff, -jnp.inf))
        attn = jnp.einsum('bhtd,bhsd,bhtsd->bhts', qi, ki, decay)
        o = (jnp.einsum('bhts,bhsv->bhtv', attn, vi)
             + jnp.einsum('bhtd,bhdv->bhtv', qi * jnp.exp(bi), state))
        blast = bi[:, :, -1:, :]
        state = (jnp.exp(blast[:, :, 0, :])[..., None] * state
                 + jnp.einsum('bhsd,bhsv->bhdv', ki * jnp.exp(blast - bi), vi))
        return state, o

    s0 = jnp.zeros((B, H, dk, dv), jnp.float32)
    _, o = lax.scan(step, s0, (qc, kc, vc, bc))
    return o.transpose(1, 2, 0, 3, 4).reshape(B, H, S, dv)


def even_mixer(h, w_in, fox_fb, gla_w2, gla_b2, gla_norm, w_o):
    B, S, _ = h.shape
    sizes = (FOX_DIM, FOX_DIM, FOX_DIM, FOX_HEADS, GLA_QK, GLA_QK, GLA_V, GLA_V, GLA_RANK)
    cuts = [sum(sizes[:i + 1]) for i in range(len(sizes) - 1)]
    fq, fk, fv, ff, gq, gk, gv, gg, glr = jnp.split(h @ w_in, cuts, axis=-1)

    def heads(t, n):
        return t.reshape(B, S, n, -1).transpose(0, 2, 1, 3)

    logf = jax.nn.log_sigmoid(ff + fox_fb).transpose(0, 2, 1)
    fox = fox_attention(heads(fq, FOX_HEADS), heads(fk, FOX_HEADS), heads(fv, FOX_HEADS), logf)
    fox = fox.transpose(0, 2, 1, 3).reshape(B, S, FOX_DIM)
    log_a = jax.nn.log_sigmoid(glr @ gla_w2 + gla_b2) / GLA_TAU
    gla = gla_attention(heads(gq, GLA_HEADS), heads(gk, GLA_HEADS), heads(gv, GLA_HEADS),
                        heads(log_a, GLA_HEADS))
    gla = rmsnorm(gla.transpose(0, 2, 1, 3), gla_norm).reshape(B, S, GLA_V) * jax.nn.silu(gg)
    return jnp.concatenate([fox, gla], axis=-1) @ w_o


def s5_mixer(h, w_in, lam_re, lam_im, log_dt, b_re, b_im, c_re, c_im, d_skip, w_glu, w_o):
    B, S, _ = h.shape
    f32 = jnp.float32
    lam_re, lam_im, log_dt = lam_re.astype(f32), lam_im.astype(f32), log_dt.astype(f32)
    b_re, b_im, c_re, c_im = b_re.astype(f32), b_im.astype(f32), c_re.astype(f32), c_im.astype(f32)
    u = h @ w_in
    ug = u.reshape(B, S, S5_GROUPS, S5_GROUP)
    dt = jnp.exp(log_dt)[:, None]
    mag = jnp.exp(lam_re * dt)
    a_re = mag * jnp.cos(lam_im * dt)
    a_im = mag * jnp.sin(lam_im * dt)
    den = lam_re * lam_re + lam_im * lam_im
    nr, ni = a_re - 1.0, a_im
    f_re = (nr * lam_re + ni * lam_im) / den
    f_im = (ni * lam_re - nr * lam_im) / den
    bb_re = f_re[..., None] * b_re - f_im[..., None] * b_im
    bb_im = f_re[..., None] * b_im + f_im[..., None] * b_re
    ut = ug.transpose(1, 0, 2, 3)
    bu_re = jnp.einsum('sbgk,gpk->sbgp', ut, bb_re)
    bu_im = jnp.einsum('sbgk,gpk->sbgp', ut, bb_im)
    ar_t = jnp.broadcast_to(a_re[None, None], (S, 1, S5_GROUPS, S5_STATE))
    ai_t = jnp.broadcast_to(a_im[None, None], (S, 1, S5_GROUPS, S5_STATE))

    def combine(e1, e2):
        a1r, a1i, b1r, b1i = e1
        a2r, a2i, b2r, b2i = e2
        return (a1r * a2r - a1i * a2i,
                a1r * a2i + a1i * a2r,
                a2r * b1r - a2i * b1i + b2r,
                a2r * b1i + a2i * b1r + b2i)

    _, _, xr, xi = lax.associative_scan(combine, (ar_t, ai_t, bu_re, bu_im), axis=0)
    y = jnp.einsum('sbgp,gkp->bsgk', xr, c_re) - jnp.einsum('sbgp,gkp->bsgk', xi, c_im)
    y = y.reshape(B, S, D_MODEL) + d_skip * u
    y = jax.nn.gelu(y)
    y = y * jax.nn.sigmoid(y @ w_glu)
    return y @ w_o


def swiglu(h, w_gate, w_up, w_down):
    return (jax.nn.silu(h @ w_gate) * (h @ w_up)) @ w_down


def moe_swiglu(h, w_router, w_gate, w_up, w_down):
    logits = (h @ w_router).astype(jnp.float32)
    top_val, top_idx = lax.top_k(logits, TOP_K)
    weights = jax.nn.softmax(top_val, axis=-1)
    gates = jnp.sum(jax.nn.one_hot(top_idx, N_EXPERTS, dtype=jnp.float32) * weights[..., None], axis=-2)
    out = jnp.zeros_like(h)
    for e in range(N_EXPERTS):
        out = out + gates[..., e:e + 1] * swiglu(h, w_gate[e], w_up[e], w_down[e])
    return out


def setup_inputs(seed: int = 0) -> dict:
    key = jax.random.key(seed)
    ks = iter(jax.random.split(key, 64))
    f32 = jnp.float32

    def nrm(shape, scale):
        return jax.random.normal(next(ks), shape, f32) * scale

    def gain(shape):
        return 1.0 + 0.05 * jax.random.normal(next(ks), shape, f32)

    D = D_MODEL
    NE, NO = N_EVEN, N_ODD
    G, P, K = S5_GROUPS, S5_STATE, S5_GROUP
    mod_s = 0.5 * D ** -0.5
    inp = {}
    inp['x'] = nrm((BATCH, SEQ, D), 1.0)
    inp['c'] = nrm((BATCH, D), 1.0)
    inp['e_norm_mix'] = gain((NE, D))
    inp['e_mod_mix_w'] = nrm((NE, D, 3 * D), mod_s)
    inp['e_mod_mix_b'] = nrm((NE, 3 * D), 0.02)
    inp['e_w_in'] = nrm((NE, D, IN_COLS), D ** -0.5)
    inp['e_fox_fb'] = 2.0 + nrm((NE, FOX_HEADS), 0.5)
    inp['e_gla_w2'] = nrm((NE, GLA_RANK, GLA_QK), GLA_RANK ** -0.5)
    inp['e_gla_b2'] = nrm((NE, GLA_QK), 0.02)
    inp['e_gla_norm'] = gain((NE, GLA_DV))
    inp['e_w_o'] = nrm((NE, MIX_WIDTH, D), MIX_WIDTH ** -0.5)
    inp['e_norm_ffn'] = gain((NE, D))
    inp['e_mod_ffn_w'] = nrm((NE, D, 3 * D), mod_s)
    inp['e_mod_ffn_b'] = nrm((NE, 3 * D), 0.02)
    inp['e_ffn_gate'] = nrm((NE, D, D_FF), D ** -0.5)
    inp['e_ffn_up'] = nrm((NE, D, D_FF), D ** -0.5)
    inp['e_ffn_down'] = nrm((NE, D_FF, D), D_FF ** -0.5)
    inp['o_norm_mix'] = gain((NO, D))
    inp['o_mod_mix_w'] = nrm((NO, D, 3 * D), mod_s)
    inp['o_mod_mix_b'] = nrm((NO, 3 * D), 0.02)
    inp['o_w_in'] = nrm((NO, D, D), D ** -0.5)
    inp['o_lam_re'] = -0.5 + nrm((NO, G, P), 0.01)
    inp['o_lam_im'] = math.pi * jnp.arange(P, dtype=f32)[None, None, :] + nrm((NO, G, P), 0.01)
    inp['o_log_dt'] = jax.random.uniform(next(ks), (NO, G), f32, math.log(1e-3), math.log(1e-1))
    inp['o_b_re'] = nrm((NO, G, P, K), (2.0 * K) ** -0.5)
    inp['o_b_im'] = nrm((NO, G, P, K), (2.0 * K) ** -0.5)
    inp['o_c_re'] = nrm((NO, G, K, P), (2.0 / P) ** 0.5)
    inp['o_c_im'] = nrm((NO, G, K, P), (2.0 / P) ** 0.5)
    inp['o_d_skip'] = nrm((NO, D), 1.0)
    inp['o_w_glu'] = nrm((NO, D, D), D ** -0.5)
    inp['o_w_o'] = nrm((NO, D, D), D ** -0.5)
    inp['o_norm_ffn'] = gain((NO, D))
    inp['o_mod_ffn_w'] = nrm((NO, D, 3 * D), mod_s)
    inp['o_mod_ffn_b'] = nrm((NO, 3 * D), 0.02)
    inp['o_router'] = nrm((NO, D, N_EXPERTS), D ** -0.5)
    inp['o_exp_gate'] = nrm((NO, N_EXPERTS, D, EXPERT_FF), D ** -0.5)
    inp['o_exp_up'] = nrm((NO, N_EXPERTS, D, EXPERT_FF), D ** -0.5)
    inp['o_exp_down'] = nrm((NO, N_EXPERTS, EXPERT_FF, D), EXPERT_FF ** -0.5)
    inp['final_norm'] = gain((D,))
    return inp


def reference(x, c,
              e_norm_mix, e_mod_mix_w, e_mod_mix_b, e_w_in, e_fox_fb, e_gla_w2, e_gla_b2,
              e_gla_norm, e_w_o, e_norm_ffn, e_mod_ffn_w, e_mod_ffn_b, e_ffn_gate, e_ffn_up,
              e_ffn_down,
              o_norm_mix, o_mod_mix_w, o_mod_mix_b, o_w_in, o_lam_re, o_lam_im, o_log_dt,
              o_b_re, o_b_im, o_c_re, o_c_im, o_d_skip, o_w_glu, o_w_o, o_norm_ffn,
              o_mod_ffn_w, o_mod_ffn_b, o_router, o_exp_gate, o_exp_up, o_exp_down,
              final_norm):
    for i in range(DEPTH):
        j = i // 2
        if i % 2 == 0:
            sh, sc, g = adaln(c, e_mod_mix_w[j], e_mod_mix_b[j])
            h = rmsnorm(x, e_norm_mix[j]) * (1.0 + sc) + sh
            y = even_mixer(h, e_w_in[j], e_fox_fb[j], e_gla_w2[j], e_gla_b2[j], e_gla_norm[j], e_w_o[j])
            x = x + (g * y).astype(x.dtype)
            sh, sc, g = adaln(c, e_mod_ffn_w[j], e_mod_ffn_b[j])
            h = rmsnorm(x, e_norm_ffn[j]) * (1.0 + sc) + sh
            y = swiglu(h, e_ffn_gate[j], e_ffn_up[j], e_ffn_down[j])
            x = x + (g * y).astype(x.dtype)
        else:
            sh, sc, g = adaln(c, o_mod_mix_w[j], o_mod_mix_b[j])
            h = rmsnorm(x, o_norm_mix[j]) * (1.0 + sc) + sh
            y = s5_mixer(h, o_w_in[j], o_lam_re[j], o_lam_im[j], o_log_dt[j], o_b_re[j], o_b_im[j],
                         o_c_re[j], o_c_im[j], o_d_skip[j], o_w_glu[j], o_w_o[j])
            x = x + (g * y).astype(x.dtype)
            sh, sc, g = adaln(c, o_mod_ffn_w[j], o_mod_ffn_b[j])
            h = rmsnorm(x, o_norm_ffn[j]) * (1.0 + sc) + sh
            y = moe_swiglu(h, o_router[j], o_exp_gate[j], o_exp_up[j], o_exp_down[j])
            x = x + (g * y).astype(x.dtype)
    return rmsnorm(x, final_norm).astype(x.dtype)
```

```python
import functools

import jax
import jax.numpy as jnp
from jax import lax
from jax.experimental import pallas as pl
from jax.experimental.pallas import tpu as pltpu

F32 = jnp.float32
BF16 = jnp.bfloat16
HIGHEST = lax.Precision.HIGHEST
EPS = 1e-6
NEG_BIG = -1e30

LANES = 128
VMEM_LIMIT = 56 * 1024 * 1024

HEAD_DIM = 64
FOX_HEADS = 8
FOX_DIM = FOX_HEADS * HEAD_DIM
GLA_HEADS = 4
GLA_DK = 64
GLA_DV = 128
GLA_QK = GLA_HEADS * GLA_DK
GLA_V = GLA_HEADS * GLA_DV
GLA_RANK = 16
GLA_TAU = 16.0
GLA_CHUNK = 64
GLA_SUB = 16
S5_GROUP = 16
S5_STATE = 64
S5_GB = 8
N_EXPERTS = 8
SMALL_FF0 = 0
SMALL_LR0 = 8


def _params(sem):
    return pltpu.CompilerParams(dimension_semantics=sem, vmem_limit_bytes=VMEM_LIMIT)


def _log_sigmoid(x):
    return jnp.minimum(x, 0.0) - jnp.log(1.0 + jnp.exp(-jnp.abs(x)))


def _silu(x):
    return x * jax.nn.sigmoid(x)


def _gelu_tanh(x):
    c = 0.7978845608028654
    return 0.5 * x * (1.0 + jnp.tanh(c * (x + 0.044715 * (x * x * x))))


def _rms(x, w):
    return x * lax.rsqrt(jnp.mean(x * x, axis=-1, keepdims=True) + EPS) * w


def _norm_mod(x, nw, mod):
    return _rms(x, nw) * (1.0 + mod[1:2]) + mod[0:1]


def _nt_dot(a, b):
    return lax.dot_general(a, b, (((1,), (1,)), ((), ())), preferred_element_type=F32)


def _mod_kernel(ct_ref, w_ref, b_ref, o_ref):
    s = _silu(ct_ref[...])
    w = w_ref[...]
    for b in range(s.shape[1]):
        o_ref[b:b + 1, :] = jnp.sum(s[:, b:b + 1] * w, axis=0, keepdims=True) + b_ref[...]


def adaln_mod(c, w, b):
    bsz, d = c.shape
    n = w.shape[1]
    tn = 512
    out = pl.pallas_call(
        _mod_kernel,
        grid=(n // tn,),
        in_specs=[pl.BlockSpec((d, bsz), lambda j: (0, 0)),
                  pl.BlockSpec((d, tn), lambda j: (0, j)),
                  pl.BlockSpec((1, tn), lambda j: (0, j))],
        out_specs=pl.BlockSpec((bsz, tn), lambda j: (0, j)),
        out_shape=jax.ShapeDtypeStruct((bsz, n), F32),
        compiler_params=_params(("parallel",)),
        name="adaln_mod",
    )(c.T, w, b.reshape(1, n))
    return out.reshape(bsz, 3, d)


def _inproj_kernel(x_ref, mod_ref, nw_ref, wbig_ref, wsm_ref, fb_ref, tri_ref,
                   fq_ref, fk_ref, fv_ref, gq_ref, gk_ref, gv_ref, gg_ref, sm_ref, cum_ref,
                   carry_ref):
    i = pl.program_id(1)
    tm = x_ref.shape[1]
    h = _norm_mod(x_ref[0], nw_ref[...], mod_ref[0])
    hb = h.astype(BF16)
    col = 0
    for ref in (fq_ref, fk_ref, fv_ref, gq_ref, gk_ref, gv_ref, gg_ref):
        n = ref.shape[2]
        ref[0] = jnp.dot(hb, wbig_ref[:, col:col + n], preferred_element_type=F32).astype(ref.dtype)
        col += n
    small = jnp.dot(h, wsm_ref[...], precision=HIGHEST, preferred_element_type=F32)
    sm_ref[0] = small
    logf = _log_sigmoid(small + fb_ref[...])

    @pl.when(i == 0)
    def _():
        carry_ref[...] = jnp.zeros_like(carry_ref)

    cum = jnp.dot(tri_ref[...], logf, precision=HIGHEST, preferred_element_type=F32) + carry_ref[...]
    cum_ref[0] = cum
    carry_ref[...] = cum[tm - 1:tm, :]


def layer0_inproj(x, mod, nw, w_in, fox_fb, tm=512):
    bsz, s, d = x.shape
    c0 = 3 * FOX_DIM
    c_ff = c0
    c_g = c_ff + FOX_HEADS
    c_lr = c_g + 2 * GLA_QK + 2 * GLA_V
    wbig = jnp.concatenate([w_in[:, :c0], w_in[:, c_g:c_lr]], axis=1).astype(BF16)
    wsm = jnp.zeros((d, LANES), F32)
    wsm = wsm.at[:, SMALL_FF0:SMALL_FF0 + FOX_HEADS].set(w_in[:, c_ff:c_g])
    wsm = wsm.at[:, SMALL_LR0:SMALL_LR0 + GLA_RANK].set(w_in[:, c_lr:])
    fb = jnp.zeros((1, LANES), F32).at[0, SMALL_FF0:SMALL_FF0 + FOX_HEADS].set(fox_fb)
    tri = jnp.tril(jnp.ones((tm, tm), F32))
    widths = (FOX_DIM, FOX_DIM, FOX_DIM, GLA_QK, GLA_QK, GLA_V, GLA_V)
    row = lambda b, i: (b, i, 0)
    const2 = lambda b, i: (0, 0)
    outs = pl.pallas_call(
        _inproj_kernel,
        grid=(bsz, s // tm),
        in_specs=[pl.BlockSpec((1, tm, d), row),
                  pl.BlockSpec((1, 3, d), lambda b, i: (b, 0, 0)),
                  pl.BlockSpec((1, d), const2),
                  pl.BlockSpec(wbig.shape, const2),
                  pl.BlockSpec(wsm.shape, const2),
                  pl.BlockSpec((1, LANES), const2),
                  pl.BlockSpec((tm, tm), const2)],
        out_specs=[pl.BlockSpec((1, tm, n), row) for n in widths]
                  + [pl.BlockSpec((1, tm, LANES), row)] * 2,
        out_shape=[jax.ShapeDtypeStruct((bsz, s, n), BF16) for n in widths]
                  + [jax.ShapeDtypeStruct((bsz, s, LANES), F32)] * 2,
        scratch_shapes=[pltpu.VMEM((1, LANES), F32)],
        compiler_params=_params(("parallel", "arbitrary")),
        name="layer0_inproj",
    )(x, mod, nw.reshape(1, d), wbig, wsm, fb, tri)
    return outs


def _fox_kernel(q_ref, k_ref, v_ref, c_ref, o_ref, m_ref, l_ref, acc_ref):
    i = pl.program_id(2)
    j = pl.program_id(3)
    tq = q_ref.shape[1]
    tk = k_ref.shape[1]
    lane = lax.broadcasted_iota(jnp.int32, (1, LANES), 1)
    scale = HEAD_DIM ** -0.5

    @pl.when(j == 0)
    def _():
        m_ref[...] = jnp.full_like(m_ref, NEG_BIG)
        l_ref[...] = jnp.zeros_like(l_ref)
        acc_ref[...] = jnp.zeros_like(acc_ref)

    def step(diagonal):
        q = q_ref[0]
        k = k_ref[0]
        v = v_ref[0]
        for h in range(2):
            head = (lane < HEAD_DIM) if h == 0 else (lane >= HEAD_DIM)
            qh = jnp.where(head, q, jnp.zeros_like(q)) * jnp.asarray(scale, q.dtype)
            s = _nt_dot(qh, k) - c_ref[0, 0, h:h + 1, :]
            if diagonal:
                r = lax.broadcasted_iota(jnp.int32, (tq, tk), 0)
                c = lax.broadcasted_iota(jnp.int32, (tq, tk), 1)
                s = jnp.where(c <= r, s, NEG_BIG)
            m_prev = m_ref[h]
            m_new = jnp.maximum(m_prev, jnp.max(s, axis=1, keepdims=True))
            alpha = jnp.exp(m_prev - m_new)
            p = jnp.exp(s - m_new)
            l_ref[h] = alpha * l_ref[h] + jnp.sum(p, axis=1, keepdims=True)
            acc_ref[h] = alpha * acc_ref[h] + jnp.dot(p.astype(BF16), v, preferred_element_type=F32)
            m_ref[h] = m_new

    @pl.when(j < i)
    def _():
        step(False)

    @pl.when(j == i)
    def _():
        step(True)
        o0 = acc_ref[0] / l_ref[0]
        o1 = acc_ref[1] / l_ref[1]
        o_ref[0] = jnp.where(lane < HEAD_DIM, o0, o1).astype(o_ref.dtype)


def fox_attention(fq, fk, fv, cum_t, t=512):
    bsz, s, _ = fq.shape
    n_pair = FOX_HEADS // 2
    nt = s // t
    return pl.pallas_call(
        _fox_kernel,
        grid=(bsz, n_pair, nt, nt),
        in_specs=[pl.BlockSpec((1, t, LANES), lambda b, p, i, j: (b, i, p)),
                  pl.BlockSpec((1, t, LANES), lambda b, p, i, j: (b, jnp.minimum(i, j), p)),
                  pl.BlockSpec((1, t, LANES), lambda b, p, i, j: (b, jnp.minimum(i, j), p)),
                  pl.BlockSpec((1, 1, 2, t), lambda b, p, i, j: (b, p, 0, jnp.minimum(i, j)))],
        out_specs=pl.BlockSpec((1, t, LANES), lambda b, p, i, j: (b, i, p)),
        out_shape=jax.ShapeDtypeStruct((bsz, s, FOX_DIM), BF16),
        scratch_shapes=[pltpu.VMEM((2, t, 1), F32), pltpu.VMEM((2, t, 1), F32),
                        pltpu.VMEM((2, t, LANES), F32)],
        compiler_params=_params(("parallel", "parallel", "parallel", "arbitrary")),
        name="fox_attention",
    )(fq, fk, fv, cum_t)


def _gla_pair_chunk(q, k, v, la, st, tri, eye, k_scr, b_scr):
    C, SUB = GLA_CHUNK, GLA_SUB
    nsub = C // SUB
    lane = lax.broadcasted_iota(jnp.int32, (1, LANES), 1)
    head_a = lane < GLA_DK
    b = jnp.dot(tri, la, precision=HIGHEST, preferred_element_type=F32)
    b_last = b[C - 1:C, :]
    k_scr[...] = k
    b_scr[...] = b
    rowblk = lax.broadcasted_iota(jnp.int32, (C, 1), 0) // SUB

    refs = [b[m * SUB - 1:m * SUB, :] for m in range(1, nsub)]
    rsel = refs[-1]
    for m in range(nsub - 2, 0, -1):
        rsel = jnp.where(rowblk == m, refs[m - 1], rsel)
    qt = q * jnp.exp(jnp.minimum(b - rsel, 0.0))
    zero = jnp.zeros_like(q)
    qs = jnp.concatenate([jnp.where(rowblk == m, qt, zero) for m in range(1, nsub)], axis=1)
    ks = jnp.concatenate(
        [jnp.where(rowblk < m, k * jnp.exp(jnp.minimum(refs[m - 1] - b, 0.0)), zero)
         for m in range(1, nsub)], axis=1)
    lane3 = lax.broadcasted_iota(jnp.int32, (1, (nsub - 1) * LANES), 1)
    head_a3 = (lane3 & (LANES - 1)) < GLA_DK
    zero3 = jnp.zeros_like(ks)
    kstack = jnp.concatenate([jnp.where(head_a3, ks, zero3), jnp.where(head_a3, zero3, ks)], axis=0)
    a_off = _nt_dot(qs.astype(BF16), kstack.astype(BF16))

    row16 = lax.broadcasted_iota(jnp.int32, (SUB, 1), 0)
    blocks = []
    for i in range(nsub):
        qb = q[i * SUB:(i + 1) * SUB, :]
        bb = b[i * SUB:(i + 1) * SUB, :]
        d = jnp.zeros((SUB, LANES), F32)
        for sp in range(SUB):
            srow = i * SUB + sp
            e = qb * k_scr[srow:srow + 1, :] * jnp.exp(jnp.minimum(bb - b_scr[srow:srow + 1, :], 0.0))
            da = jnp.sum(jnp.where(head_a, e, 0.0), axis=1, keepdims=True)
            db = jnp.sum(jnp.where(head_a, 0.0, e), axis=1, keepdims=True)
            live = row16 >= sp
            da = jnp.where(live, da, 0.0)
            db = jnp.where(live, db, 0.0)
            d = jnp.where(lane == srow, da, jnp.where(lane == GLA_DK + srow, db, d))
        blocks.append(d)
    a_pair = a_off + jnp.concatenate(blocks, axis=0)

    lane_v = lax.broadcasted_iota(jnp.int32, (1, 2 * GLA_DV), 1)
    first_v = lane_v < GLA_DV
    zv = jnp.zeros_like(v)
    vbd = jnp.concatenate([jnp.where(first_v, v, zv), jnp.where(first_v, zv, v)], axis=0)
    qh = q * jnp.exp(b)
    o = (jnp.dot(a_pair.astype(BF16), vbd, preferred_element_type=F32)
         + _nt_dot(qh.astype(BF16), st.astype(BF16)))

    k_end = k * jnp.exp(b_last - b)
    v_t = _nt_dot(eye, v).astype(BF16)
    inc = jnp.dot(v_t, k_end.astype(BF16), preferred_element_type=F32)
    row_v = lax.broadcasted_iota(jnp.int32, (2 * GLA_DV, 1), 0)
    same_head = (row_v // GLA_DV) == (lane // GLA_DK)
    st_new = st * jnp.exp(b_last) + jnp.where(same_head, inc, 0.0)
    return o, st_new


def _gla_kernel(gq_ref, gk_ref, gv_ref, gg_ref, sm_ref, w2_ref, b2_ref, gn_ref, tri_ref, eye_ref,
                o_ref, st_ref, la_ref, k_scr, b_scr):
    i = pl.program_id(1)
    tm = gq_ref.shape[1]
    n_pair = GLA_HEADS // 2
    scale = GLA_DK ** -0.5

    @pl.when(i == 0)
    def _():
        st_ref[...] = jnp.zeros_like(st_ref)

    z = jnp.dot(sm_ref[0], w2_ref[...], precision=HIGHEST, preferred_element_type=F32) + b2_ref[...]
    la_ref[...] = _log_sigmoid(z) * (1.0 / GLA_TAU)
    gn = gn_ref[...]

    def chunk(c, carry):
        base = pl.multiple_of(c * GLA_CHUNK, GLA_CHUNK)
        rows = pl.ds(base, GLA_CHUNK)
        for hp in range(n_pair):
            ql = slice(hp * LANES, (hp + 1) * LANES)
            vl = slice(hp * 2 * GLA_DV, (hp + 1) * 2 * GLA_DV)
            q = gq_ref[0, rows, ql].astype(F32) * scale
            k = gk_ref[0, rows, ql].astype(F32)
            v = gv_ref[0, rows, vl]
            o, st_new = _gla_pair_chunk(q, k, v, la_ref[rows, ql], st_ref[hp], tri_ref[...],
                                        eye_ref[...], k_scr.at[hp], b_scr.at[hp])
            st_ref[hp] = st_new
            gate = _silu(gg_ref[0, rows, vl].astype(F32))
            halves = [_rms(o[:, h * GLA_DV:(h + 1) * GLA_DV], gn) for h in range(2)]
            o_ref[0, rows, vl] = (jnp.concatenate(halves, axis=1) * gate).astype(o_ref.dtype)
        return carry

    lax.fori_loop(0, tm // GLA_CHUNK, chunk, 0)


def gla_attention(gq, gk, gv, gg, small, w2, b2, gnorm, tm=512):
    bsz, s, _ = gq.shape
    w2p = jnp.zeros((LANES, GLA_QK), F32).at[SMALL_LR0:SMALL_LR0 + GLA_RANK].set(w2)
    tri = jnp.tril(jnp.ones((GLA_CHUNK, GLA_CHUNK), F32))
    eye = jnp.eye(2 * GLA_DV, dtype=BF16)
    row = lambda b, i: (b, i, 0)
    const2 = lambda b, i: (0, 0)
    return pl.pallas_call(
        _gla_kernel,
        grid=(bsz, s // tm),
        in_specs=[pl.BlockSpec((1, tm, GLA_QK), row), pl.BlockSpec((1, tm, GLA_QK), row),
                  pl.BlockSpec((1, tm, GLA_V), row), pl.BlockSpec((1, tm, GLA_V), row),
                  pl.BlockSpec((1, tm, LANES), row),
                  pl.BlockSpec(w2p.shape, const2), pl.BlockSpec((1, GLA_QK), const2),
                  pl.BlockSpec((1, GLA_DV), const2), pl.BlockSpec(tri.shape, const2),
                  pl.BlockSpec(eye.shape, const2)],
        out_specs=pl.BlockSpec((1, tm, GLA_V), row),
        out_shape=jax.ShapeDtypeStruct((bsz, s, GLA_V), BF16),
        scratch_shapes=[pltpu.VMEM((GLA_HEADS // 2, 2 * GLA_DV, LANES), F32),
                        pltpu.VMEM((tm, GLA_QK), F32),
                        pltpu.VMEM((GLA_HEADS // 2, GLA_CHUNK, LANES), F32),
                        pltpu.VMEM((GLA_HEADS // 2, GLA_CHUNK, LANES), F32)],
        compiler_params=_params(("parallel", "arbitrary")),
        name="gla_attention",
    )(gq, gk, gv, gg, small, w2p, b2.reshape(1, GLA_QK), gnorm.reshape(1, GLA_DV), tri, eye)


def _outproj_kernel(x_ref, a_ref, b_ref, wo_ref, mod_ref, o_ref):
    na = a_ref.shape[2]
    y = (jnp.dot(a_ref[0], wo_ref[:na, :], preferred_element_type=F32)
         + jnp.dot(b_ref[0], wo_ref[na:, :], preferred_element_type=F32))
    o_ref[0] = x_ref[0] + mod_ref[0][2:3] * y


def mixer_outproj(x, a, b, w_o, mod, tm=512):
    bsz, s, d = x.shape
    row = lambda bb, i: (bb, i, 0)
    wo = w_o.astype(BF16)
    return pl.pallas_call(
        _outproj_kernel,
        grid=(bsz, s // tm),
        in_specs=[pl.BlockSpec((1, tm, d), row),
                  pl.BlockSpec((1, tm, a.shape[2]), row),
                  pl.BlockSpec((1, tm, b.shape[2]), row),
                  pl.BlockSpec(wo.shape, lambda bb, i: (0, 0)),
                  pl.BlockSpec((1, 3, d), lambda bb, i: (bb, 0, 0))],
        out_specs=pl.BlockSpec((1, tm, d), row),
        out_shape=jax.ShapeDtypeStruct(x.shape, F32),
        compiler_params=_params(("parallel", "parallel")),
        name="mixer_outproj",
    )(x, a, b, wo, mod)


def _ffn_kernel(*refs, moe):
    if moe:
        (x_ref, mod_ref, nw_ref, wg_ref, wu_ref, wd_ref, gates_ref, fnw_ref,
         o_ref, h_ref, acc_ref, gcol_ref) = refs
    else:
        x_ref, mod_ref, nw_ref, wg_ref, wu_ref, wd_ref, o_ref, h_ref, acc_ref = refs
    e = pl.program_id(2)
    j = pl.program_id(3)
    last = jnp.logical_and(e == pl.num_programs(2) - 1, j == pl.num_programs(3) - 1)

    @pl.when(jnp.logical_and(e == 0, j == 0))
    def _():
        h_ref[...] = _norm_mod(x_ref[0], nw_ref[...], mod_ref[0]).astype(BF16)
        acc_ref[...] = jnp.zeros_like(acc_ref)

    hb = h_ref[...]
    act = (_silu(jnp.dot(hb, wg_ref[0], preferred_element_type=F32))
           * jnp.dot(hb, wu_ref[0], preferred_element_type=F32))
    if moe:
        @pl.when(j == 0)
        def _():
            lane = lax.broadcasted_iota(jnp.int32, (1, LANES), 1)
            gcol_ref[...] = jnp.sum(jnp.where(lane == e, gates_ref[0], 0.0), axis=1, keepdims=True)
        act = act * gcol_ref[...]
    acc_ref[...] += jnp.dot(act.astype(BF16), wd_ref[0], preferred_element_type=F32)

    @pl.when(last)
    def _():
        out = x_ref[0] + mod_ref[0][2:3] * acc_ref[...]
        if moe:
            out = _rms(out, fnw_ref[...])
        o_ref[0] = out


def swiglu_ffn(x, mod, nw, w_gate, w_up, w_down, gates=None, final_nw=None, tm=1024, tf=256):
    bsz, s, d = x.shape
    n_e, _, f = w_gate.shape
    moe = gates is not None
    row = lambda b, i, e, j: (b, i, 0)
    const2 = lambda b, i, e, j: (0, 0)
    in_specs = [pl.BlockSpec((1, tm, d), row),
                pl.BlockSpec((1, 3, d), lambda b, i, e, j: (b, 0, 0)),
                pl.BlockSpec((1, d), const2),
                pl.BlockSpec((1, d, tf), lambda b, i, e, j: (e, 0, j)),
                pl.BlockSpec((1, d, tf), lambda b, i, e, j: (e, 0, j)),
                pl.BlockSpec((1, tf, d), lambda b, i, e, j: (e, j, 0))]
    args = [x, mod, nw.reshape(1, d), w_gate, w_up, w_down]
    scratch = [pltpu.VMEM((tm, d), BF16), pltpu.VMEM((tm, d), F32)]
    if moe:
        in_specs += [pl.BlockSpec((1, tm, LANES), row), pl.BlockSpec((1, d), const2)]
        args += [gates, final_nw.reshape(1, d)]
        scratch += [pltpu.VMEM((tm, 1), F32)]
    return pl.pallas_call(
        functools.partial(_ffn_kernel, moe=moe),
        grid=(bsz, s // tm, n_e, f // tf),
        in_specs=in_specs,
        out_specs=pl.BlockSpec((1, tm, d), row),
        out_shape=jax.ShapeDtypeStruct(x.shape, F32),
        scratch_shapes=scratch,
        compiler_params=_params(("parallel", "parallel", "arbitrary", "arbitrary")),
        name="moe_swiglu" if moe else "dense_swiglu",
    )(*args)


def _s5_disc_kernel(lre_ref, lim_ref, ldt_ref, bre_ref, bim_ref,
                    are_ref, aim_ref, bbre_ref, bbim_ref):
    lre = lre_ref[...]
    lim = lim_ref[...]
    dt = jnp.exp(ldt_ref[...])
    mag = jnp.exp(lre * dt)
    a_re = mag * jnp.cos(lim * dt)
    a_im = mag * jnp.sin(lim * dt)
    den = lre * lre + lim * lim
    nr = a_re - 1.0
    ni = a_im
    f_re = (nr * lre + ni * lim) / den
    f_im = (ni * lre - nr * lim) / den
    are_ref[...] = a_re
    aim_ref[...] = a_im
    bre = bre_ref[...]
    bim = bim_ref[...]
    bbre_ref[...] = f_re[:, None, :] * bre - f_im[:, None, :] * bim
    bbim_ref[...] = f_re[:, None, :] * bim + f_im[:, None, :] * bre


def s5_discretise(lam_re, lam_im, log_dt, b_re, b_im):
    g, p = lam_re.shape
    k = b_re.shape[2]
    return pl.pallas_call(
        _s5_disc_kernel,
        out_shape=[jax.ShapeDtypeStruct((g, p), F32)] * 2 + [jax.ShapeDtypeStruct((g, k, p), F32)] * 2,
        name="s5_discretise",
    )(lam_re, lam_im, log_dt.reshape(g, 1), b_re.transpose(0, 2, 1), b_im.transpose(0, 2, 1))


def _s5_kernel(x_ref, mod_ref, nw_ref, win_ref, bbd_ref, cbd_ref, ar_ref, ai_ref, dsk_ref,
               wglu_ref, wo_ref, o_ref, st_ref, bu_ref, xs_ref):
    i = pl.program_id(1)
    tb = x_ref.shape[1]
    d = x_ref.shape[2]
    nblk = d // LANES
    nch = bu_ref.shape[0] // 2

    @pl.when(i == 0)
    def _():
        st_ref[...] = jnp.zeros_like(st_ref)

    x = x_ref[0]
    h = _norm_mod(x, nw_ref[...], mod_ref[0])
    u = jnp.dot(h.astype(BF16), win_ref[...], preferred_element_type=F32)
    ub = u.astype(BF16)
    for jb in range(nblk):
        bu = jnp.dot(ub[:, jb * LANES:(jb + 1) * LANES], bbd_ref[jb], preferred_element_type=F32)
        for ch in range(2 * nch):
            bu_ref[ch, jb * tb:(jb + 1) * tb, :] = bu[:, ch * LANES:(ch + 1) * LANES]

    ar = [ar_ref[:, ch * LANES:(ch + 1) * LANES] for ch in range(nch)]
    ai = [ai_ref[:, ch * LANES:(ch + 1) * LANES] for ch in range(nch)]

    def step(t, carry):
        rows = pl.ds(t, nblk, stride=tb)
        new = []
        for ch in range(nch):
            xr, xi = carry[ch], carry[nch + ch]
            nr = ar[ch] * xr - ai[ch] * xi + bu_ref[ch, rows, :]
            ni = ar[ch] * xi + ai[ch] * xr + bu_ref[nch + ch, rows, :]
            xs_ref[ch, rows, :] = nr
            xs_ref[nch + ch, rows, :] = ni
            new.append((nr, ni))
        return tuple(n[0] for n in new) + tuple(n[1] for n in new)

    state = lax.fori_loop(0, tb, step, tuple(st_ref[ch] for ch in range(2 * nch)), unroll=8)
    for ch in range(2 * nch):
        st_ref[ch] = state[ch]

    ys = []
    for jb in range(nblk):
        xs = jnp.concatenate([xs_ref[ch, jb * tb:(jb + 1) * tb, :] for ch in range(2 * nch)], axis=1)
        ys.append(jnp.dot(xs.astype(BF16), cbd_ref[jb], preferred_element_type=F32))
    y = jnp.concatenate(ys, axis=1) + dsk_ref[...] * u
    y = _gelu_tanh(y)
    y = y * jax.nn.sigmoid(jnp.dot(y.astype(BF16), wglu_ref[...], preferred_element_type=F32))
    out = jnp.dot(y.astype(BF16), wo_ref[...], preferred_element_type=F32)
    o_ref[0] = x + mod_ref[0][2:3] * out


def s5_layer(x, mod, nw, w_in, a_re, a_im, bb_re, bb_im, c_re, c_im, d_skip, w_glu, w_o, tb=256):
    bsz, s, d = x.shape
    g, k, p = bb_re.shape
    nblk = g // S5_GB
    eye = jnp.eye(S5_GB, dtype=F32)

    def blockdiag_in(bb):
        t = jnp.einsum('jgkp,gh->jgkhp', bb.reshape(nblk, S5_GB, k, p), eye)
        return t.reshape(nblk, S5_GB * k, S5_GB * p)

    def blockdiag_out(c):
        t = jnp.einsum('jgkp,gh->jgphk', c.reshape(nblk, S5_GB, k, p), eye)
        return t.reshape(nblk, S5_GB * p, S5_GB * k)

    bbd = jnp.concatenate([blockdiag_in(bb_re), blockdiag_in(bb_im)], axis=2).astype(BF16)
    cbd = jnp.concatenate([blockdiag_out(c_re), -blockdiag_out(c_im)], axis=1).astype(BF16)
    half = S5_GB * p
    ar = a_re.reshape(nblk, half)
    ai = a_im.reshape(nblk, half)
    row = lambda b, i: (b, i, 0)
    const2 = lambda b, i: (0, 0)
    const3 = lambda b, i: (0, 0, 0)
    return pl.pallas_call(
        _s5_kernel,
        grid=(bsz, s // tb),
        in_specs=[pl.BlockSpec((1, tb, d), row),
                  pl.BlockSpec((1, 3, d), lambda b, i: (b, 0, 0)),
                  pl.BlockSpec((1, d), const2),
                  pl.BlockSpec((d, d), const2),
                  pl.BlockSpec(bbd.shape, const3),
                  pl.BlockSpec(cbd.shape, const3),
                  pl.BlockSpec(ar.shape, const2), pl.BlockSpec(ai.shape, const2),
                  pl.BlockSpec((1, d), const2),
                  pl.BlockSpec((d, d), const2), pl.BlockSpec((d, d), const2)],
        out_specs=pl.BlockSpec((1, tb, d), row),
        out_shape=jax.ShapeDtypeStruct(x.shape, F32),
        scratch_shapes=[pltpu.VMEM((2 * half // LANES, nblk, LANES), F32),
                        pltpu.VMEM((2 * half // LANES, nblk * tb, LANES), F32),
                        pltpu.VMEM((2 * half // LANES, nblk * tb, LANES), F32)],
        compiler_params=_params(("parallel", "arbitrary")),
        name="s5_layer",
    )(x, mod, nw.reshape(1, d), w_in.astype(BF16), bbd, cbd, ar, ai, d_skip.reshape(1, d),
      w_glu.astype(BF16), w_o.astype(BF16))


def _router_kernel(x_ref, mod_ref, nw_ref, wr_ref, g_ref):
    h = _norm_mod(x_ref[0], nw_ref[...], mod_ref[0])
    logits = jnp.dot(h, wr_ref[...], precision=HIGHEST, preferred_element_type=F32)
    lane = lax.broadcasted_iota(jnp.int32, logits.shape, 1).astype(F32)
    logits = jnp.where(lane < N_EXPERTS, logits, -jnp.inf)
    m1 = jnp.max(logits, axis=1, keepdims=True)
    i1 = jnp.min(jnp.where(logits == m1, lane, float(LANES)), axis=1, keepdims=True)
    rest = jnp.where(lane == i1, -jnp.inf, logits)
    m2 = jnp.max(rest, axis=1, keepdims=True)
    i2 = jnp.min(jnp.where(rest == m2, lane, float(LANES)), axis=1, keepdims=True)
    e2 = jnp.exp(m2 - m1)
    w1 = 1.0 / (1.0 + e2)
    w2 = e2 / (1.0 + e2)
    g_ref[0] = jnp.where(lane == i1, w1, jnp.where(lane == i2, w2, 0.0))


def moe_router(x, mod, nw, w_router, tm=512):
    bsz, s, d = x.shape
    wr = jnp.zeros((d, LANES), F32).at[:, :N_EXPERTS].set(w_router)
    row = lambda b, i: (b, i, 0)
    return pl.pallas_call(
        _router_kernel,
        grid=(bsz, s // tm),
        in_specs=[pl.BlockSpec((1, tm, d), row),
                  pl.BlockSpec((1, 3, d), lambda b, i: (b, 0, 0)),
                  pl.BlockSpec((1, d), lambda b, i: (0, 0)),
                  pl.BlockSpec((d, LANES), lambda b, i: (0, 0))],
        out_specs=pl.BlockSpec((1, tm, LANES), row),
        out_shape=jax.ShapeDtypeStruct((bsz, s, LANES), F32),
        compiler_params=_params(("parallel", "parallel")),
        name="moe_router",
    )(x, mod, nw.reshape(1, d), wr)


def kernel(x, c, e_norm_mix, e_mod_mix_w, e_mod_mix_b, e_w_in, e_fox_fb, e_gla_w2, e_gla_b2, e_gla_norm, e_w_o, e_norm_ffn, e_mod_ffn_w, e_mod_ffn_b, e_ffn_gate, e_ffn_up, e_ffn_down, o_norm_mix, o_mod_mix_w, o_mod_mix_b, o_w_in, o_lam_re, o_lam_im, o_log_dt, o_b_re, o_b_im, o_c_re, o_c_im, o_d_skip, o_w_glu, o_w_o, o_norm_ffn, o_mod_ffn_w, o_mod_ffn_b, o_router, o_exp_gate, o_exp_up, o_exp_down, final_norm):
    bsz, s, d = x.shape

    mod = adaln_mod(c, e_mod_mix_w[0], e_mod_mix_b[0])
    fq, fk, fv, gq, gk, gv, gg, small, cum = layer0_inproj(x, mod, e_norm_mix[0], e_w_in[0], e_fox_fb[0])
    cum_t = cum[:, :, SMALL_FF0:SMALL_FF0 + FOX_HEADS].transpose(0, 2, 1).reshape(bsz, FOX_HEADS // 2, 2, s)
    fox = fox_attention(fq, fk, fv, cum_t)
    gla = gla_attention(gq, gk, gv, gg, small, e_gla_w2[0], e_gla_b2[0], e_gla_norm[0])
    x = mixer_outproj(x, fox, gla, e_w_o[0], mod)
    mod = adaln_mod(c, e_mod_ffn_w[0], e_mod_ffn_b[0])
    x = swiglu_ffn(x, mod, e_norm_ffn[0], e_ffn_gate.astype(BF16), e_ffn_up.astype(BF16),
                   e_ffn_down.astype(BF16))

    mod = adaln_mod(c, o_mod_mix_w[0], o_mod_mix_b[0])
    a_re, a_im, bb_re, bb_im = s5_discretise(o_lam_re[0], o_lam_im[0], o_log_dt[0], o_b_re[0], o_b_im[0])
    x = s5_layer(x, mod, o_norm_mix[0], o_w_in[0], a_re, a_im, bb_re, bb_im, o_c_re[0], o_c_im[0],
                 o_d_skip[0], o_w_glu[0], o_w_o[0])
    mod = adaln_mod(c, o_mod_ffn_w[0], o_mod_ffn_b[0])
    gates = moe_router(x, mod, o_norm_ffn[0], o_router[0])
    x = swiglu_ffn(x, mod, o_norm_ffn[0], o_exp_gate[0].astype(BF16), o_exp_up[0].astype(BF16),
                   o_exp_down[0].astype(BF16), gates=gates, final_nw=final_norm)
    return x
```

```python
import functools

import jax
import jax.numpy as jnp
from jax import lax
from jax.experimental import pallas as pl
from jax.experimental.pallas import tpu as pltpu

F32 = jnp.float32
BF16 = jnp.bfloat16
HIGHEST = lax.Precision.HIGHEST
EPS = 1e-6
NEG_BIG = -1e30
LOG2E = 1.4426950408889634
FOX_BIAS_PARTS = 3
FOX_ROWS = 128

LANES = 128
VMEM_LIMIT = 56 * 1024 * 1024

HEAD_DIM = 64
FOX_HEADS = 8
FOX_DIM = FOX_HEADS * HEAD_DIM
GLA_HEADS = 4
GLA_DK = 64
GLA_DV = 128
GLA_QK = GLA_HEADS * GLA_DK
GLA_V = GLA_HEADS * GLA_DV
GLA_RANK = 16
GLA_TAU = 16.0
GLA_CHUNK = 64
GLA_SUB = 16
S5_GROUP = 16
S5_STATE = 64
S5_GB = 8
N_EXPERTS = 8
SMALL_FF0 = 0
SMALL_LR0 = 8


def _params(sem):
    return pltpu.CompilerParams(dimension_semantics=sem, vmem_limit_bytes=VMEM_LIMIT)


def _log_sigmoid(x):
    return jnp.minimum(x, 0.0) - jnp.log(1.0 + jnp.exp(-jnp.abs(x)))


def _silu(x):
    return x * jax.nn.sigmoid(x)


def _gelu_tanh(x):
    c = 0.7978845608028654
    return 0.5 * x * (1.0 + jnp.tanh(c * (x + 0.044715 * (x * x * x))))


def _rms(x, w):
    return x * lax.rsqrt(jnp.mean(x * x, axis=-1, keepdims=True) + EPS) * w


def _norm_mod(x, nw, mod):
    return _rms(x, nw) * (1.0 + mod[1:2]) + mod[0:1]


def _nt_dot(a, b):
    return lax.dot_general(a, b, (((1,), (1,)), ((), ())), preferred_element_type=F32)


def _mod_kernel(ct_ref, w_ref, b_ref, o_ref):
    s = _silu(ct_ref[...])
    w = w_ref[...]
    for b in range(s.shape[1]):
        o_ref[b:b + 1, :] = jnp.sum(s[:, b:b + 1] * w, axis=0, keepdims=True) + b_ref[...]


def adaln_mod(c, w, b):
    bsz, d = c.shape
    n = w.shape[1]
    tn = 512
    out = pl.pallas_call(
        _mod_kernel,
        grid=(n // tn,),
        in_specs=[pl.BlockSpec((d, bsz), lambda j: (0, 0)),
                  pl.BlockSpec((d, tn), lambda j: (0, j)),
                  pl.BlockSpec((1, tn), lambda j: (0, j))],
        out_specs=pl.BlockSpec((bsz, tn), lambda j: (0, j)),
        out_shape=jax.ShapeDtypeStruct((bsz, n), F32),
        compiler_params=_params(("parallel",)),
        name="adaln_mod",
    )(c.T, w, b.reshape(1, n))
    return out.reshape(bsz, 3, d)


def _inproj_kernel(x_ref, mod_ref, nw_ref, wbig_ref, wsm_ref, fb_ref, tri_ref,
                   fq_ref, fk_ref, fv_ref, gq_ref, gk_ref, gv_ref, gg_ref, sm_ref, cum_ref,
                   carry_ref):
    i = pl.program_id(1)
    tm = x_ref.shape[1]
    h = _norm_mod(x_ref[0], nw_ref[...], mod_ref[0])
    hb = h.astype(BF16)
    col = 0
    for ref in (fq_ref, fk_ref, fv_ref, gq_ref, gk_ref, gv_ref, gg_ref):
        n = ref.shape[2]
        y = jnp.dot(hb, wbig_ref[:, col:col + n], preferred_element_type=F32)
        if ref is fq_ref:
            y = y * (LOG2E * HEAD_DIM ** -0.5)
        ref[0] = y.astype(ref.dtype)
        col += n
    small = jnp.dot(h, wsm_ref[...], precision=HIGHEST, preferred_element_type=F32)
    sm_ref[0] = small
    logf = _log_sigmoid(small + fb_ref[...])

    @pl.when(i == 0)
    def _():
        carry_ref[...] = jnp.zeros_like(carry_ref)

    cum = jnp.dot(tri_ref[...], logf, precision=HIGHEST, preferred_element_type=F32) + carry_ref[...]
    cum_ref[0] = cum
    carry_ref[...] = cum[tm - 1:tm, :]


def layer0_inproj(x, mod, nw, w_in, fox_fb, tm=512):
    bsz, s, d = x.shape
    c0 = 3 * FOX_DIM
    c_ff = c0
    c_g = c_ff + FOX_HEADS
    c_lr = c_g + 2 * GLA_QK + 2 * GLA_V
    wbig = jnp.concatenate([w_in[:, :c0], w_in[:, c_g:c_lr]], axis=1).astype(BF16)
    wsm = jnp.zeros((d, LANES), F32)
    wsm = wsm.at[:, SMALL_FF0:SMALL_FF0 + FOX_HEADS].set(w_in[:, c_ff:c_g])
    wsm = wsm.at[:, SMALL_LR0:SMALL_LR0 + GLA_RANK].set(w_in[:, c_lr:])
    fb = jnp.zeros((1, LANES), F32).at[0, SMALL_FF0:SMALL_FF0 + FOX_HEADS].set(fox_fb)
    tri = jnp.tril(jnp.ones((tm, tm), F32))
    widths = (FOX_DIM, FOX_DIM, FOX_DIM, GLA_QK, GLA_QK, GLA_V, GLA_V)
    row = lambda b, i: (b, i, 0)
    const2 = lambda b, i: (0, 0)
    outs = pl.pallas_call(
        _inproj_kernel,
        grid=(bsz, s // tm),
        in_specs=[pl.BlockSpec((1, tm, d), row),
                  pl.BlockSpec((1, 3, d), lambda b, i: (b, 0, 0)),
                  pl.BlockSpec((1, d), const2),
                  pl.BlockSpec(wbig.shape, const2),
                  pl.BlockSpec(wsm.shape, const2),
                  pl.BlockSpec((1, LANES), const2),
                  pl.BlockSpec((tm, tm), const2)],
        out_specs=[pl.BlockSpec((1, tm, n), row) for n in widths]
                  + [pl.BlockSpec((1, tm, LANES), row)] * 2,
        out_shape=[jax.ShapeDtypeStruct((bsz, s, n), BF16) for n in widths]
                  + [jax.ShapeDtypeStruct((bsz, s, LANES), F32)] * 2,
        scratch_shapes=[pltpu.VMEM((1, LANES), F32)],
        compiler_params=_params(("parallel", "arbitrary")),
        name="layer0_inproj",
    )(x, mod, nw.reshape(1, d), wbig, wsm, fb, tri)
    return outs


def _fox_kernel(q_ref, k_ref, v_ref, c_ref, o_ref, kaug_ref, vaug_ref, m_ref, acc_ref, qaug_ref):
    pair = pl.program_id(1)
    i = pl.program_id(2)
    t = q_ref.shape[1]
    lane = lax.broadcasted_iota(jnp.int32, (1, LANES), 1)
    heads = (lane < HEAD_DIM, lane >= HEAD_DIM)
    spare = (HEAD_DIM, 0)

    @pl.when(i == 0)
    def _():
        k = k_ref[0].astype(F32)
        v = v_ref[0].astype(F32)
        cum = c_ref[0]
        for h in range(2):
            bias = jnp.sum(jnp.where(lane == 2 * pair + h, cum, 0.0), axis=1, keepdims=True) * (-LOG2E)
            ka = jnp.where(heads[h], k, 0.0)
            rest = bias
            for n in range(FOX_BIAS_PARTS):
                piece = rest.astype(BF16).astype(F32)
                rest = rest - piece
                ka = jnp.where(lane == spare[h] + n, piece, ka)
            kaug_ref[h] = ka.astype(BF16)
            vaug_ref[h, :, :LANES] = jnp.where(heads[h], v, 0.0).astype(BF16)
            vaug_ref[h, :, LANES:] = jnp.broadcast_to(jnp.where(heads[h], 1.0, 0.0), v.shape).astype(BF16)

    q = q_ref[0].astype(F32)
    for h in range(2):
        ones_here = jnp.where(lane >= spare[h], jnp.where(lane < spare[h] + FOX_BIAS_PARTS, 1.0, 0.0), 0.0)
        qaug_ref[h] = jnp.where(heads[h], q, ones_here).astype(BF16)
    m_ref[...] = jnp.full_like(m_ref, NEG_BIG)
    acc_ref[...] = jnp.zeros_like(acc_ref)

    def tile(j, diagonal):
        keys = pl.ds(pl.multiple_of(j * t, t), t)
        for r0 in range(0, t, FOX_ROWS):
            rows = slice(r0, r0 + FOX_ROWS)
            alphas = []
            upd = None
            for h in range(2):
                s = _nt_dot(qaug_ref[h, rows, :], kaug_ref[h, keys, :])
                if diagonal:
                    r = r0 + lax.broadcasted_iota(jnp.int32, (FOX_ROWS, t), 0)
                    c = lax.broadcasted_iota(jnp.int32, (FOX_ROWS, t), 1)
                    s = jnp.where(c <= r, s, NEG_BIG)
                m_prev = m_ref[h, rows, :]
                m_next = jnp.maximum(m_prev, jnp.max(s, axis=1, keepdims=True))
                m_ref[h, rows, :] = m_next
                p = jnp.exp2(s - jnp.concatenate([m_next] * (t // LANES), axis=1)).astype(BF16)
                alphas.append(jnp.exp2(m_prev - m_next))
                d = jnp.dot(p, vaug_ref[h, keys, :], preferred_element_type=F32)
                upd = d if upd is None else upd + d
            alpha = jnp.where(heads[0], alphas[0], alphas[1])
            acc_ref[rows, :] = jnp.concatenate([alpha, alpha], axis=1) * acc_ref[rows, :] + upd

    def body(j, carry):
        tile(j, False)
        return carry

    lax.fori_loop(0, i, body, 0)
    tile(i, True)
    o_ref[0] = (acc_ref[:, :LANES] / acc_ref[:, LANES:]).astype(o_ref.dtype)


def fox_attention(fq, fk, fv, cum, t=512):
    bsz, s, _ = fq.shape
    n_pair = FOX_HEADS // 2
    seq = lambda b, p, i: (b, 0, p)
    return pl.pallas_call(
        _fox_kernel,
        grid=(bsz, n_pair, s // t),
        in_specs=[pl.BlockSpec((1, t, LANES), lambda b, p, i: (b, i, p)),
                  pl.BlockSpec((1, s, LANES), seq),
                  pl.BlockSpec((1, s, LANES), seq),
                  pl.BlockSpec((1, s, LANES), lambda b, p, i: (b, 0, 0))],
        out_specs=pl.BlockSpec((1, t, LANES), lambda b, p, i: (b, i, p)),
        out_shape=jax.ShapeDtypeStruct((bsz, s, FOX_DIM), BF16),
        scratch_shapes=[pltpu.VMEM((2, s, LANES), BF16), pltpu.VMEM((2, s, 2 * LANES), BF16),
                        pltpu.VMEM((2, t, LANES), F32), pltpu.VMEM((t, 2 * LANES), F32),
                        pltpu.VMEM((2, t, LANES), BF16)],
        compiler_params=_params(("parallel", "parallel", "arbitrary")),
        name="fox_attention",
    )(fq, fk, fv, cum)


def _gla_pair_chunk(q, k, v, la, st, tri, eye, k_scr, b_scr):
    C, SUB = GLA_CHUNK, GLA_SUB
    nsub = C // SUB
    lane = lax.broadcasted_iota(jnp.int32, (1, LANES), 1)
    head_a = lane < GLA_DK
    b = jnp.dot(tri, la, precision=HIGHEST, preferred_element_type=F32)
    b_last = b[C - 1:C, :]
    k_scr[...] = k
    b_scr[...] = b
    rowblk = lax.broadcasted_iota(jnp.int32, (C, 1), 0) // SUB

    refs = [b[m * SUB - 1:m * SUB, :] for m in range(1, nsub)]
    rsel = refs[-1]
    for m in range(nsub - 2, 0, -1):
        rsel = jnp.where(rowblk == m, refs[m - 1], rsel)
    qt = q * jnp.exp(jnp.minimum(b - rsel, 0.0))
    zero = jnp.zeros_like(q)
    qs = jnp.concatenate([jnp.where(rowblk == m, qt, zero) for m in range(1, nsub)], axis=1)
    ks = jnp.concatenate(
        [jnp.where(rowblk < m, k * jnp.exp(jnp.minimum(refs[m - 1] - b, 0.0)), zero)
         for m in range(1, nsub)], axis=1)
    lane3 = lax.broadcasted_iota(jnp.int32, (1, (nsub - 1) * LANES), 1)
    head_a3 = (lane3 & (LANES - 1)) < GLA_DK
    zero3 = jnp.zeros_like(ks)
    kstack = jnp.concatenate([jnp.where(head_a3, ks, zero3), jnp.where(head_a3, zero3, ks)], axis=0)
    a_off = _nt_dot(qs.astype(BF16), kstack.astype(BF16))

    row16 = lax.broadcasted_iota(jnp.int32, (SUB, 1), 0)
    blocks = []
    for i in range(nsub):
        qb = q[i * SUB:(i + 1) * SUB, :]
        bb = b[i * SUB:(i + 1) * SUB, :]
        d = jnp.zeros((SUB, LANES), F32)
        for sp in range(SUB):
            srow = i * SUB + sp
            e = qb * k_scr[srow:srow + 1, :] * jnp.exp(jnp.minimum(bb - b_scr[srow:srow + 1, :], 0.0))
            da = jnp.sum(jnp.where(head_a, e, 0.0), axis=1, keepdims=True)
            db = jnp.sum(jnp.where(head_a, 0.0, e), axis=1, keepdims=True)
            live = row16 >= sp
            da = jnp.where(live, da, 0.0)
            db = jnp.where(live, db, 0.0)
            d = jnp.where(lane == srow, da, jnp.where(lane == GLA_DK + srow, db, d))
        blocks.append(d)
    a_pair = a_off + jnp.concatenate(blocks, axis=0)

    lane_v = lax.broadcasted_iota(jnp.int32, (1, 2 * GLA_DV), 1)
    first_v = lane_v < GLA_DV
    zv = jnp.zeros_like(v)
    vbd = jnp.concatenate([jnp.where(first_v, v, zv), jnp.where(first_v, zv, v)], axis=0)
    qh = q * jnp.exp(b)
    o = (jnp.dot(a_pair.astype(BF16), vbd, preferred_element_type=F32)
         + _nt_dot(qh.astype(BF16), st.astype(BF16)))

    k_end = k * jnp.exp(b_last - b)
    v_t = _nt_dot(eye, v).astype(BF16)
    inc = jnp.dot(v_t, k_end.astype(BF16), preferred_element_type=F32)
    row_v = lax.broadcasted_iota(jnp.int32, (2 * GLA_DV, 1), 0)
    same_head = (row_v // GLA_DV) == (lane // GLA_DK)
    st_new = st * jnp.exp(b_last) + jnp.where(same_head, inc, 0.0)
    return o, st_new


def _gla_kernel(gq_ref, gk_ref, gv_ref, gg_ref, sm_ref, w2_ref, b2_ref, gn_ref, tri_ref, eye_ref,
                o_ref, st_ref, la_ref, k_scr, b_scr):
    i = pl.program_id(1)
    tm = gq_ref.shape[1]
    n_pair = GLA_HEADS // 2
    scale = GLA_DK ** -0.5

    @pl.when(i == 0)
    def _():
        st_ref[...] = jnp.zeros_like(st_ref)

    z = jnp.dot(sm_ref[0], w2_ref[...], precision=HIGHEST, preferred_element_type=F32) + b2_ref[...]
    la_ref[...] = _log_sigmoid(z) * (1.0 / GLA_TAU)
    gn = gn_ref[...]

    def chunk(c, carry):
        base = pl.multiple_of(c * GLA_CHUNK, GLA_CHUNK)
        rows = pl.ds(base, GLA_CHUNK)
        for hp in range(n_pair):
            ql = slice(hp * LANES, (hp + 1) * LANES)
            vl = slice(hp * 2 * GLA_DV, (hp + 1) * 2 * GLA_DV)
            q = gq_ref[0, rows, ql].astype(F32) * scale
            k = gk_ref[0, rows, ql].astype(F32)
            v = gv_ref[0, rows, vl]
            o, st_new = _gla_pair_chunk(q, k, v, la_ref[rows, ql], st_ref[hp], tri_ref[...],
                                        eye_ref[...], k_scr.at[hp], b_scr.at[hp])
            st_ref[hp] = st_new
            gate = _silu(gg_ref[0, rows, vl].astype(F32))
            halves = [_rms(o[:, h * GLA_DV:(h + 1) * GLA_DV], gn) for h in range(2)]
            o_ref[0, rows, vl] = (jnp.concatenate(halves, axis=1) * gate).astype(o_ref.dtype)
        return carry

    lax.fori_loop(0, tm // GLA_CHUNK, chunk, 0)


def gla_attention(gq, gk, gv, gg, small, w2, b2, gnorm, tm=512):
    bsz, s, _ = gq.shape
    w2p = jnp.zeros((LANES, GLA_QK), F32).at[SMALL_LR0:SMALL_LR0 + GLA_RANK].set(w2)
    tri = jnp.tril(jnp.ones((GLA_CHUNK, GLA_CHUNK), F32))
    eye = jnp.eye(2 * GLA_DV, dtype=BF16)
    row = lambda b, i: (b, i, 0)
    const2 = lambda b, i: (0, 0)
    return pl.pallas_call(
        _gla_kernel,
        grid=(bsz, s // tm),
        in_specs=[pl.BlockSpec((1, tm, GLA_QK), row), pl.BlockSpec((1, tm, GLA_QK), row),
                  pl.BlockSpec((1, tm, GLA_V), row), pl.BlockSpec((1, tm, GLA_V), row),
                  pl.BlockSpec((1, tm, LANES), row),
                  pl.BlockSpec(w2p.shape, const2), pl.BlockSpec((1, GLA_QK), const2),
                  pl.BlockSpec((1, GLA_DV), const2), pl.BlockSpec(tri.shape, const2),
                  pl.BlockSpec(eye.shape, const2)],
        out_specs=pl.BlockSpec((1, tm, GLA_V), row),
        out_shape=jax.ShapeDtypeStruct((bsz, s, GLA_V), BF16),
        scratch_shapes=[pltpu.VMEM((GLA_HEADS // 2, 2 * GLA_DV, LANES), F32),
                        pltpu.VMEM((tm, GLA_QK), F32),
                        pltpu.VMEM((GLA_HEADS // 2, GLA_CHUNK, LANES), F32),
                        pltpu.VMEM((GLA_HEADS // 2, GLA_CHUNK, LANES), F32)],
        compiler_params=_params(("parallel", "arbitrary")),
        name="gla_attention",
    )(gq, gk, gv, gg, small, w2p, b2.reshape(1, GLA_QK), gnorm.reshape(1, GLA_DV), tri, eye)


def _mix_ffn_kernel(x_ref, a_ref, b_ref, wo_ref, modm_ref, modf_ref, nw_ref, wg_ref, wu_ref, wd_ref,
                    o_ref, x1_ref, h_ref, acc_ref):
    j = pl.program_id(2)

    @pl.when(j == 0)
    def _():
        na = a_ref.shape[2]
        y = (jnp.dot(a_ref[0], wo_ref[:na, :], preferred_element_type=F32)
             + jnp.dot(b_ref[0], wo_ref[na:, :], preferred_element_type=F32))
        x1 = x_ref[0] + modm_ref[0][2:3] * y
        x1_ref[...] = x1
        h_ref[...] = _norm_mod(x1, nw_ref[...], modf_ref[0]).astype(BF16)
        acc_ref[...] = jnp.zeros_like(acc_ref)

    hb = h_ref[...]
    act = (_silu(jnp.dot(hb, wg_ref[...], preferred_element_type=F32))
           * jnp.dot(hb, wu_ref[...], preferred_element_type=F32))
    acc_ref[...] += jnp.dot(act.astype(BF16), wd_ref[...], preferred_element_type=F32)

    @pl.when(j == pl.num_programs(2) - 1)
    def _():
        o_ref[0] = x1_ref[...] + modf_ref[0][2:3] * acc_ref[...]


def mixer_out_and_ffn(x, a, b, w_o, mod_mix, mod_ffn, nw, w_gate, w_up, w_down, tm=1024, tf=256):
    bsz, s, d = x.shape
    f = w_gate.shape[1]
    row = lambda bb, i, j: (bb, i, 0)
    per_b = lambda bb, i, j: (bb, 0, 0)
    const2 = lambda bb, i, j: (0, 0)
    return pl.pallas_call(
        _mix_ffn_kernel,
        grid=(bsz, s // tm, f // tf),
        in_specs=[pl.BlockSpec((1, tm, d), row),
                  pl.BlockSpec((1, tm, a.shape[2]), row),
                  pl.BlockSpec((1, tm, b.shape[2]), row),
                  pl.BlockSpec(w_o.shape, const2),
                  pl.BlockSpec((1, 3, d), per_b),
                  pl.BlockSpec((1, 3, d), per_b),
                  pl.BlockSpec((1, d), const2),
                  pl.BlockSpec((d, tf), lambda bb, i, j: (0, j)),
                  pl.BlockSpec((d, tf), lambda bb, i, j: (0, j)),
                  pl.BlockSpec((tf, d), lambda bb, i, j: (j, 0))],
        out_specs=pl.BlockSpec((1, tm, d), row),
        out_shape=jax.ShapeDtypeStruct(x.shape, F32),
        scratch_shapes=[pltpu.VMEM((tm, d), F32), pltpu.VMEM((tm, d), BF16), pltpu.VMEM((tm, d), F32)],
        compiler_params=_params(("parallel", "parallel", "arbitrary")),
        name="mixer_out_and_ffn",
    )(x, a, b, w_o, mod_mix, mod_ffn, nw.reshape(1, d), w_gate, w_up, w_down)


def _s5_disc_kernel(lre_ref, lim_ref, ldt_ref, bre_ref, bim_ref,
                    are_ref, aim_ref, bbre_ref, bbim_ref):
    lre = lre_ref[...]
    lim = lim_ref[...]
    dt = jnp.exp(ldt_ref[...])
    mag = jnp.exp(lre * dt)
    a_re = mag * jnp.cos(lim * dt)
    a_im = mag * jnp.sin(lim * dt)
    den = lre * lre + lim * lim
    nr = a_re - 1.0
    ni = a_im
    f_re = (nr * lre + ni * lim) / den
    f_im = (ni * lre - nr * lim) / den
    are_ref[...] = a_re
    aim_ref[...] = a_im
    bre = bre_ref[...]
    bim = bim_ref[...]
    bbre_ref[...] = f_re[:, None, :] * bre - f_im[:, None, :] * bim
    bbim_ref[...] = f_re[:, None, :] * bim + f_im[:, None, :] * bre


def s5_discretise(lam_re, lam_im, log_dt, b_re, b_im):
    g, p = lam_re.shape
    k = b_re.shape[2]
    return pl.pallas_call(
        _s5_disc_kernel,
        out_shape=[jax.ShapeDtypeStruct((g, p), F32)] * 2 + [jax.ShapeDtypeStruct((g, k, p), F32)] * 2,
        name="s5_discretise",
    )(lam_re, lam_im, log_dt.reshape(g, 1), b_re.transpose(0, 2, 1), b_im.transpose(0, 2, 1))


def _s5_kernel(x_ref, mod_ref, nw_ref, win_ref, bbd_ref, cbd_ref, ar_ref, ai_ref, dsk_ref,
               wglu_ref, wo_ref, o_ref, st_ref, bu_ref, xs_ref):
    i = pl.program_id(1)
    tb = x_ref.shape[1]
    d = x_ref.shape[2]
    nblk = d // LANES
    nch = bu_ref.shape[0] // 2

    @pl.when(i == 0)
    def _():
        st_ref[...] = jnp.zeros_like(st_ref)

    x = x_ref[0]
    h = _norm_mod(x, nw_ref[...], mod_ref[0])
    u = jnp.dot(h.astype(BF16), win_ref[...], preferred_element_type=F32)
    ub = u.astype(BF16)
    for jb in range(nblk):
        bu = jnp.dot(ub[:, jb * LANES:(jb + 1) * LANES], bbd_ref[jb], preferred_element_type=F32)
        for ch in range(2 * nch):
            bu_ref[ch, jb * tb:(jb + 1) * tb, :] = bu[:, ch * LANES:(ch + 1) * LANES]

    ar = [ar_ref[:, ch * LANES:(ch + 1) * LANES] for ch in range(nch)]
    ai = [ai_ref[:, ch * LANES:(ch + 1) * LANES] for ch in range(nch)]

    def step(t, carry):
        rows = pl.ds(t, nblk, stride=tb)
        new = []
        for ch in range(nch):
            xr, xi = carry[ch], carry[nch + ch]
            nr = ar[ch] * xr - ai[ch] * xi + bu_ref[ch, rows, :]
            ni = ar[ch] * xi + ai[ch] * xr + bu_ref[nch + ch, rows, :]
            xs_ref[ch, rows, :] = nr
            xs_ref[nch + ch, rows, :] = ni
            new.append((nr, ni))
        return tuple(n[0] for n in new) + tuple(n[1] for n in new)

    state = lax.fori_loop(0, tb, step, tuple(st_ref[ch] for ch in range(2 * nch)), unroll=8)
    for ch in range(2 * nch):
        st_ref[ch] = state[ch]

    ys = []
    for jb in range(nblk):
        xs = jnp.concatenate([xs_ref[ch, jb * tb:(jb + 1) * tb, :] for ch in range(2 * nch)], axis=1)
        ys.append(jnp.dot(xs.astype(BF16), cbd_ref[jb], preferred_element_type=F32))
    y = jnp.concatenate(ys, axis=1) + dsk_ref[...] * u
    y = _gelu_tanh(y)
    y = y * jax.nn.sigmoid(jnp.dot(y.astype(BF16), wglu_ref[...], preferred_element_type=F32))
    out = jnp.dot(y.astype(BF16), wo_ref[...], preferred_element_type=F32)
    o_ref[0] = x + mod_ref[0][2:3] * out


def s5_layer(x, mod, nw, w_in, a_re, a_im, bb_re, bb_im, c_re, c_im, d_skip, w_glu, w_o, tb=256):
    bsz, s, d = x.shape
    g, k, p = bb_re.shape
    nblk = g // S5_GB
    eye = jnp.eye(S5_GB, dtype=F32)

    def blockdiag_in(bb):
        t = jnp.einsum('jgkp,gh->jgkhp', bb.reshape(nblk, S5_GB, k, p), eye)
        return t.reshape(nblk, S5_GB * k, S5_GB * p)

    def blockdiag_out(c):
        t = jnp.einsum('jgkp,gh->jgphk', c.reshape(nblk, S5_GB, k, p), eye)
        return t.reshape(nblk, S5_GB * p, S5_GB * k)

    bbd = jnp.concatenate([blockdiag_in(bb_re), blockdiag_in(bb_im)], axis=2).astype(BF16)
    cbd = jnp.concatenate([blockdiag_out(c_re), -blockdiag_out(c_im)], axis=1).astype(BF16)
    half = S5_GB * p
    ar = a_re.reshape(nblk, half)
    ai = a_im.reshape(nblk, half)
    row = lambda b, i: (b, i, 0)
    const2 = lambda b, i: (0, 0)
    const3 = lambda b, i: (0, 0, 0)
    return pl.pallas_call(
        _s5_kernel,
        grid=(bsz, s // tb),
        in_specs=[pl.BlockSpec((1, tb, d), row),
                  pl.BlockSpec((1, 3, d), lambda b, i: (b, 0, 0)),
                  pl.BlockSpec((1, d), const2),
                  pl.BlockSpec((d, d), const2),
                  pl.BlockSpec(bbd.shape, const3),
                  pl.BlockSpec(cbd.shape, const3),
                  pl.BlockSpec(ar.shape, const2), pl.BlockSpec(ai.shape, const2),
                  pl.BlockSpec((1, d), const2),
                  pl.BlockSpec((d, d), const2), pl.BlockSpec((d, d), const2)],
        out_specs=pl.BlockSpec((1, tb, d), row),
        out_shape=jax.ShapeDtypeStruct(x.shape, F32),
        scratch_shapes=[pltpu.VMEM((2 * half // LANES, nblk, LANES), F32),
                        pltpu.VMEM((2 * half // LANES, nblk * tb, LANES), F32),
                        pltpu.VMEM((2 * half // LANES, nblk * tb, LANES), F32)],
        compiler_params=_params(("parallel", "arbitrary")),
        name="s5_layer",
    )(x, mod, nw.reshape(1, d), w_in.astype(BF16), bbd, cbd, ar, ai, d_skip.reshape(1, d),
      w_glu.astype(BF16), w_o.astype(BF16))


TOK_E, TOK_RANK, TOK_W = 0, 2, 4
MOE_TILE = 512


def _router_kernel(x_ref, mod_ref, nw_ref, wr_ref, tri_ref, sel_ref,
                   hb_ref, tok_ref, tokt_ref, before_ref, total_ref, carry_ref):
    blk = pl.program_id(0)
    h = _norm_mod(x_ref[0], nw_ref[...], mod_ref[0])
    hb_ref[0] = h.astype(BF16)
    logits = jnp.dot(h, wr_ref[...], precision=HIGHEST, preferred_element_type=F32)
    lane = lax.broadcasted_iota(jnp.int32, logits.shape, 1).astype(F32)
    logits = jnp.where(lane < N_EXPERTS, logits, -jnp.inf)
    m1 = jnp.max(logits, axis=1, keepdims=True)
    i1 = jnp.min(jnp.where(logits == m1, lane, float(LANES)), axis=1, keepdims=True)
    rest = jnp.where(lane == i1, -jnp.inf, logits)
    m2 = jnp.max(rest, axis=1, keepdims=True)
    i2 = jnp.min(jnp.where(rest == m2, lane, float(LANES)), axis=1, keepdims=True)
    e2 = jnp.exp(m2 - m1)
    w1 = 1.0 / (1.0 + e2)
    w2 = e2 / (1.0 + e2)

    @pl.when(blk == 0)
    def _():
        carry_ref[...] = jnp.zeros_like(carry_ref)

    routed = jnp.where(lane == i1, 1.0, jnp.where(lane == i2, 1.0, 0.0))
    before = carry_ref[...]
    rank = jnp.dot(tri_ref[...], routed.astype(BF16), preferred_element_type=F32) + before
    r1 = jnp.sum(jnp.where(lane == i1, rank, 0.0), axis=1, keepdims=True)
    r2 = jnp.sum(jnp.where(lane == i2, rank, 0.0), axis=1, keepdims=True)
    fields = ((TOK_E, i1), (TOK_E + 1, i2), (TOK_RANK, r1), (TOK_RANK + 1, r2), (TOK_W, w1), (TOK_W + 1, w2))
    tok = jnp.zeros_like(logits)
    for ln, val in fields:
        tok = jnp.where(lane == ln, val, tok)
    tok_ref[0] = tok
    tokt_ref[0] = _nt_dot_f32(sel_ref[...], tok)
    before_ref[0] = before
    total = before + jnp.sum(routed, axis=0, keepdims=True)
    carry_ref[...] = total
    total_ref[...] = total


def _nt_dot_f32(a, b):
    return lax.dot_general(a, b, (((1,), (1,)), ((), ())), precision=HIGHEST, preferred_element_type=F32)


def moe_router(x, mod, nw, w_router):
    bsz, s, d = x.shape
    nb = MOE_TILE
    per_b = s // nb
    nblk = bsz * per_b
    wr = jnp.zeros((d, LANES), F32).at[:, :N_EXPERTS].set(w_router)
    tri = jnp.tril(jnp.ones((nb, nb), BF16), k=-1)
    sel = jnp.eye(8, LANES, dtype=F32)
    const2 = lambda i: (0, 0)
    return pl.pallas_call(
        _router_kernel,
        grid=(nblk,),
        in_specs=[pl.BlockSpec((1, nb, d), lambda i: (i // per_b, i % per_b, 0)),
                  pl.BlockSpec((1, 3, d), lambda i: (i // per_b, 0, 0)),
                  pl.BlockSpec((1, d), const2),
                  pl.BlockSpec((d, LANES), const2),
                  pl.BlockSpec((nb, nb), const2),
                  pl.BlockSpec((8, LANES), const2)],
        out_specs=[pl.BlockSpec((1, nb, d), lambda i: (i, 0, 0)),
                   pl.BlockSpec((1, nb, LANES), lambda i: (i, 0, 0)),
                   pl.BlockSpec((1, 8, nb), lambda i: (i, 0, 0)),
                   pl.BlockSpec((1, 1, LANES), lambda i: (i, 0, 0)),
                   pl.BlockSpec((1, LANES), const2)],
        out_shape=[jax.ShapeDtypeStruct((nblk, nb, d), BF16),
                   jax.ShapeDtypeStruct((nblk, nb, LANES), F32),
                   jax.ShapeDtypeStruct((nblk, 8, nb), F32),
                   jax.ShapeDtypeStruct((nblk, 1, LANES), F32),
                   jax.ShapeDtypeStruct((1, LANES), F32)],
        scratch_shapes=[pltpu.VMEM((1, LANES), F32)],
        compiler_params=_params(("arbitrary",)),
        name="moe_router",
    )(x, mod, nw.reshape(1, d), wr, tri, sel)


def _moe_plan(before, total, n_tiles, max_pairs):
    tm = MOE_TILE
    nblk = before.shape[0]
    ntile = (total + tm - 1) // tm
    tile_end = jnp.cumsum(ntile)
    tile_start = tile_end - ntile
    n_valid = tile_end[-1]
    r = jnp.arange(n_tiles, dtype=jnp.int32)
    texp = jnp.minimum(jnp.sum((tile_end[None, :] <= r[:, None]).astype(jnp.int32), axis=1), N_EXPERTS - 1)
    r0 = (r - tile_start[texp]) * tm
    after = jnp.concatenate([before[1:], total[None, :]], axis=0)
    lo = before[:, texp].T
    hi = after[:, texp].T
    meets = (lo < (r0 + tm)[:, None]) & (hi > r0[:, None]) & (r < n_valid)[:, None]
    n_pairs = jnp.sum(meets).astype(jnp.int32)
    slot = jnp.arange(max_pairs, dtype=jnp.int32)

    def pair_list(mat):
        flat = jnp.nonzero(mat.reshape(-1), size=max_pairs, fill_value=0)[0].astype(jnp.int32)
        return jnp.where(slot < n_pairs, flat, flat[jnp.maximum(n_pairs - 1, 0)])

    by_tile = pair_list(meets)
    by_blk = pair_list(meets.T)
    return dict(texp=texp.astype(jnp.int32), n_valid=n_valid.astype(jnp.int32).reshape(1),
                off=(tile_start * tm).astype(jnp.int32), n_pairs=n_pairs.reshape(1),
                g_tile=by_tile // nblk, g_blk=by_tile % nblk,
                c_blk=by_blk // n_tiles, c_tile=by_blk % n_tiles)


def _slot_positions(tok_fields, k, off_ref, along_lanes):
    if along_lanes:
        e = tok_fields[TOK_E + k:TOK_E + k + 1, :]
        rank = tok_fields[TOK_RANK + k:TOK_RANK + k + 1, :]
    else:
        e = tok_fields[:, TOK_E + k:TOK_E + k + 1]
        rank = tok_fields[:, TOK_RANK + k:TOK_RANK + k + 1]
    pos = rank
    for ex in range(N_EXPERTS):
        pos = pos + jnp.where(e == float(ex), off_ref[ex].astype(F32), 0.0)
    return pos


def _gather_kernel(tile_ref, blk_ref, np_ref, off_ref, tokt_ref, tok_ref, hb_ref,
                   hs_ref, ws_ref, acc_ref, wacc_ref):
    p = pl.program_id(0)
    tm = hs_ref.shape[0]
    r = tile_ref[p]
    prev = tile_ref[jnp.maximum(p - 1, 0)]
    live = p < np_ref[0]

    @pl.when(jnp.logical_or(p == 0, prev != r))
    def _():
        acc_ref[...] = jnp.zeros_like(acc_ref)
        wacc_ref[...] = jnp.zeros_like(wacc_ref)

    @pl.when(live)
    def _():
        tokt = tokt_ref[0]
        tok = tok_ref[0]
        lane = lax.broadcasted_iota(jnp.int32, (1, LANES), 1)
        rowid = (r * tm + lax.broadcasted_iota(jnp.int32, (tm, 1), 0)).astype(F32)
        onehots = [jnp.where(_slot_positions(tokt, k, off_ref, True) == rowid, 1.0, 0.0) for k in range(2)]
        acc_ref[...] += jnp.dot((onehots[0] + onehots[1]).astype(BF16), hb_ref[0], preferred_element_type=F32)
        for k in range(2):
            picked = jnp.dot(onehots[k], tok, precision=HIGHEST, preferred_element_type=F32)
            wacc_ref[...] += jnp.where(lane == TOK_W + k, picked, 0.0)
        hs_ref[...] = acc_ref[...].astype(hs_ref.dtype)
        ws_ref[...] = wacc_ref[...]


def moe_gather(plan, tokt, tok, hb, n_tiles, max_pairs):
    nblk, nb, d = hb.shape
    tm = MOE_TILE
    by_blk3 = lambda p, t, b, n, o: (b[p], 0, 0)
    by_tile2 = lambda p, t, b, n, o: (t[p], 0)
    return pl.pallas_call(
        _gather_kernel,
        grid_spec=pltpu.PrefetchScalarGridSpec(
            num_scalar_prefetch=4,
            grid=(max_pairs,),
            in_specs=[pl.BlockSpec((1, 8, nb), by_blk3),
                      pl.BlockSpec((1, nb, LANES), by_blk3),
                      pl.BlockSpec((1, nb, d), by_blk3)],
            out_specs=[pl.BlockSpec((tm, d), by_tile2), pl.BlockSpec((tm, LANES), by_tile2)],
            scratch_shapes=[pltpu.VMEM((tm, d), F32), pltpu.VMEM((tm, LANES), F32)]),
        out_shape=[jax.ShapeDtypeStruct((n_tiles * tm, d), BF16),
                   jax.ShapeDtypeStruct((n_tiles * tm, LANES), F32)],
        compiler_params=_params(("arbitrary",)),
        name="moe_gather",
    )(plan["g_tile"], plan["g_blk"], plan["n_pairs"], plan["off"], tokt, tok, hb)


def _experts_kernel(texp_ref, nvalid_ref, hs_ref, ws_ref, wg_ref, wu_ref, wd_ref, ys_ref, acc_ref):
    r = pl.program_id(0)
    j = pl.program_id(1)

    @pl.when(j == 0)
    def _():
        acc_ref[...] = jnp.zeros_like(acc_ref)

    @pl.when(r < nvalid_ref[0])
    def _():
        hb = hs_ref[...]
        act = (_silu(jnp.dot(hb, wg_ref[0], preferred_element_type=F32))
               * jnp.dot(hb, wu_ref[0], preferred_element_type=F32))
        acc_ref[...] += jnp.dot(act.astype(BF16), wd_ref[0], preferred_element_type=F32)

    @pl.when(j == pl.num_programs(1) - 1)
    def _():
        w = jnp.sum(ws_ref[...], axis=1, keepdims=True)
        y = jnp.where(r < nvalid_ref[0], acc_ref[...] * w, 0.0)
        ys_ref[...] = y.astype(ys_ref.dtype)


def moe_experts(plan, hs, ws, w_gate, w_up, w_down, tf=256):
    rows, d = hs.shape
    tm = MOE_TILE
    f = w_gate.shape[2]
    rows2 = lambda r, j, te, nv: (r, 0)
    return pl.pallas_call(
        _experts_kernel,
        grid_spec=pltpu.PrefetchScalarGridSpec(
            num_scalar_prefetch=2,
            grid=(rows // tm, f // tf),
            in_specs=[pl.BlockSpec((tm, d), rows2),
                      pl.BlockSpec((tm, LANES), rows2),
                      pl.BlockSpec((1, d, tf), lambda r, j, te, nv: (te[r], 0, j)),
                      pl.BlockSpec((1, d, tf), lambda r, j, te, nv: (te[r], 0, j)),
                      pl.BlockSpec((1, tf, d), lambda r, j, te, nv: (te[r], j, 0))],
            out_specs=pl.BlockSpec((tm, d), rows2),
            scratch_shapes=[pltpu.VMEM((tm, d), F32)]),
        out_shape=jax.ShapeDtypeStruct((rows, d), BF16),
        compiler_params=_params(("parallel", "arbitrary")),
        name="moe_experts",
    )(plan["texp"], plan["n_valid"], hs, ws, w_gate, w_up, w_down)


def _combine_kernel(tile_ref, blk_ref, np_ref, off_ref, tok_ref, ys_ref, x_ref, mod_ref, fnw_ref,
                    o_ref, acc_ref):
    p = pl.program_id(0)
    tm = ys_ref.shape[0]
    n_pairs = np_ref[0]
    b = blk_ref[p]
    r = tile_ref[p]
    live = p < n_pairs
    first = jnp.logical_or(p == 0, blk_ref[jnp.maximum(p - 1, 0)] != b)
    nxt = jnp.minimum(p + 1, pl.num_programs(0) - 1)
    last = jnp.logical_or(p + 1 >= n_pairs, blk_ref[nxt] != b)

    @pl.when(jnp.logical_and(live, first))
    def _():
        acc_ref[...] = jnp.zeros_like(acc_ref)

    @pl.when(live)
    def _():
        tok = tok_ref[0]
        rowid = (r * tm + lax.broadcasted_iota(jnp.int32, (1, tm), 1)).astype(F32)
        onehot = (jnp.where(_slot_positions(tok, 0, off_ref, False) == rowid, 1.0, 0.0)
                  + jnp.where(_slot_positions(tok, 1, off_ref, False) == rowid, 1.0, 0.0))
        acc_ref[...] += jnp.dot(onehot.astype(BF16), ys_ref[...], preferred_element_type=F32)

    @pl.when(jnp.logical_and(live, last))
    def _():
        o_ref[0] = _rms(x_ref[0] + mod_ref[0][2:3] * acc_ref[...], fnw_ref[...])


def moe_combine(plan, tok, ys, x, mod, final_nw, max_pairs):
    bsz, s, d = x.shape
    nb = MOE_TILE
    tm = MOE_TILE
    per_b = s // nb
    tok_blk = lambda p, t, b, n, o: (b[p], 0, 0)
    x_blk = lambda p, t, b, n, o: (b[p] // per_b, b[p] % per_b, 0)
    return pl.pallas_call(
        _combine_kernel,
        grid_spec=pltpu.PrefetchScalarGridSpec(
            num_scalar_prefetch=4,
            grid=(max_pairs,),
            in_specs=[pl.BlockSpec((1, nb, LANES), tok_blk),
                      pl.BlockSpec((tm, d), lambda p, t, b, n, o: (t[p], 0)),
                      pl.BlockSpec((1, nb, d), x_blk),
                      pl.BlockSpec((1, 3, d), lambda p, t, b, n, o: (b[p] // per_b, 0, 0)),
                      pl.BlockSpec((1, d), lambda p, t, b, n, o: (0, 0))],
            out_specs=pl.BlockSpec((1, nb, d), x_blk),
            scratch_shapes=[pltpu.VMEM((nb, d), F32)]),
        out_shape=jax.ShapeDtypeStruct(x.shape, F32),
        compiler_params=_params(("arbitrary",)),
        name="moe_combine",
    )(plan["c_tile"], plan["c_blk"], plan["n_pairs"], plan["off"], tok, ys, x, mod, final_nw.reshape(1, d))


def moe_layer(x, mod, nw, w_router, w_gate, w_up, w_down, final_nw):
    bsz, s, d = x.shape
    nblk = bsz * s // MOE_TILE
    n_tiles = 2 * nblk + N_EXPERTS
    max_pairs = n_tiles + N_EXPERTS * nblk
    hb, tok, tokt, before, total = moe_router(x, mod, nw, w_router)
    plan = _moe_plan(before[:, 0, :N_EXPERTS].astype(jnp.int32), total[0, :N_EXPERTS].astype(jnp.int32),
                     n_tiles, max_pairs)
    hs, ws = moe_gather(plan, tokt, tok, hb, n_tiles, max_pairs)
    ys = moe_experts(plan, hs, ws, w_gate, w_up, w_down)
    return moe_combine(plan, tok, ys, x, mod, final_nw, max_pairs)


def kernel(x, c, e_norm_mix, e_mod_mix_w, e_mod_mix_b, e_w_in, e_fox_fb, e_gla_w2, e_gla_b2, e_gla_norm, e_w_o, e_norm_ffn, e_mod_ffn_w, e_mod_ffn_b, e_ffn_gate, e_ffn_up, e_ffn_down, o_norm_mix, o_mod_mix_w, o_mod_mix_b, o_w_in, o_lam_re, o_lam_im, o_log_dt, o_b_re, o_b_im, o_c_re, o_c_im, o_d_skip, o_w_glu, o_w_o, o_norm_ffn, o_mod_ffn_w, o_mod_ffn_b, o_router, o_exp_gate, o_exp_up, o_exp_down, final_norm):
    bsz, s, d = x.shape

    mod = adaln_mod(c, e_mod_mix_w[0], e_mod_mix_b[0])
    fq, fk, fv, gq, gk, gv, gg, small, cum = layer0_inproj(x, mod, e_norm_mix[0], e_w_in[0], e_fox_fb[0])
    fox = fox_attention(fq, fk, fv, cum)
    gla = gla_attention(gq, gk, gv, gg, small, e_gla_w2[0], e_gla_b2[0], e_gla_norm[0])
    mod_ffn = adaln_mod(c, e_mod_ffn_w[0], e_mod_ffn_b[0])
    x = mixer_out_and_ffn(x, fox, gla, e_w_o[0].astype(BF16), mod, mod_ffn, e_norm_ffn[0],
                          e_ffn_gate[0].astype(BF16), e_ffn_up[0].astype(BF16), e_ffn_down[0].astype(BF16))

    mod = adaln_mod(c, o_mod_mix_w[0], o_mod_mix_b[0])
    a_re, a_im, bb_re, bb_im = s5_discretise(o_lam_re[0], o_lam_im[0], o_log_dt[0], o_b_re[0], o_b_im[0])
    x = s5_layer(x, mod, o_norm_mix[0], o_w_in[0], a_re, a_im, bb_re, bb_im, o_c_re[0], o_c_im[0],
                 o_d_skip[0], o_w_glu[0], o_w_o[0])
    mod = adaln_mod(c, o_mod_ffn_w[0], o_mod_ffn_b[0])
    return moe_layer(x, mod, o_norm_ffn[0], o_router[0], o_exp_gate[0].astype(BF16),
                     o_exp_up[0].astype(BF16), o_exp_down[0].astype(BF16), final_norm)
```

```python
import functools

import jax
import jax.numpy as jnp
from jax import lax
from jax.experimental import pallas as pl
from jax.experimental.pallas import tpu as pltpu

F32 = jnp.float32
BF16 = jnp.bfloat16
HIGHEST = lax.Precision.HIGHEST
EPS = 1e-6
NEG_BIG = -1e30
LOG2E = 1.4426950408889634
FOX_BIAS_PARTS = 3
FOX_ROWS = 128

LANES = 128
VMEM_LIMIT = 56 * 1024 * 1024

HEAD_DIM = 64
FOX_HEADS = 8
FOX_DIM = FOX_HEADS * HEAD_DIM
GLA_HEADS = 4
GLA_DK = 64
GLA_DV = 128
GLA_QK = GLA_HEADS * GLA_DK
GLA_V = GLA_HEADS * GLA_DV
GLA_RANK = 16
GLA_TAU = 16.0
GLA_CHUNK = 64
GLA_SUB = 16
S5_GROUP = 16
S5_STATE = 64
S5_GB = 8
N_EXPERTS = 8
SMALL_FF0 = 0
SMALL_LR0 = 8


def _params(sem):
    return pltpu.CompilerParams(dimension_semantics=sem, vmem_limit_bytes=VMEM_LIMIT)


def _log_sigmoid(x):
    return jnp.minimum(x, 0.0) - jnp.log(1.0 + jnp.exp(-jnp.abs(x)))


def _silu(x):
    return x * jax.nn.sigmoid(x)


def _gelu_tanh(x):
    c = 0.7978845608028654
    return 0.5 * x * (1.0 + jnp.tanh(c * (x + 0.044715 * (x * x * x))))


def _rms(x, w):
    return x * lax.rsqrt(jnp.mean(x * x, axis=-1, keepdims=True) + EPS) * w


def _norm_mod(x, nw, mod):
    return _rms(x, nw) * (1.0 + mod[1:2]) + mod[0:1]


def _nt_dot(a, b):
    return lax.dot_general(a, b, (((1,), (1,)), ((), ())), preferred_element_type=F32)


def _mod_kernel(ct_ref, w_ref, b_ref, o_ref):
    s = _silu(ct_ref[...])
    w = w_ref[...]
    for b in range(s.shape[1]):
        o_ref[b:b + 1, :] = jnp.sum(s[:, b:b + 1] * w, axis=0, keepdims=True) + b_ref[...]


def adaln_mod(c, w, b):
    bsz, d = c.shape
    n = w.shape[1]
    tn = 512
    out = pl.pallas_call(
        _mod_kernel,
        grid=(n // tn,),
        in_specs=[pl.BlockSpec((d, bsz), lambda j: (0, 0)),
                  pl.BlockSpec((d, tn), lambda j: (0, j)),
                  pl.BlockSpec((1, tn), lambda j: (0, j))],
        out_specs=pl.BlockSpec((bsz, tn), lambda j: (0, j)),
        out_shape=jax.ShapeDtypeStruct((bsz, n), F32),
        compiler_params=_params(("parallel",)),
        name="adaln_mod",
    )(c.T, w, b.reshape(1, n))
    return out.reshape(bsz, 3, d)


def _inproj_kernel(x_ref, mod_ref, nw_ref, wbig_ref, wsm_ref, fb_ref, tri_ref,
                   fq_ref, fk_ref, fv_ref, gq_ref, gk_ref, gv_ref, gg_ref, sm_ref, cum_ref,
                   carry_ref):
    i = pl.program_id(1)
    tm = x_ref.shape[1]
    h = _norm_mod(x_ref[0], nw_ref[...], mod_ref[0])
    hb = h.astype(BF16)
    col = 0
    for ref in (fq_ref, fk_ref, fv_ref, gq_ref, gk_ref, gv_ref, gg_ref):
        n = ref.shape[2]
        y = jnp.dot(hb, wbig_ref[:, col:col + n], preferred_element_type=F32)
        if ref is fq_ref:
            y = y * (LOG2E * HEAD_DIM ** -0.5)
        ref[0] = y.astype(ref.dtype)
        col += n
    small = jnp.dot(h, wsm_ref[...], precision=HIGHEST, preferred_element_type=F32)
    sm_ref[0] = small
    logf = _log_sigmoid(small + fb_ref[...])

    @pl.when(i == 0)
    def _():
        carry_ref[...] = jnp.zeros_like(carry_ref)

    cum = jnp.dot(tri_ref[...], logf, precision=HIGHEST, preferred_element_type=F32) + carry_ref[...]
    cum_ref[0] = cum
    carry_ref[...] = cum[tm - 1:tm, :]


def layer0_inproj(x, mod, nw, w_in, fox_fb, tm=512):
    bsz, s, d = x.shape
    c0 = 3 * FOX_DIM
    c_ff = c0
    c_g = c_ff + FOX_HEADS
    c_lr = c_g + 2 * GLA_QK + 2 * GLA_V
    wbig = jnp.concatenate([w_in[:, :c0], w_in[:, c_g:c_lr]], axis=1).astype(BF16)
    wsm = jnp.zeros((d, LANES), F32)
    wsm = wsm.at[:, SMALL_FF0:SMALL_FF0 + FOX_HEADS].set(w_in[:, c_ff:c_g])
    wsm = wsm.at[:, SMALL_LR0:SMALL_LR0 + GLA_RANK].set(w_in[:, c_lr:])
    fb = jnp.zeros((1, LANES), F32).at[0, SMALL_FF0:SMALL_FF0 + FOX_HEADS].set(fox_fb)
    tri = jnp.tril(jnp.ones((tm, tm), F32))
    widths = (FOX_DIM, FOX_DIM, FOX_DIM, GLA_QK, GLA_QK, GLA_V, GLA_V)
    row = lambda b, i: (b, i, 0)
    const2 = lambda b, i: (0, 0)
    outs = pl.pallas_call(
        _inproj_kernel,
        grid=(bsz, s // tm),
        in_specs=[pl.BlockSpec((1, tm, d), row),
                  pl.BlockSpec((1, 3, d), lambda b, i: (b, 0, 0)),
                  pl.BlockSpec((1, d), const2),
                  pl.BlockSpec(wbig.shape, const2),
                  pl.BlockSpec(wsm.shape, const2),
                  pl.BlockSpec((1, LANES), const2),
                  pl.BlockSpec((tm, tm), const2)],
        out_specs=[pl.BlockSpec((1, tm, n), row) for n in widths]
                  + [pl.BlockSpec((1, tm, LANES), row)] * 2,
        out_shape=[jax.ShapeDtypeStruct((bsz, s, n), BF16) for n in widths]
                  + [jax.ShapeDtypeStruct((bsz, s, LANES), F32)] * 2,
        scratch_shapes=[pltpu.VMEM((1, LANES), F32)],
        compiler_params=_params(("parallel", "arbitrary")),
        name="layer0_inproj",
    )(x, mod, nw.reshape(1, d), wbig, wsm, fb, tri)
    return outs


def _fox_kernel(q_ref, k_ref, v_ref, c_ref, o_ref, kaug_ref, vaug_ref, m_ref, acc_ref, qaug_ref):
    pair = pl.program_id(1)
    i = pl.program_id(2)
    t = q_ref.shape[1]
    lane = lax.broadcasted_iota(jnp.int32, (1, LANES), 1)
    heads = (lane < HEAD_DIM, lane >= HEAD_DIM)
    spare = (HEAD_DIM, 0)

    @pl.when(i == 0)
    def _():
        k = k_ref[0].astype(F32)
        v = v_ref[0].astype(F32)
        cum = c_ref[0]
        for h in range(2):
            bias = jnp.sum(jnp.where(lane == 2 * pair + h, cum, 0.0), axis=1, keepdims=True) * (-LOG2E)
            ka = jnp.where(heads[h], k, 0.0)
            rest = bias
            for n in range(FOX_BIAS_PARTS):
                piece = rest.astype(BF16).astype(F32)
                rest = rest - piece
                ka = jnp.where(lane == spare[h] + n, piece, ka)
            kaug_ref[h] = ka.astype(BF16)
            vaug_ref[h, :, :LANES] = jnp.where(heads[h], v, 0.0).astype(BF16)
            vaug_ref[h, :, LANES:] = jnp.broadcast_to(jnp.where(heads[h], 1.0, 0.0), v.shape).astype(BF16)

    q = q_ref[0].astype(F32)
    for h in range(2):
        ones_here = jnp.where(lane >= spare[h], jnp.where(lane < spare[h] + FOX_BIAS_PARTS, 1.0, 0.0), 0.0)
        qaug_ref[h] = jnp.where(heads[h], q, ones_here).astype(BF16)
    m_ref[...] = jnp.full_like(m_ref, NEG_BIG)
    acc_ref[...] = jnp.zeros_like(acc_ref)

    def tile(j, diagonal):
        keys = pl.ds(pl.multiple_of(j * t, t), t)
        for r0 in range(0, t, FOX_ROWS):
            rows = slice(r0, r0 + FOX_ROWS)
            alphas = []
            upd = None
            for h in range(2):
                s = _nt_dot(qaug_ref[h, rows, :], kaug_ref[h, keys, :])
                if diagonal:
                    r = r0 + lax.broadcasted_iota(jnp.int32, (FOX_ROWS, t), 0)
                    c = lax.broadcasted_iota(jnp.int32, (FOX_ROWS, t), 1)
                    s = jnp.where(c <= r, s, NEG_BIG)
                m_prev = m_ref[h, rows, :]
                m_next = jnp.maximum(m_prev, jnp.max(s, axis=1, keepdims=True))
                m_ref[h, rows, :] = m_next
                p = jnp.exp2(s - jnp.concatenate([m_next] * (t // LANES), axis=1)).astype(BF16)
                alphas.append(jnp.exp2(m_prev - m_next))
                d = jnp.dot(p, vaug_ref[h, keys, :], preferred_element_type=F32)
                upd = d if upd is None else upd + d
            alpha = jnp.where(heads[0], alphas[0], alphas[1])
            acc_ref[rows, :] = jnp.concatenate([alpha, alpha], axis=1) * acc_ref[rows, :] + upd

    def body(j, carry):
        tile(j, False)
        return carry

    lax.fori_loop(0, i, body, 0)
    tile(i, True)
    o_ref[0] = (acc_ref[:, :LANES] / acc_ref[:, LANES:]).astype(o_ref.dtype)


def fox_attention(fq, fk, fv, cum, t=512):
    bsz, s, _ = fq.shape
    n_pair = FOX_HEADS // 2
    seq = lambda b, p, i: (b, 0, p)
    return pl.pallas_call(
        _fox_kernel,
        grid=(bsz, n_pair, s // t),
        in_specs=[pl.BlockSpec((1, t, LANES), lambda b, p, i: (b, i, p)),
                  pl.BlockSpec((1, s, LANES), seq),
                  pl.BlockSpec((1, s, LANES), seq),
                  pl.BlockSpec((1, s, LANES), lambda b, p, i: (b, 0, 0))],
        out_specs=pl.BlockSpec((1, t, LANES), lambda b, p, i: (b, i, p)),
        out_shape=jax.ShapeDtypeStruct((bsz, s, FOX_DIM), BF16),
        scratch_shapes=[pltpu.VMEM((2, s, LANES), BF16), pltpu.VMEM((2, s, 2 * LANES), BF16),
                        pltpu.VMEM((2, t, LANES), F32), pltpu.VMEM((t, 2 * LANES), F32),
                        pltpu.VMEM((2, t, LANES), BF16)],
        compiler_params=_params(("parallel", "parallel", "arbitrary")),
        name="fox_attention",
    )(fq, fk, fv, cum)


def _gla_pair_chunk(q, k, v, la, st, tri, eye, k_scr, b_scr):
    C, SUB = GLA_CHUNK, GLA_SUB
    nsub = C // SUB
    lane = lax.broadcasted_iota(jnp.int32, (1, LANES), 1)
    head_a = lane < GLA_DK
    b = jnp.dot(tri, la, precision=HIGHEST, preferred_element_type=F32)
    b_last = b[C - 1:C, :]
    k_scr[...] = k
    b_scr[...] = b
    rowblk = lax.broadcasted_iota(jnp.int32, (C, 1), 0) // SUB

    refs = [b[m * SUB - 1:m * SUB, :] for m in range(1, nsub)]
    rsel = refs[-1]
    for m in range(nsub - 2, 0, -1):
        rsel = jnp.where(rowblk == m, refs[m - 1], rsel)
    qt = q * jnp.exp(jnp.minimum(b - rsel, 0.0))
    zero = jnp.zeros_like(q)
    qs = jnp.concatenate([jnp.where(rowblk == m, qt, zero) for m in range(1, nsub)], axis=1)
    ks = jnp.concatenate(
        [jnp.where(rowblk < m, k * jnp.exp(jnp.minimum(refs[m - 1] - b, 0.0)), zero)
         for m in range(1, nsub)], axis=1)
    lane3 = lax.broadcasted_iota(jnp.int32, (1, (nsub - 1) * LANES), 1)
    head_a3 = (lane3 & (LANES - 1)) < GLA_DK
    zero3 = jnp.zeros_like(ks)
    kstack = jnp.concatenate([jnp.where(head_a3, ks, zero3), jnp.where(head_a3, zero3, ks)], axis=0)
    a_off = _nt_dot(qs.astype(BF16), kstack.astype(BF16))

    row16 = lax.broadcasted_iota(jnp.int32, (SUB, 1), 0)
    blocks = []
    for i in range(nsub):
        qb = q[i * SUB:(i + 1) * SUB, :]
        bb = b[i * SUB:(i + 1) * SUB, :]
        d = jnp.zeros((SUB, LANES), F32)
        for sp in range(SUB):
            srow = i * SUB + sp
            e = qb * k_scr[srow:srow + 1, :] * jnp.exp(jnp.minimum(bb - b_scr[srow:srow + 1, :], 0.0))
            da = jnp.sum(jnp.where(head_a, e, 0.0), axis=1, keepdims=True)
            db = jnp.sum(jnp.where(head_a, 0.0, e), axis=1, keepdims=True)
            live = row16 >= sp
            da = jnp.where(live, da, 0.0)
            db = jnp.where(live, db, 0.0)
            d = jnp.where(lane == srow, da, jnp.where(lane == GLA_DK + srow, db, d))
        blocks.append(d)
    a_pair = a_off + jnp.concatenate(blocks, axis=0)

    lane_v = lax.broadcasted_iota(jnp.int32, (1, 2 * GLA_DV), 1)
    first_v = lane_v < GLA_DV
    zv = jnp.zeros_like(v)
    vbd = jnp.concatenate([jnp.where(first_v, v, zv), jnp.where(first_v, zv, v)], axis=0)
    qh = q * jnp.exp(b)
    o = (jnp.dot(a_pair.astype(BF16), vbd, preferred_element_type=F32)
         + _nt_dot(qh.astype(BF16), st.astype(BF16)))

    k_end = k * jnp.exp(b_last - b)
    v_t = _nt_dot(eye, v).astype(BF16)
    inc = jnp.dot(v_t, k_end.astype(BF16), preferred_element_type=F32)
    row_v = lax.broadcasted_iota(jnp.int32, (2 * GLA_DV, 1), 0)
    same_head = (row_v // GLA_DV) == (lane // GLA_DK)
    st_new = st * jnp.exp(b_last) + jnp.where(same_head, inc, 0.0)
    return o, st_new


def _gla_kernel(gq_ref, gk_ref, gv_ref, gg_ref, sm_ref, w2_ref, b2_ref, gn_ref, tri_ref, eye_ref,
                o_ref, st_ref, la_ref, k_scr, b_scr):
    i = pl.program_id(1)
    tm = gq_ref.shape[1]
    n_pair = GLA_HEADS // 2
    scale = GLA_DK ** -0.5

    @pl.when(i == 0)
    def _():
        st_ref[...] = jnp.zeros_like(st_ref)

    z = jnp.dot(sm_ref[0], w2_ref[...], precision=HIGHEST, preferred_element_type=F32) + b2_ref[...]
    la_ref[...] = _log_sigmoid(z) * (1.0 / GLA_TAU)
    gn = gn_ref[...]

    def chunk(c, carry):
        base = pl.multiple_of(c * GLA_CHUNK, GLA_CHUNK)
        rows = pl.ds(base, GLA_CHUNK)
        for hp in range(n_pair):
            ql = slice(hp * LANES, (hp + 1) * LANES)
            vl = slice(hp * 2 * GLA_DV, (hp + 1) * 2 * GLA_DV)
            q = gq_ref[0, rows, ql].astype(F32) * scale
            k = gk_ref[0, rows, ql].astype(F32)
            v = gv_ref[0, rows, vl]
            o, st_new = _gla_pair_chunk(q, k, v, la_ref[rows, ql], st_ref[hp], tri_ref[...],
                                        eye_ref[...], k_scr.at[hp], b_scr.at[hp])
            st_ref[hp] = st_new
            gate = _silu(gg_ref[0, rows, vl].astype(F32))
            halves = [_rms(o[:, h * GLA_DV:(h + 1) * GLA_DV], gn) for h in range(2)]
            o_ref[0, rows, vl] = (jnp.concatenate(halves, axis=1) * gate).astype(o_ref.dtype)
        return carry

    lax.fori_loop(0, tm // GLA_CHUNK, chunk, 0)


def gla_attention(gq, gk, gv, gg, small, w2, b2, gnorm, tm=512):
    bsz, s, _ = gq.shape
    w2p = jnp.zeros((LANES, GLA_QK), F32).at[SMALL_LR0:SMALL_LR0 + GLA_RANK].set(w2)
    tri = jnp.tril(jnp.ones((GLA_CHUNK, GLA_CHUNK), F32))
    eye = jnp.eye(2 * GLA_DV, dtype=BF16)
    row = lambda b, i: (b, i, 0)
    const2 = lambda b, i: (0, 0)
    return pl.pallas_call(
        _gla_kernel,
        grid=(bsz, s // tm),
        in_specs=[pl.BlockSpec((1, tm, GLA_QK), row), pl.BlockSpec((1, tm, GLA_QK), row),
                  pl.BlockSpec((1, tm, GLA_V), row), pl.BlockSpec((1, tm, GLA_V), row),
                  pl.BlockSpec((1, tm, LANES), row),
                  pl.BlockSpec(w2p.shape, const2), pl.BlockSpec((1, GLA_QK), const2),
                  pl.BlockSpec((1, GLA_DV), const2), pl.BlockSpec(tri.shape, const2),
                  pl.BlockSpec(eye.shape, const2)],
        out_specs=pl.BlockSpec((1, tm, GLA_V), row),
        out_shape=jax.ShapeDtypeStruct((bsz, s, GLA_V), BF16),
        scratch_shapes=[pltpu.VMEM((GLA_HEADS // 2, 2 * GLA_DV, LANES), F32),
                        pltpu.VMEM((tm, GLA_QK), F32),
                        pltpu.VMEM((GLA_HEADS // 2, GLA_CHUNK, LANES), F32),
                        pltpu.VMEM((GLA_HEADS // 2, GLA_CHUNK, LANES), F32)],
        compiler_params=_params(("parallel", "arbitrary")),
        name="gla_attention",
    )(gq, gk, gv, gg, small, w2p, b2.reshape(1, GLA_QK), gnorm.reshape(1, GLA_DV), tri, eye)


def _mix_ffn_kernel(x_ref, a_ref, b_ref, wo_ref, modm_ref, modf_ref, nw_ref, wg_ref, wu_ref, wd_ref,
                    o_ref, x1_ref, h_ref, acc_ref):
    j = pl.program_id(2)

    @pl.when(j == 0)
    def _():
        na = a_ref.shape[2]
        y = (jnp.dot(a_ref[0], wo_ref[:na, :], preferred_element_type=F32)
             + jnp.dot(b_ref[0], wo_ref[na:, :], preferred_element_type=F32))
        x1 = x_ref[0] + modm_ref[0][2:3] * y
        x1_ref[...] = x1
        h_ref[...] = _norm_mod(x1, nw_ref[...], modf_ref[0]).astype(BF16)
        acc_ref[...] = jnp.zeros_like(acc_ref)

    hb = h_ref[...]
    act = (_silu(jnp.dot(hb, wg_ref[...], preferred_element_type=F32))
           * jnp.dot(hb, wu_ref[...], preferred_element_type=F32))
    acc_ref[...] += jnp.dot(act.astype(BF16), wd_ref[...], preferred_element_type=F32)

    @pl.when(j == pl.num_programs(2) - 1)
    def _():
        o_ref[0] = x1_ref[...] + modf_ref[0][2:3] * acc_ref[...]


def mixer_out_and_ffn(x, a, b, w_o, mod_mix, mod_ffn, nw, w_gate, w_up, w_down, tm=1024, tf=256):
    bsz, s, d = x.shape
    f = w_gate.shape[1]
    row = lambda bb, i, j: (bb, i, 0)
    per_b = lambda bb, i, j: (bb, 0, 0)
    const2 = lambda bb, i, j: (0, 0)
    return pl.pallas_call(
        _mix_ffn_kernel,
        grid=(bsz, s // tm, f // tf),
        in_specs=[pl.BlockSpec((1, tm, d), row),
                  pl.BlockSpec((1, tm, a.shape[2]), row),
                  pl.BlockSpec((1, tm, b.shape[2]), row),
                  pl.BlockSpec(w_o.shape, const2),
                  pl.BlockSpec((1, 3, d), per_b),
                  pl.BlockSpec((1, 3, d), per_b),
                  pl.BlockSpec((1, d), const2),
                  pl.BlockSpec((d, tf), lambda bb, i, j: (0, j)),
                  pl.BlockSpec((d, tf), lambda bb, i, j: (0, j)),
                  pl.BlockSpec((tf, d), lambda bb, i, j: (j, 0))],
        out_specs=pl.BlockSpec((1, tm, d), row),
        out_shape=jax.ShapeDtypeStruct(x.shape, F32),
        scratch_shapes=[pltpu.VMEM((tm, d), F32), pltpu.VMEM((tm, d), BF16), pltpu.VMEM((tm, d), F32)],
        compiler_params=_params(("parallel", "parallel", "arbitrary")),
        name="mixer_out_and_ffn",
    )(x, a, b, w_o, mod_mix, mod_ffn, nw.reshape(1, d), w_gate, w_up, w_down)


S5_SEG = 8


def _s5_disc_kernel(lre_ref, lim_ref, ldt_ref, bre_ref, bim_ref,
                    bbre_ref, bbim_ref, pre_ref, pim_ref):
    lre = lre_ref[...]
    lim = lim_ref[...]
    dt = jnp.exp(ldt_ref[...])
    mag = jnp.exp(lre * dt)
    a_re = mag * jnp.cos(lim * dt)
    a_im = mag * jnp.sin(lim * dt)
    den = lre * lre + lim * lim
    nr = a_re - 1.0
    ni = a_im
    f_re = (nr * lre + ni * lim) / den
    f_im = (ni * lre - nr * lim) / den
    bre = bre_ref[...]
    bim = bim_ref[...]
    bbre_ref[...] = f_re[:, None, :] * bre - f_im[:, None, :] * bim
    bbim_ref[...] = f_re[:, None, :] * bim + f_im[:, None, :] * bre
    n = (lax.broadcasted_iota(jnp.int32, pre_ref.shape, 0) + 1).astype(F32)
    mag_n = jnp.exp(n * (lre * dt)[None])
    ang_n = n * (lim * dt)[None]
    pre_ref[...] = mag_n * jnp.cos(ang_n)
    pim_ref[...] = mag_n * jnp.sin(ang_n)


def s5_discretise(lam_re, lam_im, log_dt, b_re, b_im, seg_len):
    g, p = lam_re.shape
    k = b_re.shape[2]
    return pl.pallas_call(
        _s5_disc_kernel,
        out_shape=[jax.ShapeDtypeStruct((g, k, p), F32)] * 2 + [jax.ShapeDtypeStruct((seg_len, g, p), F32)] * 2,
        name="s5_discretise",
    )(lam_re, lam_im, log_dt.reshape(g, 1), b_re.transpose(0, 2, 1), b_im.transpose(0, 2, 1))


def _s5_kernel(x_ref, mod_ref, nw_ref, perm_ref, permt_ref, win_ref, bbd_ref, cbd_ref, apow_ref, dsk_ref,
               wglu_ref, wo_ref, o_ref, carry_ref, bu_ref, xs_ref, xsb_ref):
    i = pl.program_id(1)
    tb = x_ref.shape[1]
    nblk = bu_ref.shape[0]
    nch = bu_ref.shape[2] // (2 * LANES)
    seg_len = tb // S5_SEG
    re_l = [slice(ch * LANES, (ch + 1) * LANES) for ch in range(nch)]
    im_l = [slice((nch + ch) * LANES, (nch + ch + 1) * LANES) for ch in range(nch)]

    @pl.when(i == 0)
    def _():
        carry_ref[...] = jnp.zeros_like(carry_ref)

    x = x_ref[0]
    h = _norm_mod(x, nw_ref[...], mod_ref[0]).astype(BF16)
    hp = jnp.dot(perm_ref[...], h, preferred_element_type=F32).astype(BF16)
    u = jnp.dot(hp, win_ref[...], preferred_element_type=F32)
    ub = u.astype(BF16)
    for jb in range(nblk):
        bu_ref[jb] = jnp.dot(ub[:, jb * LANES:(jb + 1) * LANES], bbd_ref[jb], preferred_element_type=F32)

    seg_row = lax.broadcasted_iota(jnp.int32, (S5_SEG, 1), 0)

    def channel_block(jb, blk_carry):
        def bcast(n, lanes):
            return jnp.broadcast_to(apow_ref[jb, n:n + 1, lanes], (S5_SEG, LANES))

        a1 = [(bcast(0, re_l[ch]), bcast(0, im_l[ch])) for ch in range(nch)]

        def step(t, state):
            rows = pl.ds(pl.multiple_of(t * S5_SEG, S5_SEG), S5_SEG)
            out = []
            for ch in range(nch):
                xr, xi = state[2 * ch], state[2 * ch + 1]
                ar, ai = a1[ch]
                nr = ar * xr - ai * xi + bu_ref[jb, rows, re_l[ch]]
                ni = ar * xi + ai * xr + bu_ref[jb, rows, im_l[ch]]
                xs_ref[jb, rows, re_l[ch]] = nr
                xs_ref[jb, rows, im_l[ch]] = ni
                out += [nr, ni]
            return tuple(out)

        zero = jnp.zeros((S5_SEG, LANES), F32)
        ends = lax.fori_loop(0, seg_len, step, (zero,) * (2 * nch), unroll=4)

        starts = []
        for ch in range(nch):
            alr = apow_ref[jb, seg_len - 1:seg_len, re_l[ch]]
            ali = apow_ref[jb, seg_len - 1:seg_len, im_l[ch]]
            er, ei = ends[2 * ch], ends[2 * ch + 1]
            zr = carry_ref[jb, :, re_l[ch]]
            zi = carry_ref[jb, :, im_l[ch]]
            tr = jnp.zeros((S5_SEG, LANES), F32)
            ti = jnp.zeros((S5_SEG, LANES), F32)
            for s in range(S5_SEG):
                tr = jnp.where(seg_row == s, zr, tr)
                ti = jnp.where(seg_row == s, zi, ti)
                zr, zi = (alr * zr - ali * zi + er[s:s + 1], alr * zi + ali * zr + ei[s:s + 1])
            carry_ref[jb, :, re_l[ch]] = zr
            carry_ref[jb, :, im_l[ch]] = zi
            starts.append((jnp.concatenate([tr, tr], axis=0), jnp.concatenate([ti, ti], axis=0)))

        for t2 in range(seg_len // 2):
            rows = slice(t2 * 2 * S5_SEG, (t2 + 1) * 2 * S5_SEG)
            for ch in range(nch):
                pr = jnp.concatenate([bcast(2 * t2, re_l[ch]), bcast(2 * t2 + 1, re_l[ch])], axis=0)
                pi = jnp.concatenate([bcast(2 * t2, im_l[ch]), bcast(2 * t2 + 1, im_l[ch])], axis=0)
                zr, zi = starts[ch]
                xsb_ref[jb, rows, re_l[ch]] = (xs_ref[jb, rows, re_l[ch]] + pr * zr - pi * zi).astype(BF16)
                xsb_ref[jb, rows, im_l[ch]] = (xs_ref[jb, rows, im_l[ch]] + pr * zi + pi * zr).astype(BF16)
        return blk_carry

    lax.fori_loop(0, nblk, channel_block, 0)

    ys = [jnp.dot(xsb_ref[jb], cbd_ref[jb], preferred_element_type=F32) for jb in range(nblk)]
    y = jnp.concatenate(ys, axis=1) + dsk_ref[...] * u
    y = _gelu_tanh(y)
    y = y * jax.nn.sigmoid(jnp.dot(y.astype(BF16), wglu_ref[...], preferred_element_type=F32))
    z = jnp.dot(y.astype(BF16), wo_ref[...], preferred_element_type=F32)
    z_hi = z.astype(BF16)
    z_lo = (z - z_hi.astype(F32)).astype(BF16)
    z = (jnp.dot(permt_ref[...], z_hi, preferred_element_type=F32)
         + jnp.dot(permt_ref[...], z_lo, preferred_element_type=F32))
    o_ref[0] = x + mod_ref[0][2:3] * z


def s5_layer(x, mod, nw, w_in, bb_re, bb_im, pow_re, pow_im, c_re, c_im, d_skip, w_glu, w_o, tb):
    bsz, s, d = x.shape
    g, k, p = bb_re.shape
    nblk = g // S5_GB
    seg_len = pow_re.shape[0]
    assert tb == seg_len * S5_SEG
    eye = jnp.eye(S5_GB, dtype=F32)

    def blockdiag_in(bb):
        t = jnp.einsum('jgkp,gh->jgkhp', bb.reshape(nblk, S5_GB, k, p), eye)
        return t.reshape(nblk, S5_GB * k, S5_GB * p)

    def blockdiag_out(c):
        t = jnp.einsum('jgkp,gh->jgphk', c.reshape(nblk, S5_GB, k, p), eye)
        return t.reshape(nblk, S5_GB * p, S5_GB * k)

    bbd = jnp.concatenate([blockdiag_in(bb_re), blockdiag_in(bb_im)], axis=2).astype(BF16)
    cbd = jnp.concatenate([blockdiag_out(c_re), -blockdiag_out(c_im)], axis=1).astype(BF16)
    half = S5_GB * p
    apow = jnp.concatenate([pow_re.reshape(seg_len, nblk, half), pow_im.reshape(seg_len, nblk, half)],
                           axis=2).transpose(1, 0, 2)
    new_row = jnp.arange(tb)
    old_row = (new_row % S5_SEG) * seg_len + new_row // S5_SEG
    perm = (old_row[:, None] == jnp.arange(tb)[None, :]).astype(BF16)
    row = lambda b, i: (b, i, 0)
    const2 = lambda b, i: (0, 0)
    const3 = lambda b, i: (0, 0, 0)
    return pl.pallas_call(
        _s5_kernel,
        grid=(bsz, s // tb),
        in_specs=[pl.BlockSpec((1, tb, d), row),
                  pl.BlockSpec((1, 3, d), lambda b, i: (b, 0, 0)),
                  pl.BlockSpec((1, d), const2),
                  pl.BlockSpec((tb, tb), const2), pl.BlockSpec((tb, tb), const2),
                  pl.BlockSpec((d, d), const2),
                  pl.BlockSpec(bbd.shape, const3),
                  pl.BlockSpec(cbd.shape, const3),
                  pl.BlockSpec(apow.shape, const3),
                  pl.BlockSpec((1, d), const2),
                  pl.BlockSpec((d, d), const2), pl.BlockSpec((d, d), const2)],
        out_specs=pl.BlockSpec((1, tb, d), row),
        out_shape=jax.ShapeDtypeStruct(x.shape, F32),
        scratch_shapes=[pltpu.VMEM((nblk, 1, 2 * half), F32),
                        pltpu.VMEM((nblk, tb, 2 * half), F32),
                        pltpu.VMEM((nblk, tb, 2 * half), F32),
                        pltpu.VMEM((nblk, tb, 2 * half), BF16)],
        compiler_params=_params(("parallel", "arbitrary")),
        name="s5_layer",
    )(x, mod, nw.reshape(1, d), perm, perm.T, w_in.astype(BF16), bbd, cbd, apow, d_skip.reshape(1, d),
      w_glu.astype(BF16), w_o.astype(BF16))


TOK_E, TOK_RANK, TOK_W = 0, 2, 4
W_PIECES = 3
MOE_TILE = 512
MOE_WIN = 192
ROW_ALIGN = 16


def _router_kernel(x_ref, mod_ref, nw_ref, wr_ref, tri_ref, sel_ref,
                   hb_ref, tok_ref, tokt_ref, tokw_ref, before_ref, total_ref, carry_ref):
    blk = pl.program_id(0)
    h = _norm_mod(x_ref[0], nw_ref[...], mod_ref[0])
    hb_ref[0] = h.astype(BF16)
    logits = jnp.dot(h, wr_ref[...], precision=HIGHEST, preferred_element_type=F32)
    lane = lax.broadcasted_iota(jnp.int32, logits.shape, 1).astype(F32)
    logits = jnp.where(lane < N_EXPERTS, logits, -jnp.inf)
    m1 = jnp.max(logits, axis=1, keepdims=True)
    i1 = jnp.min(jnp.where(logits == m1, lane, float(LANES)), axis=1, keepdims=True)
    rest = jnp.where(lane == i1, -jnp.inf, logits)
    m2 = jnp.max(rest, axis=1, keepdims=True)
    i2 = jnp.min(jnp.where(rest == m2, lane, float(LANES)), axis=1, keepdims=True)
    e2 = jnp.exp(m2 - m1)
    w1 = 1.0 / (1.0 + e2)
    w2 = e2 / (1.0 + e2)

    @pl.when(blk == 0)
    def _():
        carry_ref[...] = jnp.zeros_like(carry_ref)

    routed = jnp.where(lane == i1, 1.0, jnp.where(lane == i2, 1.0, 0.0))
    before = carry_ref[...]
    rank = jnp.dot(tri_ref[...], routed.astype(BF16), preferred_element_type=F32) + before
    r1 = jnp.sum(jnp.where(lane == i1, rank, 0.0), axis=1, keepdims=True)
    r2 = jnp.sum(jnp.where(lane == i2, rank, 0.0), axis=1, keepdims=True)
    fields = ((TOK_E, i1), (TOK_E + 1, i2), (TOK_RANK, r1), (TOK_RANK + 1, r2), (TOK_W, w1), (TOK_W + 1, w2))
    tok = jnp.zeros_like(logits)
    for ln, val in fields:
        tok = jnp.where(lane == ln, val, tok)
    tok_ref[0] = tok
    tokt_ref[0] = _nt_dot_f32(sel_ref[...], tok)
    pieces = jnp.zeros_like(logits)
    for k, w in enumerate((w1, w2)):
        rest = w
        for n in range(W_PIECES):
            piece = rest.astype(BF16).astype(F32)
            rest = rest - piece
            pieces = jnp.where(lane == k * W_PIECES + n, piece, pieces)
    tokw_ref[0] = pieces.astype(BF16)
    before_ref[0] = before
    total = before + jnp.sum(routed, axis=0, keepdims=True)
    carry_ref[...] = total
    total_ref[...] = total


def _nt_dot_f32(a, b):
    return lax.dot_general(a, b, (((1,), (1,)), ((), ())), precision=HIGHEST, preferred_element_type=F32)


def moe_router(x, mod, nw, w_router):
    bsz, s, d = x.shape
    nb = MOE_TILE
    per_b = s // nb
    nblk = bsz * per_b
    wr = jnp.zeros((d, LANES), F32).at[:, :N_EXPERTS].set(w_router)
    tri = jnp.tril(jnp.ones((nb, nb), BF16), k=-1)
    sel = jnp.eye(8, LANES, dtype=F32)
    const2 = lambda i: (0, 0)
    return pl.pallas_call(
        _router_kernel,
        grid=(nblk,),
        in_specs=[pl.BlockSpec((1, nb, d), lambda i: (i // per_b, i % per_b, 0)),
                  pl.BlockSpec((1, 3, d), lambda i: (i // per_b, 0, 0)),
                  pl.BlockSpec((1, d), const2),
                  pl.BlockSpec((d, LANES), const2),
                  pl.BlockSpec((nb, nb), const2),
                  pl.BlockSpec((8, LANES), const2)],
        out_specs=[pl.BlockSpec((1, nb, d), lambda i: (i, 0, 0)),
                   pl.BlockSpec((1, nb, LANES), lambda i: (i, 0, 0)),
                   pl.BlockSpec((1, 8, nb), lambda i: (i, 0, 0)),
                   pl.BlockSpec((1, nb, LANES), lambda i: (i, 0, 0)),
                   pl.BlockSpec((1, 1, LANES), lambda i: (i, 0, 0)),
                   pl.BlockSpec((1, LANES), const2)],
        out_shape=[jax.ShapeDtypeStruct((nblk, nb, d), BF16),
                   jax.ShapeDtypeStruct((nblk, nb, LANES), F32),
                   jax.ShapeDtypeStruct((nblk, 8, nb), F32),
                   jax.ShapeDtypeStruct((nblk, nb, LANES), BF16),
                   jax.ShapeDtypeStruct((nblk, 1, LANES), F32),
                   jax.ShapeDtypeStruct((1, LANES), F32)],
        scratch_shapes=[pltpu.VMEM((1, LANES), F32)],
        compiler_params=_params(("arbitrary",)),
        name="moe_router",
    )(x, mod, nw.reshape(1, d), wr, tri, sel)


def _moe_plan(before, total, n_tiles, max_pairs):
    tm = MOE_TILE
    nblk = before.shape[0]
    ntile = (total + tm - 1) // tm
    tile_end = jnp.cumsum(ntile)
    tile_start = tile_end - ntile
    n_valid = tile_end[-1]
    r = jnp.arange(n_tiles, dtype=jnp.int32)
    texp = jnp.minimum(jnp.sum((tile_end[None, :] <= r[:, None]).astype(jnp.int32), axis=1), N_EXPERTS - 1)
    r0 = (r - tile_start[texp]) * tm
    after = jnp.concatenate([before[1:], total[None, :]], axis=0)
    lo = before[:, texp].T
    hi = after[:, texp].T
    meets = (lo < (r0 + tm)[:, None]) & (hi > r0[:, None]) & (r < n_valid)[:, None]
    n_pairs = jnp.sum(meets).astype(jnp.int32)
    slot = jnp.arange(max_pairs, dtype=jnp.int32)

    def pair_list(mat):
        flat = jnp.nonzero(mat.reshape(-1), size=max_pairs, fill_value=0)[0].astype(jnp.int32)
        return jnp.where(slot < n_pairs, flat, flat[jnp.maximum(n_pairs - 1, 0)])

    by_tile = pair_list(meets)
    by_blk = pair_list(meets.T)
    start = jnp.clip(lo - r0[:, None], 0, tm)
    end = jnp.clip(hi - r0[:, None], 0, tm)
    first = (start // ROW_ALIGN) * ROW_ALIGN
    nwin = jnp.where(meets, (end - first + MOE_WIN - 1) // MOE_WIN, 0).astype(jnp.int32)
    first = first.astype(jnp.int32)
    return dict(texp=texp.astype(jnp.int32), n_valid=n_valid.astype(jnp.int32).reshape(1),
                off=(tile_start * tm).astype(jnp.int32), n_pairs=n_pairs.reshape(1),
                g_tile=by_tile // nblk, g_blk=by_tile % nblk,
                g_first=first.reshape(-1)[by_tile], g_nwin=nwin.reshape(-1)[by_tile],
                c_blk=by_blk // n_tiles, c_tile=by_blk % n_tiles,
                c_first=first.T.reshape(-1)[by_blk], c_nwin=nwin.T.reshape(-1)[by_blk])


def _slot_positions(tok_fields, k, off_ref, along_lanes):
    if along_lanes:
        e = tok_fields[TOK_E + k:TOK_E + k + 1, :]
        rank = tok_fields[TOK_RANK + k:TOK_RANK + k + 1, :]
    else:
        e = tok_fields[:, TOK_E + k:TOK_E + k + 1]
        rank = tok_fields[:, TOK_RANK + k:TOK_RANK + k + 1]
    pos = rank
    for ex in range(N_EXPERTS):
        pos = pos + jnp.where(e == float(ex), off_ref[ex].astype(F32), 0.0)
    return pos


def _window_rows(want, tile, tm, shape, axis):
    base = pl.multiple_of(jnp.minimum(want, tm - MOE_WIN), ROW_ALIGN)
    local = base + lax.broadcasted_iota(jnp.int32, shape, axis)
    rowid = jnp.where(local >= want, tile * tm + local, -1).astype(F32)
    return base, rowid


def _pair_edges(p, key_ref, n_pairs):
    key = key_ref[p]
    first = jnp.logical_or(p == 0, key_ref[jnp.maximum(p - 1, 0)] != key)
    nxt = jnp.minimum(p + 1, pl.num_programs(0) - 1)
    last = jnp.logical_or(p + 1 >= n_pairs, key_ref[nxt] != key)
    return first, last


def _gather_kernel(tile_ref, blk_ref, first_ref, nwin_ref, np_ref, off_ref, tokt_ref, tokw_ref, hb_ref,
                   hs_ref, ws_ref, acc_ref, wacc_ref):
    p = pl.program_id(0)
    tm = hs_ref.shape[0]
    r = tile_ref[p]
    live = p < np_ref[0]
    first, last = _pair_edges(p, tile_ref, np_ref[0])

    @pl.when(first)
    def _():
        acc_ref[...] = jnp.zeros_like(acc_ref)
        wacc_ref[...] = jnp.zeros_like(wacc_ref)

    @pl.when(live)
    def _():
        tokt = tokt_ref[0]
        pos = [_slot_positions(tokt, k, off_ref, True) for k in range(2)]
        lane = lax.broadcasted_iota(jnp.int32, (1, LANES), 1)

        def window(i, carry):
            base, rowid = _window_rows(first_ref[p] + i * MOE_WIN, r, tm, (MOE_WIN, 1), 0)
            onehots = [jnp.where(pos[k] == rowid, 1.0, 0.0).astype(BF16) for k in range(2)]
            rows = pl.ds(base, MOE_WIN)
            acc_ref[rows, :] += jnp.dot(onehots[0] + onehots[1], hb_ref[0], preferred_element_type=F32)
            picked = [jnp.dot(onehots[k], tokw_ref[0], preferred_element_type=F32) for k in range(2)]
            wacc_ref[rows, :] += jnp.where(lane < W_PIECES, picked[0],
                                           jnp.where(lane < 2 * W_PIECES, picked[1], 0.0))
            return carry

        lax.fori_loop(0, nwin_ref[p], window, 0)

    @pl.when(jnp.logical_and(live, last))
    def _():
        hs_ref[...] = acc_ref[...].astype(hs_ref.dtype)
        ws_ref[...] = wacc_ref[...]


def moe_gather(plan, tokt, tokw, hb, n_tiles, max_pairs):
    nblk, nb, d = hb.shape
    tm = MOE_TILE
    by_blk3 = lambda p, t, b, *_: (b[p], 0, 0)
    by_tile2 = lambda p, t, *_: (t[p], 0)
    return pl.pallas_call(
        _gather_kernel,
        grid_spec=pltpu.PrefetchScalarGridSpec(
            num_scalar_prefetch=6,
            grid=(max_pairs,),
            in_specs=[pl.BlockSpec((1, 8, nb), by_blk3),
                      pl.BlockSpec((1, nb, LANES), by_blk3),
                      pl.BlockSpec((1, nb, d), by_blk3)],
            out_specs=[pl.BlockSpec((tm, d), by_tile2), pl.BlockSpec((tm, LANES), by_tile2)],
            scratch_shapes=[pltpu.VMEM((tm, d), F32), pltpu.VMEM((tm, LANES), F32)]),
        out_shape=[jax.ShapeDtypeStruct((n_tiles * tm, d), BF16),
                   jax.ShapeDtypeStruct((n_tiles * tm, LANES), F32)],
        compiler_params=_params(("arbitrary",)),
        name="moe_gather",
    )(plan["g_tile"], plan["g_blk"], plan["g_first"], plan["g_nwin"], plan["n_pairs"], plan["off"],
      tokt, tokw, hb)


def _experts_kernel(texp_ref, nvalid_ref, hs_ref, ws_ref, wg_ref, wu_ref, wd_ref, ys_ref, acc_ref, *, tf):
    r = pl.program_id(0)
    valid = r < nvalid_ref[0]

    @pl.when(valid)
    def _():
        hb = hs_ref[...]
        for j in range(wg_ref.shape[2] // tf):
            cols = slice(j * tf, (j + 1) * tf)
            act = (_silu(jnp.dot(hb, wg_ref[0, :, cols], preferred_element_type=F32))
                   * jnp.dot(hb, wu_ref[0, :, cols], preferred_element_type=F32))
            part = jnp.dot(act.astype(BF16), wd_ref[0, cols, :], preferred_element_type=F32)
            if j == 0:
                acc_ref[...] = part
            else:
                acc_ref[...] += part
        w = jnp.sum(ws_ref[...], axis=1, keepdims=True)
        ys_ref[...] = (acc_ref[...] * w).astype(ys_ref.dtype)

    @pl.when(jnp.logical_not(valid))
    def _():
        ys_ref[...] = jnp.zeros_like(ys_ref)


def moe_experts(plan, hs, ws, w_gate, w_up, w_down, tf=256):
    rows, d = hs.shape
    tm = MOE_TILE
    f = w_gate.shape[2]
    rows2 = lambda r, te, nv: (r, 0)
    expert3 = lambda r, te, nv: (te[r], 0, 0)
    return pl.pallas_call(
        functools.partial(_experts_kernel, tf=tf),
        grid_spec=pltpu.PrefetchScalarGridSpec(
            num_scalar_prefetch=2,
            grid=(rows // tm,),
            in_specs=[pl.BlockSpec((tm, d), rows2),
                      pl.BlockSpec((tm, LANES), rows2),
                      pl.BlockSpec((1, d, f), expert3),
                      pl.BlockSpec((1, d, f), expert3),
                      pl.BlockSpec((1, f, d), expert3)],
            out_specs=pl.BlockSpec((tm, d), rows2),
            scratch_shapes=[pltpu.VMEM((tm, d), F32)]),
        out_shape=jax.ShapeDtypeStruct((rows, d), BF16),
        compiler_params=_params(("arbitrary",)),
        name="moe_experts",
    )(plan["texp"], plan["n_valid"], hs, ws, w_gate, w_up, w_down)


def _combine_kernel(tile_ref, blk_ref, first_ref, nwin_ref, np_ref, off_ref, tok_ref, ys_ref, x_ref,
                    mod_ref, fnw_ref, o_ref, acc_ref):
    p = pl.program_id(0)
    tm = ys_ref.shape[0]
    r = tile_ref[p]
    live = p < np_ref[0]
    first, last = _pair_edges(p, blk_ref, np_ref[0])

    @pl.when(first)
    def _():
        acc_ref[...] = jnp.zeros_like(acc_ref)

    @pl.when(live)
    def _():
        tok = tok_ref[0]
        pos = [_slot_positions(tok, k, off_ref, False) for k in range(2)]

        def window(i, carry):
            base, rowid = _window_rows(first_ref[p] + i * MOE_WIN, r, tm, (1, MOE_WIN), 1)
            onehot = (jnp.where(pos[0] == rowid, 1.0, 0.0) + jnp.where(pos[1] == rowid, 1.0, 0.0))
            acc_ref[...] += jnp.dot(onehot.astype(BF16), ys_ref[pl.ds(base, MOE_WIN), :],
                                    preferred_element_type=F32)
            return carry

        lax.fori_loop(0, nwin_ref[p], window, 0)

    @pl.when(jnp.logical_and(live, last))
    def _():
        o_ref[0] = _rms(x_ref[0] + mod_ref[0][2:3] * acc_ref[...], fnw_ref[...])


def moe_combine(plan, tok, ys, x, mod, final_nw, max_pairs):
    bsz, s, d = x.shape
    nb = MOE_TILE
    tm = MOE_TILE
    per_b = s // nb
    tok_blk = lambda p, t, b, *_: (b[p], 0, 0)
    x_blk = lambda p, t, b, *_: (b[p] // per_b, b[p] % per_b, 0)
    return pl.pallas_call(
        _combine_kernel,
        grid_spec=pltpu.PrefetchScalarGridSpec(
            num_scalar_prefetch=6,
            grid=(max_pairs,),
            in_specs=[pl.BlockSpec((1, nb, LANES), tok_blk),
                      pl.BlockSpec((tm, d), lambda p, t, *_: (t[p], 0)),
                      pl.BlockSpec((1, nb, d), x_blk),
                      pl.BlockSpec((1, 3, d), lambda p, t, b, *_: (b[p] // per_b, 0, 0)),
                      pl.BlockSpec((1, d), lambda p, *_: (0, 0))],
            out_specs=pl.BlockSpec((1, nb, d), x_blk),
            scratch_shapes=[pltpu.VMEM((nb, d), F32)]),
        out_shape=jax.ShapeDtypeStruct(x.shape, F32),
        compiler_params=_params(("arbitrary",)),
        name="moe_combine",
    )(plan["c_tile"], plan["c_blk"], plan["c_first"], plan["c_nwin"], plan["n_pairs"], plan["off"],
      tok, ys, x, mod, final_nw.reshape(1, d))


def moe_layer(x, mod, nw, w_router, w_gate, w_up, w_down, final_nw):
    bsz, s, d = x.shape
    nblk = bsz * s // MOE_TILE
    n_tiles = 2 * nblk + N_EXPERTS
    max_pairs = n_tiles + N_EXPERTS * nblk
    hb, tok, tokt, tokw, before, total = moe_router(x, mod, nw, w_router)
    plan = _moe_plan(before[:, 0, :N_EXPERTS].astype(jnp.int32), total[0, :N_EXPERTS].astype(jnp.int32),
                     n_tiles, max_pairs)
    hs, ws = moe_gather(plan, tokt, tokw, hb, n_tiles, max_pairs)
    ys = moe_experts(plan, hs, ws, w_gate, w_up, w_down)
    return moe_combine(plan, tok, ys, x, mod, final_nw, max_pairs)


def kernel(x, c, e_norm_mix, e_mod_mix_w, e_mod_mix_b, e_w_in, e_fox_fb, e_gla_w2, e_gla_b2, e_gla_norm, e_w_o, e_norm_ffn, e_mod_ffn_w, e_mod_ffn_b, e_ffn_gate, e_ffn_up, e_ffn_down, o_norm_mix, o_mod_mix_w, o_mod_mix_b, o_w_in, o_lam_re, o_lam_im, o_log_dt, o_b_re, o_b_im, o_c_re, o_c_im, o_d_skip, o_w_glu, o_w_o, o_norm_ffn, o_mod_ffn_w, o_mod_ffn_b, o_router, o_exp_gate, o_exp_up, o_exp_down, final_norm):
    bsz, s, d = x.shape

    mod = adaln_mod(c, e_mod_mix_w[0], e_mod_mix_b[0])
    fq, fk, fv, gq, gk, gv, gg, small, cum = layer0_inproj(x, mod, e_norm_mix[0], e_w_in[0], e_fox_fb[0])
    fox = fox_attention(fq, fk, fv, cum)
    gla = gla_attention(gq, gk, gv, gg, small, e_gla_w2[0], e_gla_b2[0], e_gla_norm[0])
    mod_ffn = adaln_mod(c, e_mod_ffn_w[0], e_mod_ffn_b[0])
    x = mixer_out_and_ffn(x, fox, gla, e_w_o[0].astype(BF16), mod, mod_ffn, e_norm_ffn[0],
                          e_ffn_gate[0].astype(BF16), e_ffn_up[0].astype(BF16), e_ffn_down[0].astype(BF16))

    mod = adaln_mod(c, o_mod_mix_w[0], o_mod_mix_b[0])
    s5_tb = 256
    bb_re, bb_im, pow_re, pow_im = s5_discretise(o_lam_re[0], o_lam_im[0], o_log_dt[0], o_b_re[0], o_b_im[0],
                                                 s5_tb // S5_SEG)
    x = s5_layer(x, mod, o_norm_mix[0], o_w_in[0], bb_re, bb_im, pow_re, pow_im, o_c_re[0], o_c_im[0],
                 o_d_skip[0], o_w_glu[0], o_w_o[0], s5_tb)
    mod = adaln_mod(c, o_mod_ffn_w[0], o_mod_ffn_b[0])
    return moe_layer(x, mod, o_norm_ffn[0], o_router[0], o_exp_gate[0].astype(BF16),
                     o_exp_up[0].astype(BF16), o_exp_down[0].astype(BF16), final_norm)
```

```python
import functools

import jax
import jax.numpy as jnp
from jax import lax
from jax.experimental import pallas as pl
from jax.experimental.pallas import tpu as pltpu

F32 = jnp.float32
BF16 = jnp.bfloat16
EPS = 1e-6
NEG_BIG = -1e30
LOG2E = 1.4426950408889634
FOX_BIAS_PARTS = 3
FOX_ROWS = 128

LANES = 128
VMEM_LIMIT = 56 * 1024 * 1024

HEAD_DIM = 64
FOX_HEADS = 8
FOX_DIM = FOX_HEADS * HEAD_DIM
GLA_HEADS = 4
GLA_DK = 64
GLA_DV = 128
GLA_QK = GLA_HEADS * GLA_DK
GLA_V = GLA_HEADS * GLA_DV
GLA_RANK = 16
GLA_TAU = 16.0
GLA_CHUNK = 64
GLA_SUB = 16
S5_GROUP = 16
S5_STATE = 64
S5_GB = 8
N_EXPERTS = 8
SMALL_FF0 = 0
SMALL_LR0 = 8


def _params(sem):
    return pltpu.CompilerParams(dimension_semantics=sem, vmem_limit_bytes=VMEM_LIMIT)


def _log_sigmoid(x):
    return jnp.minimum(x, 0.0) - jnp.log(1.0 + jnp.exp(-jnp.abs(x)))


def _silu(x):
    return x * jax.nn.sigmoid(x)


def _gelu_tanh(x):
    c = 0.7978845608028654
    return 0.5 * x * (1.0 + jnp.tanh(c * (x + 0.044715 * (x * x * x))))


def _rms(x, w):
    return x * lax.rsqrt(jnp.mean(x * x, axis=-1, keepdims=True) + EPS) * w


def _norm_mod(x, nw, mod):
    return _rms(x, nw) * (1.0 + mod[1:2]) + mod[0:1]


def _split_bf16(x, pieces):
    out = []
    for _ in range(pieces):
        p = x.astype(BF16)
        out.append(p)
        x = x - p.astype(F32)
    return out


def _dot_f32ish(a, b_hi, b_lo):
    a_hi, a_lo = _split_bf16(a, 2)
    return (jnp.dot(a_hi, b_hi, preferred_element_type=F32) + jnp.dot(a_hi, b_lo, preferred_element_type=F32)
            + jnp.dot(a_lo, b_hi, preferred_element_type=F32))


def _dot_exact_left(a_bf16, b):
    return sum(jnp.dot(a_bf16, p, preferred_element_type=F32) for p in _split_bf16(b, 3))


def _nt_dot(a, b):
    return lax.dot_general(a, b, (((1,), (1,)), ((), ())), preferred_element_type=F32)


def _mod_kernel(ct_ref, w_ref, b_ref, o_ref):
    s = _silu(ct_ref[...])
    w = w_ref[...]
    for b in range(s.shape[1]):
        o_ref[b:b + 1, :] = jnp.sum(s[:, b:b + 1] * w, axis=0, keepdims=True) + b_ref[...]


def adaln_mod(c, w, b):
    bsz, d = c.shape
    n = w.shape[1]
    tn = 512
    out = pl.pallas_call(
        _mod_kernel,
        grid=(n // tn,),
        in_specs=[pl.BlockSpec((d, bsz), lambda j: (0, 0)),
                  pl.BlockSpec((d, tn), lambda j: (0, j)),
                  pl.BlockSpec((1, tn), lambda j: (0, j))],
        out_specs=pl.BlockSpec((bsz, tn), lambda j: (0, j)),
        out_shape=jax.ShapeDtypeStruct((bsz, n), F32),
        compiler_params=_params(("parallel",)),
        name="adaln_mod",
    )(c.T, w, b.reshape(1, n))
    return out.reshape(bsz, 3, d)


def _inproj_kernel(x_ref, mod_ref, nw_ref, wbig_ref, wsm_ref, fb_ref, tri_ref,
                   fq_ref, fk_ref, fv_ref, gq_ref, gk_ref, gv_ref, gg_ref, sm_ref, cum_ref,
                   carry_ref):
    i = pl.program_id(1)
    tm = x_ref.shape[1]
    h = _norm_mod(x_ref[0], nw_ref[...], mod_ref[0])
    hb = h.astype(BF16)
    col = 0
    for ref in (fq_ref, fk_ref, fv_ref, gq_ref, gk_ref, gv_ref, gg_ref):
        n = ref.shape[2]
        y = jnp.dot(hb, wbig_ref[:, col:col + n], preferred_element_type=F32)
        if ref is fq_ref:
            y = y * (LOG2E * HEAD_DIM ** -0.5)
        ref[0] = y.astype(ref.dtype)
        col += n
    small = _dot_f32ish(h, wsm_ref[0], wsm_ref[1])
    sm_ref[0] = small
    logf = _log_sigmoid(small + fb_ref[...])

    @pl.when(i == 0)
    def _():
        carry_ref[...] = jnp.zeros_like(carry_ref)

    cum = _dot_exact_left(tri_ref[...], logf) + carry_ref[...]
    cum_ref[0] = cum
    carry_ref[...] = cum[tm - 1:tm, :]


def layer0_inproj(x, mod, nw, w_in, fox_fb, tm=512):
    bsz, s, d = x.shape
    c0 = 3 * FOX_DIM
    c_ff = c0
    c_g = c_ff + FOX_HEADS
    c_lr = c_g + 2 * GLA_QK + 2 * GLA_V
    wbig = jnp.concatenate([w_in[:, :c0], w_in[:, c_g:c_lr]], axis=1).astype(BF16)
    wsm = jnp.zeros((d, LANES), F32)
    wsm = wsm.at[:, SMALL_FF0:SMALL_FF0 + FOX_HEADS].set(w_in[:, c_ff:c_g])
    wsm = wsm.at[:, SMALL_LR0:SMALL_LR0 + GLA_RANK].set(w_in[:, c_lr:])
    fb = jnp.zeros((1, LANES), F32).at[0, SMALL_FF0:SMALL_FF0 + FOX_HEADS].set(fox_fb)
    wsm = jnp.stack(_split_bf16(wsm, 2))
    tri = jnp.tril(jnp.ones((tm, tm), BF16))
    widths = (FOX_DIM, FOX_DIM, FOX_DIM, GLA_QK, GLA_QK, GLA_V, GLA_V)
    row = lambda b, i: (b, i, 0)
    const2 = lambda b, i: (0, 0)
    outs = pl.pallas_call(
        _inproj_kernel,
        grid=(bsz, s // tm),
        in_specs=[pl.BlockSpec((1, tm, d), row),
                  pl.BlockSpec((1, 3, d), lambda b, i: (b, 0, 0)),
                  pl.BlockSpec((1, d), const2),
                  pl.BlockSpec(wbig.shape, const2),
                  pl.BlockSpec(wsm.shape, lambda b, i: (0, 0, 0)),
                  pl.BlockSpec((1, LANES), const2),
                  pl.BlockSpec((tm, tm), const2)],
        out_specs=[pl.BlockSpec((1, tm, n), row) for n in widths]
                  + [pl.BlockSpec((1, tm, LANES), row)] * 2,
        out_shape=[jax.ShapeDtypeStruct((bsz, s, n), BF16) for n in widths]
                  + [jax.ShapeDtypeStruct((bsz, s, LANES), F32)] * 2,
        scratch_shapes=[pltpu.VMEM((1, LANES), F32)],
        compiler_params=_params(("parallel", "arbitrary")),
        name="layer0_inproj",
    )(x, mod, nw.reshape(1, d), wbig, wsm, fb, tri)
    return outs


def _fox_kernel(q_ref, k_ref, v_ref, c_ref, o_ref, kaug_ref, vaug_ref, m_ref, acc_ref, qaug_ref):
    pair = pl.program_id(1)
    i = pl.program_id(2)
    t = q_ref.shape[1]
    lane = lax.broadcasted_iota(jnp.int32, (1, LANES), 1)
    heads = (lane < HEAD_DIM, lane >= HEAD_DIM)
    spare = (HEAD_DIM, 0)

    @pl.when(i == 0)
    def _():
        k = k_ref[0].astype(F32)
        v = v_ref[0].astype(F32)
        cum = c_ref[0]
        for h in range(2):
            bias = jnp.sum(jnp.where(lane == 2 * pair + h, cum, 0.0), axis=1, keepdims=True) * (-LOG2E)
            ka = jnp.where(heads[h], k, 0.0)
            rest = bias
            for n in range(FOX_BIAS_PARTS):
                piece = rest.astype(BF16).astype(F32)
                rest = rest - piece
                ka = jnp.where(lane == spare[h] + n, piece, ka)
            kaug_ref[h] = ka.astype(BF16)
            vaug_ref[h, :, :LANES] = jnp.where(heads[h], v, 0.0).astype(BF16)
            vaug_ref[h, :, LANES:] = jnp.broadcast_to(jnp.where(heads[h], 1.0, 0.0), v.shape).astype(BF16)

    q = q_ref[0].astype(F32)
    for h in range(2):
        ones_here = jnp.where(lane >= spare[h], jnp.where(lane < spare[h] + FOX_BIAS_PARTS, 1.0, 0.0), 0.0)
        qaug_ref[h] = jnp.where(heads[h], q, ones_here).astype(BF16)
    m_ref[...] = jnp.full_like(m_ref, NEG_BIG)
    acc_ref[...] = jnp.zeros_like(acc_ref)

    def tile(j, diagonal):
        keys = pl.ds(pl.multiple_of(j * t, t), t)
        for r0 in range(0, t, FOX_ROWS):
            rows = slice(r0, r0 + FOX_ROWS)
            alphas = []
            upd = None
            for h in range(2):
                s = _nt_dot(qaug_ref[h, rows, :], kaug_ref[h, keys, :])
                if diagonal:
                    r = r0 + lax.broadcasted_iota(jnp.int32, (FOX_ROWS, t), 0)
                    c = lax.broadcasted_iota(jnp.int32, (FOX_ROWS, t), 1)
                    s = jnp.where(c <= r, s, NEG_BIG)
                m_prev = m_ref[h, rows, :]
                m_next = jnp.maximum(m_prev, jnp.max(s, axis=1, keepdims=True))
                m_ref[h, rows, :] = m_next
                p = jnp.exp2(s - jnp.concatenate([m_next] * (t // LANES), axis=1)).astype(BF16)
                alphas.append(jnp.exp2(m_prev - m_next))
                d = jnp.dot(p, vaug_ref[h, keys, :], preferred_element_type=F32)
                upd = d if upd is None else upd + d
            alpha = jnp.where(heads[0], alphas[0], alphas[1])
            acc_ref[rows, :] = jnp.concatenate([alpha, alpha], axis=1) * acc_ref[rows, :] + upd

    def body(j, carry):
        tile(j, False)
        return carry

    lax.fori_loop(0, i, body, 0)
    tile(i, True)
    o_ref[0] = (acc_ref[:, :LANES] / acc_ref[:, LANES:]).astype(o_ref.dtype)


def fox_attention(fq, fk, fv, cum, t=512):
    bsz, s, _ = fq.shape
    n_pair = FOX_HEADS // 2
    seq = lambda b, p, i: (b, 0, p)
    return pl.pallas_call(
        _fox_kernel,
        grid=(bsz, n_pair, s // t),
        in_specs=[pl.BlockSpec((1, t, LANES), lambda b, p, i: (b, i, p)),
                  pl.BlockSpec((1, s, LANES), seq),
                  pl.BlockSpec((1, s, LANES), seq),
                  pl.BlockSpec((1, s, LANES), lambda b, p, i: (b, 0, 0))],
        out_specs=pl.BlockSpec((1, t, LANES), lambda b, p, i: (b, i, p)),
        out_shape=jax.ShapeDtypeStruct((bsz, s, FOX_DIM), BF16),
        scratch_shapes=[pltpu.VMEM((2, s, LANES), BF16), pltpu.VMEM((2, s, 2 * LANES), BF16),
                        pltpu.VMEM((2, t, LANES), F32), pltpu.VMEM((t, 2 * LANES), F32),
                        pltpu.VMEM((2, t, LANES), BF16)],
        compiler_params=_params(("parallel", "parallel", "arbitrary")),
        name="fox_attention",
    )(fq, fk, fv, cum)


def _gla_pair_chunk(q, k, v, la, st, tri, eye, k_scr, b_scr):
    C, SUB = GLA_CHUNK, GLA_SUB
    nsub = C // SUB
    lane = lax.broadcasted_iota(jnp.int32, (1, LANES), 1)
    head_a = lane < GLA_DK
    b = _dot_exact_left(tri, la)
    b_last = b[C - 1:C, :]
    k_scr[...] = k
    b_scr[...] = b
    rowblk = lax.broadcasted_iota(jnp.int32, (C, 1), 0) // SUB

    refs = [b[m * SUB - 1:m * SUB, :] for m in range(1, nsub)]
    rsel = refs[-1]
    for m in range(nsub - 2, 0, -1):
        rsel = jnp.where(rowblk == m, refs[m - 1], rsel)
    qt = q * jnp.exp(jnp.minimum(b - rsel, 0.0))
    zero = jnp.zeros_like(q)
    qs = jnp.concatenate([jnp.where(rowblk == m, qt, zero) for m in range(1, nsub)], axis=1)
    ks = jnp.concatenate(
        [jnp.where(rowblk < m, k * jnp.exp(jnp.minimum(refs[m - 1] - b, 0.0)), zero)
         for m in range(1, nsub)], axis=1)
    lane3 = lax.broadcasted_iota(jnp.int32, (1, (nsub - 1) * LANES), 1)
    head_a3 = (lane3 & (LANES - 1)) < GLA_DK
    zero3 = jnp.zeros_like(ks)
    kstack = jnp.concatenate([jnp.where(head_a3, ks, zero3), jnp.where(head_a3, zero3, ks)], axis=0)
    a_off = _nt_dot(qs.astype(BF16), kstack.astype(BF16))

    row16 = lax.broadcasted_iota(jnp.int32, (SUB, 1), 0)
    blocks = []
    for i in range(nsub):
        qb = q[i * SUB:(i + 1) * SUB, :]
        bb = b[i * SUB:(i + 1) * SUB, :]
        d = jnp.zeros((SUB, LANES), F32)
        for sp in range(SUB):
            srow = i * SUB + sp
            e = qb * k_scr[srow:srow + 1, :] * jnp.exp(jnp.minimum(bb - b_scr[srow:srow + 1, :], 0.0))
            da = jnp.sum(jnp.where(head_a, e, 0.0), axis=1, keepdims=True)
            db = jnp.sum(jnp.where(head_a, 0.0, e), axis=1, keepdims=True)
            live = row16 >= sp
            da = jnp.where(live, da, 0.0)
            db = jnp.where(live, db, 0.0)
            d = jnp.where(lane == srow, da, jnp.where(lane == GLA_DK + srow, db, d))
        blocks.append(d)
    a_pair = a_off + jnp.concatenate(blocks, axis=0)

    lane_v = lax.broadcasted_iota(jnp.int32, (1, 2 * GLA_DV), 1)
    first_v = lane_v < GLA_DV
    zv = jnp.zeros_like(v)
    vbd = jnp.concatenate([jnp.where(first_v, v, zv), jnp.where(first_v, zv, v)], axis=0)
    qh = q * jnp.exp(b)
    o = (jnp.dot(a_pair.astype(BF16), vbd, preferred_element_type=F32)
         + _nt_dot(qh.astype(BF16), st.astype(BF16)))

    k_end = k * jnp.exp(b_last - b)
    v_t = _nt_dot(eye, v).astype(BF16)
    inc = jnp.dot(v_t, k_end.astype(BF16), preferred_element_type=F32)
    row_v = lax.broadcasted_iota(jnp.int32, (2 * GLA_DV, 1), 0)
    same_head = (row_v // GLA_DV) == (lane // GLA_DK)
    st_new = st * jnp.exp(b_last) + jnp.where(same_head, inc, 0.0)
    return o, st_new


def _gla_kernel(gq_ref, gk_ref, gv_ref, gg_ref, sm_ref, w2_ref, b2_ref, gn_ref, tri_ref, eye_ref,
                o_ref, st_ref, la_ref, k_scr, b_scr):
    i = pl.program_id(1)
    tm = gq_ref.shape[1]
    n_pair = GLA_HEADS // 2
    scale = GLA_DK ** -0.5

    @pl.when(i == 0)
    def _():
        st_ref[...] = jnp.zeros_like(st_ref)

    z = _dot_f32ish(sm_ref[0], w2_ref[0], w2_ref[1]) + b2_ref[...]
    la_ref[...] = _log_sigmoid(z) * (1.0 / GLA_TAU)
    gn = gn_ref[...]

    def chunk(c, carry):
        base = pl.multiple_of(c * GLA_CHUNK, GLA_CHUNK)
        rows = pl.ds(base, GLA_CHUNK)
        for hp in range(n_pair):
            ql = slice(hp * LANES, (hp + 1) * LANES)
            vl = slice(hp * 2 * GLA_DV, (hp + 1) * 2 * GLA_DV)
            q = gq_ref[0, rows, ql].astype(F32) * scale
            k = gk_ref[0, rows, ql].astype(F32)
            v = gv_ref[0, rows, vl]
            o, st_new = _gla_pair_chunk(q, k, v, la_ref[rows, ql], st_ref[hp], tri_ref[...],
                                        eye_ref[...], k_scr.at[hp], b_scr.at[hp])
            st_ref[hp] = st_new
            gate = _silu(gg_ref[0, rows, vl].astype(F32))
            halves = [_rms(o[:, h * GLA_DV:(h + 1) * GLA_DV], gn) for h in range(2)]
            o_ref[0, rows, vl] = (jnp.concatenate(halves, axis=1) * gate).astype(o_ref.dtype)
        return carry

    lax.fori_loop(0, tm // GLA_CHUNK, chunk, 0)


def gla_attention(gq, gk, gv, gg, small, w2, b2, gnorm, tm=512):
    bsz, s, _ = gq.shape
    w2p = jnp.zeros((LANES, GLA_QK), F32).at[SMALL_LR0:SMALL_LR0 + GLA_RANK].set(w2)
    w2p = jnp.stack(_split_bf16(w2p, 2))
    tri = jnp.tril(jnp.ones((GLA_CHUNK, GLA_CHUNK), BF16))
    eye = jnp.eye(2 * GLA_DV, dtype=BF16)
    row = lambda b, i: (b, i, 0)
    const2 = lambda b, i: (0, 0)
    return pl.pallas_call(
        _gla_kernel,
        grid=(bsz, s // tm),
        in_specs=[pl.BlockSpec((1, tm, GLA_QK), row), pl.BlockSpec((1, tm, GLA_QK), row),
                  pl.BlockSpec((1, tm, GLA_V), row), pl.BlockSpec((1, tm, GLA_V), row),
                  pl.BlockSpec((1, tm, LANES), row),
                  pl.BlockSpec(w2p.shape, lambda b, i: (0, 0, 0)), pl.BlockSpec((1, GLA_QK), const2),
                  pl.BlockSpec((1, GLA_DV), const2), pl.BlockSpec(tri.shape, const2),
                  pl.BlockSpec(eye.shape, const2)],
        out_specs=pl.BlockSpec((1, tm, GLA_V), row),
        out_shape=jax.ShapeDtypeStruct((bsz, s, GLA_V), BF16),
        scratch_shapes=[pltpu.VMEM((GLA_HEADS // 2, 2 * GLA_DV, LANES), F32),
                        pltpu.VMEM((tm, GLA_QK), F32),
                        pltpu.VMEM((GLA_HEADS // 2, GLA_CHUNK, LANES), F32),
                        pltpu.VMEM((GLA_HEADS // 2, GLA_CHUNK, LANES), F32)],
        compiler_params=_params(("parallel", "arbitrary")),
        name="gla_attention",
    )(gq, gk, gv, gg, small, w2p, b2.reshape(1, GLA_QK), gnorm.reshape(1, GLA_DV), tri, eye)


def _mix_ffn_kernel(x_ref, a_ref, b_ref, wo_ref, modm_ref, modf_ref, nw_ref, wg_ref, wu_ref, wd_ref,
                    o_ref, acc_ref, *, tf):
    na = a_ref.shape[2]
    y = (jnp.dot(a_ref[0], wo_ref[:na, :], preferred_element_type=F32)
         + jnp.dot(b_ref[0], wo_ref[na:, :], preferred_element_type=F32))
    x1 = x_ref[0] + modm_ref[0][2:3] * y
    hb = _norm_mod(x1, nw_ref[...], modf_ref[0]).astype(BF16)
    for j in range(wg_ref.shape[1] // tf):
        cols = slice(j * tf, (j + 1) * tf)
        act = (_silu(jnp.dot(hb, wg_ref[:, cols], preferred_element_type=F32))
               * jnp.dot(hb, wu_ref[:, cols], preferred_element_type=F32))
        part = jnp.dot(act.astype(BF16), wd_ref[cols, :], preferred_element_type=F32)
        if j == 0:
            acc_ref[...] = part
        else:
            acc_ref[...] += part
    o_ref[0] = x1 + modf_ref[0][2:3] * acc_ref[...]


def mixer_out_and_ffn(x, a, b, w_o, mod_mix, mod_ffn, nw, w_gate, w_up, w_down, tm=512, tf=256):
    bsz, s, d = x.shape
    row = lambda bb, i: (bb, i, 0)
    per_b = lambda bb, i: (bb, 0, 0)
    const2 = lambda bb, i: (0, 0)
    resident = lambda arr: pl.BlockSpec(arr.shape, const2, pipeline_mode=pl.Buffered(1))
    return pl.pallas_call(
        functools.partial(_mix_ffn_kernel, tf=tf),
        grid=(bsz, s // tm),
        in_specs=[pl.BlockSpec((1, tm, d), row),
                  pl.BlockSpec((1, tm, a.shape[2]), row),
                  pl.BlockSpec((1, tm, b.shape[2]), row),
                  resident(w_o),
                  pl.BlockSpec((1, 3, d), per_b),
                  pl.BlockSpec((1, 3, d), per_b),
                  pl.BlockSpec((1, d), const2),
                  resident(w_gate), resident(w_up), resident(w_down)],
        out_specs=pl.BlockSpec((1, tm, d), row),
        out_shape=jax.ShapeDtypeStruct(x.shape, F32),
        scratch_shapes=[pltpu.VMEM((tm, d), F32)],
        compiler_params=_params(("parallel", "parallel")),
        name="mixer_out_and_ffn",
    )(x, a, b, w_o, mod_mix, mod_ffn, nw.reshape(1, d), w_gate, w_up, w_down)


S5_SEG = 8


def _s5_disc_kernel(lre_ref, lim_ref, ldt_ref, bre_ref, bim_ref,
                    bbre_ref, bbim_ref, pre_ref, pim_ref):
    lre = lre_ref[...]
    lim = lim_ref[...]
    dt = jnp.exp(ldt_ref[...])
    mag = jnp.exp(lre * dt)
    a_re = mag * jnp.cos(lim * dt)
    a_im = mag * jnp.sin(lim * dt)
    den = lre * lre + lim * lim
    nr = a_re - 1.0
    ni = a_im
    f_re = (nr * lre + ni * lim) / den
    f_im = (ni * lre - nr * lim) / den
    bre = bre_ref[...]
    bim = bim_ref[...]
    bbre_ref[...] = f_re[:, None, :] * bre - f_im[:, None, :] * bim
    bbim_ref[...] = f_re[:, None, :] * bim + f_im[:, None, :] * bre
    n = (lax.broadcasted_iota(jnp.int32, pre_ref.shape, 0) + 1).astype(F32)
    mag_n = jnp.exp(n * (lre * dt)[None])
    ang_n = n * (lim * dt)[None]
    pre_ref[...] = mag_n * jnp.cos(ang_n)
    pim_ref[...] = mag_n * jnp.sin(ang_n)


def s5_discretise(lam_re, lam_im, log_dt, b_re, b_im, seg_len):
    g, p = lam_re.shape
    k = b_re.shape[2]
    return pl.pallas_call(
        _s5_disc_kernel,
        out_shape=[jax.ShapeDtypeStruct((g, k, p), F32)] * 2 + [jax.ShapeDtypeStruct((seg_len, g, p), F32)] * 2,
        name="s5_discretise",
    )(lam_re, lam_im, log_dt.reshape(g, 1), b_re.transpose(0, 2, 1), b_im.transpose(0, 2, 1))


def _s5_kernel(x_ref, mod_ref, nw_ref, perm_ref, permt_ref, win_ref, bbd_ref, cbd_ref, apow_ref, dsk_ref,
               wglu_ref, wo_ref, o_ref, carry_ref, bu_ref, xs_ref, xsb_ref):
    i = pl.program_id(1)
    tb = x_ref.shape[1]
    nblk = bu_ref.shape[0]
    nch = bu_ref.shape[2] // (2 * LANES)
    seg_len = tb // S5_SEG
    re_l = [slice(ch * LANES, (ch + 1) * LANES) for ch in range(nch)]
    im_l = [slice((nch + ch) * LANES, (nch + ch + 1) * LANES) for ch in range(nch)]

    @pl.when(i == 0)
    def _():
        carry_ref[...] = jnp.zeros_like(carry_ref)

    x = x_ref[0]
    h = _norm_mod(x, nw_ref[...], mod_ref[0]).astype(BF16)
    hp = jnp.dot(perm_ref[...], h, preferred_element_type=F32).astype(BF16)
    u = jnp.dot(hp, win_ref[...], preferred_element_type=F32)
    ub = u.astype(BF16)
    for jb in range(nblk):
        bu_ref[jb] = jnp.dot(ub[:, jb * LANES:(jb + 1) * LANES], bbd_ref[jb], preferred_element_type=F32)

    seg_row = lax.broadcasted_iota(jnp.int32, (S5_SEG, 1), 0)

    def channel_block(jb, blk_carry):
        def bcast(n, lanes):
            return jnp.broadcast_to(apow_ref[jb, n:n + 1, lanes], (S5_SEG, LANES))

        a1 = [(bcast(0, re_l[ch]), bcast(0, im_l[ch])) for ch in range(nch)]

        def step(t, state):
            rows = pl.ds(pl.multiple_of(t * S5_SEG, S5_SEG), S5_SEG)
            out = []
            for ch in range(nch):
                xr, xi = state[2 * ch], state[2 * ch + 1]
                ar, ai = a1[ch]
                nr = ar * xr - ai * xi + bu_ref[jb, rows, re_l[ch]]
                ni = ar * xi + ai * xr + bu_ref[jb, rows, im_l[ch]]
                xs_ref[jb, rows, re_l[ch]] = nr
                xs_ref[jb, rows, im_l[ch]] = ni
                out += [nr, ni]
            return tuple(out)

        zero = jnp.zeros((S5_SEG, LANES), F32)
        ends = lax.fori_loop(0, seg_len, step, (zero,) * (2 * nch), unroll=4)

        starts = []
        for ch in range(nch):
            alr = apow_ref[jb, seg_len - 1:seg_len, re_l[ch]]
            ali = apow_ref[jb, seg_len - 1:seg_len, im_l[ch]]
            er, ei = ends[2 * ch], ends[2 * ch + 1]
            zr = carry_ref[jb, :, re_l[ch]]
            zi = carry_ref[jb, :, im_l[ch]]
            tr = jnp.zeros((S5_SEG, LANES), F32)
            ti = jnp.zeros((S5_SEG, LANES), F32)
            for s in range(S5_SEG):
                tr = jnp.where(seg_row == s, zr, tr)
                ti = jnp.where(seg_row == s, zi, ti)
                zr, zi = (alr * zr - ali * zi + er[s:s + 1], alr * zi + ali * zr + ei[s:s + 1])
            carry_ref[jb, :, re_l[ch]] = zr
            carry_ref[jb, :, im_l[ch]] = zi
            starts.append((jnp.concatenate([tr, tr], axis=0), jnp.concatenate([ti, ti], axis=0)))

        for t2 in range(seg_len // 2):
            rows = slice(t2 * 2 * S5_SEG, (t2 + 1) * 2 * S5_SEG)
            for ch in range(nch):
                pr = jnp.concatenate([bcast(2 * t2, re_l[ch]), bcast(2 * t2 + 1, re_l[ch])], axis=0)
                pi = jnp.concatenate([bcast(2 * t2, im_l[ch]), bcast(2 * t2 + 1, im_l[ch])], axis=0)
                zr, zi = starts[ch]
                xsb_ref[jb, rows, re_l[ch]] = (xs_ref[jb, rows, re_l[ch]] + pr * zr - pi * zi).astype(BF16)
                xsb_ref[jb, rows, im_l[ch]] = (xs_ref[jb, rows, im_l[ch]] + pr * zi + pi * zr).astype(BF16)
        return blk_carry

    lax.fori_loop(0, nblk, channel_block, 0)

    ys = [jnp.dot(xsb_ref[jb], cbd_ref[jb], preferred_element_type=F32) for jb in range(nblk)]
    y = jnp.concatenate(ys, axis=1) + dsk_ref[...] * u
    y = _gelu_tanh(y)
    y = y * jax.nn.sigmoid(jnp.dot(y.astype(BF16), wglu_ref[...], preferred_element_type=F32))
    z = jnp.dot(y.astype(BF16), wo_ref[...], preferred_element_type=F32)
    z_hi = z.astype(BF16)
    z_lo = (z - z_hi.astype(F32)).astype(BF16)
    z = (jnp.dot(permt_ref[...], z_hi, preferred_element_type=F32)
         + jnp.dot(permt_ref[...], z_lo, preferred_element_type=F32))
    o_ref[0] = x + mod_ref[0][2:3] * z


def s5_layer(x, mod, nw, w_in, bb_re, bb_im, pow_re, pow_im, c_re, c_im, d_skip, w_glu, w_o, tb):
    bsz, s, d = x.shape
    g, k, p = bb_re.shape
    nblk = g // S5_GB
    seg_len = pow_re.shape[0]
    assert tb == seg_len * S5_SEG
    eye = jnp.eye(S5_GB, dtype=F32)

    def blockdiag_in(bb):
        t = jnp.einsum('jgkp,gh->jgkhp', bb.reshape(nblk, S5_GB, k, p), eye)
        return t.reshape(nblk, S5_GB * k, S5_GB * p)

    def blockdiag_out(c):
        t = jnp.einsum('jgkp,gh->jgphk', c.reshape(nblk, S5_GB, k, p), eye)
        return t.reshape(nblk, S5_GB * p, S5_GB * k)

    bbd = jnp.concatenate([blockdiag_in(bb_re), blockdiag_in(bb_im)], axis=2).astype(BF16)
    cbd = jnp.concatenate([blockdiag_out(c_re), -blockdiag_out(c_im)], axis=1).astype(BF16)
    half = S5_GB * p
    apow = jnp.concatenate([pow_re.reshape(seg_len, nblk, half), pow_im.reshape(seg_len, nblk, half)],
                           axis=2).transpose(1, 0, 2)
    new_row = jnp.arange(tb)
    old_row = (new_row % S5_SEG) * seg_len + new_row // S5_SEG
    perm = (old_row[:, None] == jnp.arange(tb)[None, :]).astype(BF16)
    row = lambda b, i: (b, i, 0)
    const2 = lambda b, i: (0, 0)
    const3 = lambda b, i: (0, 0, 0)
    return pl.pallas_call(
        _s5_kernel,
        grid=(bsz, s // tb),
        in_specs=[pl.BlockSpec((1, tb, d), row),
                  pl.BlockSpec((1, 3, d), lambda b, i: (b, 0, 0)),
                  pl.BlockSpec((1, d), const2),
                  pl.BlockSpec((tb, tb), const2), pl.BlockSpec((tb, tb), const2),
                  pl.BlockSpec((d, d), const2),
                  pl.BlockSpec(bbd.shape, const3),
                  pl.BlockSpec(cbd.shape, const3),
                  pl.BlockSpec(apow.shape, const3),
                  pl.BlockSpec((1, d), const2),
                  pl.BlockSpec((d, d), const2), pl.BlockSpec((d, d), const2)],
        out_specs=pl.BlockSpec((1, tb, d), row),
        out_shape=jax.ShapeDtypeStruct(x.shape, F32),
        scratch_shapes=[pltpu.VMEM((nblk, 1, 2 * half), F32),
                        pltpu.VMEM((nblk, tb, 2 * half), F32),
                        pltpu.VMEM((nblk, tb, 2 * half), F32),
                        pltpu.VMEM((nblk, tb, 2 * half), BF16)],
        compiler_params=_params(("parallel", "arbitrary")),
        name="s5_layer",
    )(x, mod, nw.reshape(1, d), perm, perm.T, w_in.astype(BF16), bbd, cbd, apow, d_skip.reshape(1, d),
      w_glu.astype(BF16), w_o.astype(BF16))


TOK_E, TOK_RANK, TOK_W = 0, 2, 4
W_PIECES = 3
MOE_TILE = 512
MOE_WIN = 192
ROW_ALIGN = 16
COMBINE_MAX_WIN = (2 * MOE_TILE + N_EXPERTS * (ROW_ALIGN - 1)) // MOE_WIN + N_EXPERTS + 1


def _router_kernel(x_ref, mod_ref, nw_ref, wr_ref, tri_ref, sel_ref,
                   hb_ref, tok_ref, tokt_ref, tokw_ref, before_ref, total_ref, carry_ref):
    blk = pl.program_id(0)
    h = _norm_mod(x_ref[0], nw_ref[...], mod_ref[0])
    hb_ref[0] = h.astype(BF16)
    logits = _dot_f32ish(h, wr_ref[0], wr_ref[1])
    lane = lax.broadcasted_iota(jnp.int32, logits.shape, 1).astype(F32)
    logits = jnp.where(lane < N_EXPERTS, logits, -jnp.inf)
    m1 = jnp.max(logits, axis=1, keepdims=True)
    i1 = jnp.min(jnp.where(logits == m1, lane, float(LANES)), axis=1, keepdims=True)
    rest = jnp.where(lane == i1, -jnp.inf, logits)
    m2 = jnp.max(rest, axis=1, keepdims=True)
    i2 = jnp.min(jnp.where(rest == m2, lane, float(LANES)), axis=1, keepdims=True)
    e2 = jnp.exp(m2 - m1)
    w1 = 1.0 / (1.0 + e2)
    w2 = e2 / (1.0 + e2)

    @pl.when(blk == 0)
    def _():
        carry_ref[...] = jnp.zeros_like(carry_ref)

    routed = jnp.where(lane == i1, 1.0, jnp.where(lane == i2, 1.0, 0.0))
    before = carry_ref[...]
    rank = jnp.dot(tri_ref[...], routed.astype(BF16), preferred_element_type=F32) + before
    r1 = jnp.sum(jnp.where(lane == i1, rank, 0.0), axis=1, keepdims=True)
    r2 = jnp.sum(jnp.where(lane == i2, rank, 0.0), axis=1, keepdims=True)
    fields = ((TOK_E, i1), (TOK_E + 1, i2), (TOK_RANK, r1), (TOK_RANK + 1, r2), (TOK_W, w1), (TOK_W + 1, w2))
    tok = jnp.zeros_like(logits)
    for ln, val in fields:
        tok = jnp.where(lane == ln, val, tok)
    tok_ref[0] = tok
    tokt_ref[0] = sum(_nt_dot(sel_ref[...], p) for p in _split_bf16(tok, 3))
    pieces = jnp.zeros_like(logits)
    for k, w in enumerate((w1, w2)):
        rest = w
        for n in range(W_PIECES):
            piece = rest.astype(BF16).astype(F32)
            rest = rest - piece
            pieces = jnp.where(lane == k * W_PIECES + n, piece, pieces)
    tokw_ref[0] = pieces.astype(BF16)
    before_ref[0] = before
    total = before + jnp.sum(routed, axis=0, keepdims=True)
    carry_ref[...] = total
    total_ref[...] = total


def moe_router(x, mod, nw, w_router):
    bsz, s, d = x.shape
    nb = MOE_TILE
    per_b = s // nb
    nblk = bsz * per_b
    wr = jnp.stack(_split_bf16(jnp.zeros((d, LANES), F32).at[:, :N_EXPERTS].set(w_router), 2))
    tri = jnp.tril(jnp.ones((nb, nb), BF16), k=-1)
    sel = jnp.eye(8, LANES, dtype=BF16)
    const2 = lambda i: (0, 0)
    return pl.pallas_call(
        _router_kernel,
        grid=(nblk,),
        in_specs=[pl.BlockSpec((1, nb, d), lambda i: (i // per_b, i % per_b, 0)),
                  pl.BlockSpec((1, 3, d), lambda i: (i // per_b, 0, 0)),
                  pl.BlockSpec((1, d), const2),
                  pl.BlockSpec((2, d, LANES), lambda i: (0, 0, 0)),
                  pl.BlockSpec((nb, nb), const2),
                  pl.BlockSpec((8, LANES), const2)],
        out_specs=[pl.BlockSpec((1, nb, d), lambda i: (i, 0, 0)),
                   pl.BlockSpec((1, nb, LANES), lambda i: (i, 0, 0)),
                   pl.BlockSpec((1, 8, nb), lambda i: (i, 0, 0)),
                   pl.BlockSpec((1, nb, LANES), lambda i: (i, 0, 0)),
                   pl.BlockSpec((1, 1, LANES), lambda i: (i, 0, 0)),
                   pl.BlockSpec((1, LANES), const2)],
        out_shape=[jax.ShapeDtypeStruct((nblk, nb, d), BF16),
                   jax.ShapeDtypeStruct((nblk, nb, LANES), F32),
                   jax.ShapeDtypeStruct((nblk, 8, nb), F32),
                   jax.ShapeDtypeStruct((nblk, nb, LANES), BF16),
                   jax.ShapeDtypeStruct((nblk, 1, LANES), F32),
                   jax.ShapeDtypeStruct((1, LANES), F32)],
        scratch_shapes=[pltpu.VMEM((1, LANES), F32)],
        compiler_params=_params(("arbitrary",)),
        name="moe_router",
    )(x, mod, nw.reshape(1, d), wr, tri, sel)


def _moe_plan(before, total, n_tiles, max_pairs):
    tm = MOE_TILE
    nblk = before.shape[0]
    ntile = (total + tm - 1) // tm
    tile_end = jnp.cumsum(ntile)
    tile_start = tile_end - ntile
    n_valid = tile_end[-1]
    r = jnp.arange(n_tiles, dtype=jnp.int32)
    texp = jnp.minimum(jnp.sum((tile_end[None, :] <= r[:, None]).astype(jnp.int32), axis=1), N_EXPERTS - 1)
    r0 = (r - tile_start[texp]) * tm
    after = jnp.concatenate([before[1:], total[None, :]], axis=0)
    lo = before[:, texp].T
    hi = after[:, texp].T
    meets = (lo < (r0 + tm)[:, None]) & (hi > r0[:, None]) & (r < n_valid)[:, None]
    n_pairs = jnp.sum(meets).astype(jnp.int32)
    slot = jnp.arange(max_pairs, dtype=jnp.int32)

    def pair_list(mat):
        flat = jnp.nonzero(mat.reshape(-1), size=max_pairs, fill_value=0)[0].astype(jnp.int32)
        return jnp.where(slot < n_pairs, flat, flat[jnp.maximum(n_pairs - 1, 0)])

    by_tile = pair_list(meets)
    start = jnp.clip(lo - r0[:, None], 0, tm)
    end = jnp.clip(hi - r0[:, None], 0, tm)
    first = (start // ROW_ALIGN) * ROW_ALIGN
    nwin = jnp.where(meets, (end - first + MOE_WIN - 1) // MOE_WIN, 0).astype(jnp.int32)
    first = first.astype(jnp.int32)
    spare_tile = jnp.where(n_valid < n_tiles, jnp.minimum(n_valid + slot - n_pairs, n_tiles - 1), by_tile // nblk)
    return dict(texp=texp.astype(jnp.int32), n_valid=n_valid.astype(jnp.int32).reshape(1),
                off=(tile_start * tm).astype(jnp.int32), n_pairs=n_pairs.reshape(1),
                g_tile=jnp.where(slot < n_pairs, by_tile // nblk, spare_tile).astype(jnp.int32),
                g_blk=by_tile % nblk,
                g_first=first.reshape(-1)[by_tile], g_nwin=nwin.reshape(-1)[by_tile])


def _slot_positions(tok_fields, k, off_ref, along_lanes):
    if along_lanes:
        e = tok_fields[TOK_E + k:TOK_E + k + 1, :]
        rank = tok_fields[TOK_RANK + k:TOK_RANK + k + 1, :]
    else:
        e = tok_fields[:, TOK_E + k:TOK_E + k + 1]
        rank = tok_fields[:, TOK_RANK + k:TOK_RANK + k + 1]
    pos = rank
    for ex in range(N_EXPERTS):
        pos = pos + jnp.where(e == float(ex), off_ref[ex].astype(F32), 0.0)
    return pos


def _window_rows(want, tile, tm, shape, axis):
    base = pl.multiple_of(jnp.minimum(want, tm - MOE_WIN), ROW_ALIGN)
    local = base + lax.broadcasted_iota(jnp.int32, shape, axis)
    rowid = jnp.where(local >= want, tile * tm + local, -1).astype(F32)
    return base, rowid


def _pair_edges(p, key_ref, n_pairs):
    key = key_ref[p]
    first = jnp.logical_or(p == 0, key_ref[jnp.maximum(p - 1, 0)] != key)
    nxt = jnp.minimum(p + 1, pl.num_programs(0) - 1)
    last = jnp.logical_or(p + 1 >= n_pairs, key_ref[nxt] != key)
    return first, last


def _gather_kernel(tile_ref, blk_ref, first_ref, nwin_ref, np_ref, nvalid_ref, off_ref, tokt_ref, tokw_ref,
                   hb_ref, hs_ref, ws_ref, acc_ref, wacc_ref):
    p = pl.program_id(0)
    tm = hs_ref.shape[0]
    r = tile_ref[p]
    live = p < np_ref[0]
    first, last = _pair_edges(p, tile_ref, np_ref[0])

    @pl.when(first)
    def _():
        acc_ref[...] = jnp.zeros_like(acc_ref)
        wacc_ref[...] = jnp.zeros_like(wacc_ref)

    @pl.when(live)
    def _():
        tokt = tokt_ref[0]
        pos = [_slot_positions(tokt, k, off_ref, True) for k in range(2)]
        lane = lax.broadcasted_iota(jnp.int32, (1, LANES), 1)

        def window(i, carry):
            base, rowid = _window_rows(first_ref[p] + i * MOE_WIN, r, tm, (MOE_WIN, 1), 0)
            onehots = [jnp.where(pos[k] == rowid, 1.0, 0.0).astype(BF16) for k in range(2)]
            rows = pl.ds(base, MOE_WIN)
            acc_ref[rows, :] += jnp.dot(onehots[0] + onehots[1], hb_ref[0], preferred_element_type=F32)
            picked = [jnp.dot(onehots[k], tokw_ref[0], preferred_element_type=F32) for k in range(2)]
            wacc_ref[rows, :] += jnp.where(lane < W_PIECES, picked[0],
                                           jnp.where(lane < 2 * W_PIECES, picked[1], 0.0))
            return carry

        lax.fori_loop(0, nwin_ref[p], window, 0)

    @pl.when(jnp.logical_and(live, last))
    def _():
        hs_ref[...] = acc_ref[...].astype(hs_ref.dtype)
        ws_ref[...] = wacc_ref[...]

    @pl.when(jnp.logical_and(jnp.logical_not(live), r >= nvalid_ref[0]))
    def _():
        hs_ref[...] = jnp.zeros_like(hs_ref)
        ws_ref[...] = jnp.zeros_like(ws_ref)


def moe_gather(plan, tokt, tokw, hb, n_tiles, max_pairs):
    nblk, nb, d = hb.shape
    tm = MOE_TILE
    by_blk3 = lambda p, t, b, *_: (b[p], 0, 0)
    by_tile2 = lambda p, t, *_: (t[p], 0)
    return pl.pallas_call(
        _gather_kernel,
        grid_spec=pltpu.PrefetchScalarGridSpec(
            num_scalar_prefetch=7,
            grid=(max_pairs,),
            in_specs=[pl.BlockSpec((1, 8, nb), by_blk3),
                      pl.BlockSpec((1, nb, LANES), by_blk3),
                      pl.BlockSpec((1, nb, d), by_blk3)],
            out_specs=[pl.BlockSpec((tm, d), by_tile2), pl.BlockSpec((tm, LANES), by_tile2)],
            scratch_shapes=[pltpu.VMEM((tm, d), F32), pltpu.VMEM((tm, LANES), F32)]),
        out_shape=[jax.ShapeDtypeStruct((n_tiles * tm, d), BF16),
                   jax.ShapeDtypeStruct((n_tiles * tm, LANES), F32)],
        compiler_params=_params(("arbitrary",)),
        name="moe_gather",
    )(plan["g_tile"], plan["g_blk"], plan["g_first"], plan["g_nwin"], plan["n_pairs"], plan["n_valid"],
      plan["off"], tokt, tokw, hb)


def _experts_kernel(texp_ref, nvalid_ref, hs_ref, ws_ref, wg_ref, wu_ref, wd_ref, ys_ref, acc_ref, *, tf):
    r = pl.program_id(0)
    valid = r < nvalid_ref[0]

    @pl.when(valid)
    def _():
        hb = hs_ref[...]
        for j in range(wg_ref.shape[2] // tf):
            cols = slice(j * tf, (j + 1) * tf)
            act = (_silu(jnp.dot(hb, wg_ref[0, :, cols], preferred_element_type=F32))
                   * jnp.dot(hb, wu_ref[0, :, cols], preferred_element_type=F32))
            part = jnp.dot(act.astype(BF16), wd_ref[0, cols, :], preferred_element_type=F32)
            if j == 0:
                acc_ref[...] = part
            else:
                acc_ref[...] += part
        w = jnp.sum(ws_ref[...], axis=1, keepdims=True)
        ys_ref[...] = (acc_ref[...] * w).astype(ys_ref.dtype)

    @pl.when(jnp.logical_not(valid))
    def _():
        ys_ref[...] = jnp.zeros_like(ys_ref)


def moe_experts(plan, hs, ws, w_gate, w_up, w_down, tf=256):
    rows, d = hs.shape
    tm = MOE_TILE
    f = w_gate.shape[2]
    rows2 = lambda r, te, nv: (r, 0)
    expert3 = lambda r, te, nv: (te[r], 0, 0)
    return pl.pallas_call(
        functools.partial(_experts_kernel, tf=tf),
        grid_spec=pltpu.PrefetchScalarGridSpec(
            num_scalar_prefetch=2,
            grid=(rows // tm,),
            in_specs=[pl.BlockSpec((tm, d), rows2),
                      pl.BlockSpec((tm, LANES), rows2),
                      pl.BlockSpec((1, d, f), expert3),
                      pl.BlockSpec((1, d, f), expert3),
                      pl.BlockSpec((1, f, d), expert3)],
            out_specs=pl.BlockSpec((tm, d), rows2),
            scratch_shapes=[pltpu.VMEM((tm, d), F32)]),
        out_shape=jax.ShapeDtypeStruct((rows, d), BF16),
        compiler_params=_params(("arbitrary",)),
        name="moe_experts",
    )(plan["texp"], plan["n_valid"], hs, ws, w_gate, w_up, w_down)


def _combine_windows(before, total, off, n_rows):
    nblk = before.shape[0]
    after = jnp.concatenate([before[1:], total[None, :]], axis=0)
    lo = off[None, :] + before
    hi = off[None, :] + after
    first = (lo // ROW_ALIGN) * ROW_ALIGN
    nwin = jnp.where(hi > lo, (hi - first + MOE_WIN - 1) // MOE_WIN, 0)
    ends = jnp.cumsum(nwin, axis=1)
    k = jnp.arange(COMBINE_MAX_WIN, dtype=jnp.int32)
    e_of = jnp.minimum(jnp.sum((ends[:, None, :] <= k[None, :, None]).astype(jnp.int32), axis=2), N_EXPERTS - 1)
    i_of = k[None, :] - jnp.take_along_axis(ends - nwin, e_of, axis=1)
    want = jnp.take_along_axis(first, e_of, axis=1) + i_of * MOE_WIN
    start = jnp.clip(want, 0, n_rows - MOE_WIN)
    limit = jnp.take_along_axis(hi, e_of, axis=1)
    flat = lambda a: a.reshape(-1).astype(jnp.int32)
    return flat(start), flat(want), flat(limit), ends[:, -1].astype(jnp.int32)


def _combine_kernel(start_ref, want_ref, limit_ref, nwin_ref, off_ref, tok_ref, x_ref, mod_ref, fnw_ref,
                    ys_hbm, o_ref, buf_ref, sem_ref, acc_ref):
    b = pl.program_id(0)
    n = nwin_ref[b]
    tok = tok_ref[0]
    pos = [_slot_positions(tok, k, off_ref, False) for k in range(2)]

    def window_copy(k, slot):
        start = pl.multiple_of(start_ref[b * COMBINE_MAX_WIN + k], ROW_ALIGN)
        return pltpu.make_async_copy(ys_hbm.at[pl.ds(start, MOE_WIN), :], buf_ref.at[slot], sem_ref.at[slot])

    @pl.when(n > 0)
    def _():
        window_copy(0, 0).start()

    acc_ref[...] = jnp.zeros_like(acc_ref)

    def window(k, carry):
        slot = k & 1

        @pl.when(k + 1 < n)
        def _():
            window_copy(k + 1, 1 - slot).start()

        window_copy(k, slot).wait()
        w = b * COMBINE_MAX_WIN + k
        local = start_ref[w] + lax.broadcasted_iota(jnp.int32, (1, MOE_WIN), 1)
        rowid = jnp.where(local >= want_ref[w], jnp.where(local < limit_ref[w], local, -1), -1).astype(F32)
        onehot = (jnp.where(pos[0] == rowid, 1.0, 0.0) + jnp.where(pos[1] == rowid, 1.0, 0.0))
        acc_ref[...] += jnp.dot(onehot.astype(BF16), buf_ref[slot], preferred_element_type=F32)
        return carry

    lax.fori_loop(0, n, window, 0)
    o_ref[0] = _rms(x_ref[0] + mod_ref[0][2:3] * acc_ref[...], fnw_ref[...])


def moe_combine(plan, before, total, tok, ys, x, mod, final_nw):
    bsz, s, d = x.shape
    nb = MOE_TILE
    per_b = s // nb
    start, want, limit, nwin = _combine_windows(before, total, plan["off"], ys.shape[0])
    tok_blk = lambda b, *_: (b, 0, 0)
    x_blk = lambda b, *_: (b // per_b, b % per_b, 0)
    return pl.pallas_call(
        _combine_kernel,
        grid_spec=pltpu.PrefetchScalarGridSpec(
            num_scalar_prefetch=5,
            grid=(bsz * per_b,),
            in_specs=[pl.BlockSpec((1, nb, LANES), tok_blk),
                      pl.BlockSpec((1, nb, d), x_blk),
                      pl.BlockSpec((1, 3, d), lambda b, *_: (b // per_b, 0, 0)),
                      pl.BlockSpec((1, d), lambda b, *_: (0, 0)),
                      pl.BlockSpec(memory_space=pl.ANY)],
            out_specs=pl.BlockSpec((1, nb, d), x_blk),
            scratch_shapes=[pltpu.VMEM((2, MOE_WIN, d), BF16), pltpu.SemaphoreType.DMA((2,)),
                            pltpu.VMEM((nb, d), F32)]),
        out_shape=jax.ShapeDtypeStruct(x.shape, F32),
        compiler_params=_params(("arbitrary",)),
        name="moe_combine",
    )(start, want, limit, nwin, plan["off"], tok, x, mod, final_nw.reshape(1, d), ys)


def moe_layer(x, mod, nw, w_router, w_gate, w_up, w_down, final_nw):
    bsz, s, d = x.shape
    nblk = bsz * s // MOE_TILE
    n_tiles = 2 * nblk + N_EXPERTS
    max_pairs = n_tiles + N_EXPERTS * nblk
    hb, tok, tokt, tokw, before, total = moe_router(x, mod, nw, w_router)
    before = before[:, 0, :N_EXPERTS].astype(jnp.int32)
    total = total[0, :N_EXPERTS].astype(jnp.int32)
    plan = _moe_plan(before, total, n_tiles, max_pairs)
    hs, ws = moe_gather(plan, tokt, tokw, hb, n_tiles, max_pairs)
    ys = moe_experts(plan, hs, ws, w_gate, w_up, w_down)
    return moe_combine(plan, before, total, tok, ys, x, mod, final_nw)


def kernel(x, c, e_norm_mix, e_mod_mix_w, e_mod_mix_b, e_w_in, e_fox_fb, e_gla_w2, e_gla_b2, e_gla_norm, e_w_o, e_norm_ffn, e_mod_ffn_w, e_mod_ffn_b, e_ffn_gate, e_ffn_up, e_ffn_down, o_norm_mix, o_mod_mix_w, o_mod_mix_b, o_w_in, o_lam_re, o_lam_im, o_log_dt, o_b_re, o_b_im, o_c_re, o_c_im, o_d_skip, o_w_glu, o_w_o, o_norm_ffn, o_mod_ffn_w, o_mod_ffn_b, o_router, o_exp_gate, o_exp_up, o_exp_down, final_norm):
    bsz, s, d = x.shape

    mod = adaln_mod(c, e_mod_mix_w[0], e_mod_mix_b[0])
    fq, fk, fv, gq, gk, gv, gg, small, cum = layer0_inproj(x, mod, e_norm_mix[0], e_w_in[0], e_fox_fb[0])
    fox = fox_attention(fq, fk, fv, cum)
    gla = gla_attention(gq, gk, gv, gg, small, e_gla_w2[0], e_gla_b2[0], e_gla_norm[0])
    mod_ffn = adaln_mod(c, e_mod_ffn_w[0], e_mod_ffn_b[0])
    x = mixer_out_and_ffn(x, fox, gla, e_w_o[0].astype(BF16), mod, mod_ffn, e_norm_ffn[0],
                          e_ffn_gate[0].astype(BF16), e_ffn_up[0].astype(BF16), e_ffn_down[0].astype(BF16))

    mod = adaln_mod(c, o_mod_mix_w[0], o_mod_mix_b[0])
    s5_tb = 256
    bb_re, bb_im, pow_re, pow_im = s5_discretise(o_lam_re[0], o_lam_im[0], o_log_dt[0], o_b_re[0], o_b_im[0],
                                                 s5_tb // S5_SEG)
    x = s5_layer(x, mod, o_norm_mix[0], o_w_in[0], bb_re, bb_im, pow_re, pow_im, o_c_re[0], o_c_im[0],
                 o_d_skip[0], o_w_glu[0], o_w_o[0], s5_tb)
    mod = adaln_mod(c, o_mod_ffn_w[0], o_mod_ffn_b[0])
    return moe_layer(x, mod, o_norm_ffn[0], o_router[0], o_exp_gate[0].astype(BF16),
                     o_exp_up[0].astype(BF16), o_exp_down[0].astype(BF16), final_norm)
```

```python
import functools

import jax
import jax.numpy as jnp
from jax import lax
from jax.experimental import pallas as pl
from jax.experimental.pallas import tpu as pltpu

F32 = jnp.float32
BF16 = jnp.bfloat16
EPS = 1e-6
NEG_BIG = -1e30
LOG2E = 1.4426950408889634
FOX_BIAS_PARTS = 3
FOX_ROWS = 128

LANES = 128
VMEM_LIMIT = 56 * 1024 * 1024

HEAD_DIM = 64
FOX_HEADS = 8
FOX_DIM = FOX_HEADS * HEAD_DIM
GLA_HEADS = 4
GLA_DK = 64
GLA_DV = 128
GLA_QK = GLA_HEADS * GLA_DK
GLA_V = GLA_HEADS * GLA_DV
GLA_RANK = 16
GLA_TAU = 16.0
GLA_CHUNK = 64
GLA_SUB = 8
S5_GROUP = 16
S5_STATE = 64
S5_GB = 8
N_EXPERTS = 8
SMALL_FF0 = 0
SMALL_LR0 = 8


def _params(sem):
    return pltpu.CompilerParams(dimension_semantics=sem, vmem_limit_bytes=VMEM_LIMIT)


def _log_sigmoid(x):
    return jnp.minimum(x, 0.0) - jnp.log(1.0 + jnp.exp(-jnp.abs(x)))


def _silu(x):
    return x * jax.nn.sigmoid(x)


def _gelu_tanh(x):
    c = 0.7978845608028654
    return 0.5 * x * (1.0 + jnp.tanh(c * (x + 0.044715 * (x * x * x))))


def _rms(x, w):
    return x * lax.rsqrt(jnp.mean(x * x, axis=-1, keepdims=True) + EPS) * w


def _norm_mod(x, nw, mod):
    return _rms(x, nw) * (1.0 + mod[1:2]) + mod[0:1]


def _split_bf16(x, pieces):
    out = []
    for _ in range(pieces):
        p = x.astype(BF16)
        out.append(p)
        x = x - p.astype(F32)
    return out


def _dot_f32ish(a, b_hi, b_lo):
    a_hi, a_lo = _split_bf16(a, 2)
    return (jnp.dot(a_hi, b_hi, preferred_element_type=F32) + jnp.dot(a_hi, b_lo, preferred_element_type=F32)
            + jnp.dot(a_lo, b_hi, preferred_element_type=F32))


def _dot_exact_left(a_bf16, b):
    return sum(jnp.dot(a_bf16, p, preferred_element_type=F32) for p in _split_bf16(b, 3))


def _nt_dot(a, b):
    return lax.dot_general(a, b, (((1,), (1,)), ((), ())), preferred_element_type=F32)


def _mod_kernel(ct_ref, w_ref, b_ref, o_ref):
    s = _silu(ct_ref[...])
    w = w_ref[...]
    for b in range(s.shape[1]):
        o_ref[b:b + 1, :] = jnp.sum(s[:, b:b + 1] * w, axis=0, keepdims=True) + b_ref[...]


def adaln_mod(c, w, b):
    bsz, d = c.shape
    n = w.shape[1]
    tn = 512
    out = pl.pallas_call(
        _mod_kernel,
        grid=(n // tn,),
        in_specs=[pl.BlockSpec((d, bsz), lambda j: (0, 0)),
                  pl.BlockSpec((d, tn), lambda j: (0, j)),
                  pl.BlockSpec((1, tn), lambda j: (0, j))],
        out_specs=pl.BlockSpec((bsz, tn), lambda j: (0, j)),
        out_shape=jax.ShapeDtypeStruct((bsz, n), F32),
        compiler_params=_params(("parallel",)),
        name="adaln_mod",
    )(c.T, w, b.reshape(1, n))
    return out.reshape(bsz, 3, d)


def _inproj_kernel(x_ref, mod_ref, nw_ref, wbig_ref, wsm_ref, fb_ref, tri_ref,
                   fq_ref, fk_ref, fv_ref, gq_ref, gk_ref, gv_ref, gg_ref, sm_ref, cum_ref,
                   carry_ref):
    i = pl.program_id(1)
    tm = x_ref.shape[1]
    h = _norm_mod(x_ref[0], nw_ref[...], mod_ref[0])
    hb = h.astype(BF16)
    col = 0
    for ref in (fq_ref, fk_ref, fv_ref, gq_ref, gk_ref, gv_ref, gg_ref):
        n = ref.shape[2]
        y = jnp.dot(hb, wbig_ref[:, col:col + n], preferred_element_type=F32)
        if ref is fq_ref:
            y = y * (LOG2E * HEAD_DIM ** -0.5)
        ref[0] = y.astype(ref.dtype)
        col += n
    small = _dot_f32ish(h, wsm_ref[0], wsm_ref[1])
    sm_ref[0] = small
    logf = _log_sigmoid(small + fb_ref[...])

    @pl.when(i == 0)
    def _():
        carry_ref[...] = jnp.zeros_like(carry_ref)

    cum = _dot_exact_left(tri_ref[...], logf) + carry_ref[...]
    cum_ref[0] = cum
    carry_ref[...] = cum[tm - 1:tm, :]


def layer0_inproj(x, mod, nw, w_in, fox_fb, tm=512):
    bsz, s, d = x.shape
    c0 = 3 * FOX_DIM
    c_ff = c0
    c_g = c_ff + FOX_HEADS
    c_lr = c_g + 2 * GLA_QK + 2 * GLA_V
    wbig = jnp.concatenate([w_in[:, :c0], w_in[:, c_g:c_lr]], axis=1).astype(BF16)
    wsm = jnp.zeros((d, LANES), F32)
    wsm = wsm.at[:, SMALL_FF0:SMALL_FF0 + FOX_HEADS].set(w_in[:, c_ff:c_g])
    wsm = wsm.at[:, SMALL_LR0:SMALL_LR0 + GLA_RANK].set(w_in[:, c_lr:])
    fb = jnp.zeros((1, LANES), F32).at[0, SMALL_FF0:SMALL_FF0 + FOX_HEADS].set(fox_fb)
    wsm = jnp.stack(_split_bf16(wsm, 2))
    tri = jnp.tril(jnp.ones((tm, tm), BF16))
    widths = (FOX_DIM, FOX_DIM, FOX_DIM, GLA_QK, GLA_QK, GLA_V, GLA_V)
    row = lambda b, i: (b, i, 0)
    const2 = lambda b, i: (0, 0)
    outs = pl.pallas_call(
        _inproj_kernel,
        grid=(bsz, s // tm),
        in_specs=[pl.BlockSpec((1, tm, d), row),
                  pl.BlockSpec((1, 3, d), lambda b, i: (b, 0, 0)),
                  pl.BlockSpec((1, d), const2),
                  pl.BlockSpec(wbig.shape, const2),
                  pl.BlockSpec(wsm.shape, lambda b, i: (0, 0, 0)),
                  pl.BlockSpec((1, LANES), const2),
                  pl.BlockSpec((tm, tm), const2)],
        out_specs=[pl.BlockSpec((1, tm, n), row) for n in widths]
                  + [pl.BlockSpec((1, tm, LANES), row)] * 2,
        out_shape=[jax.ShapeDtypeStruct((bsz, s, n), BF16) for n in widths]
                  + [jax.ShapeDtypeStruct((bsz, s, LANES), F32)] * 2,
        scratch_shapes=[pltpu.VMEM((1, LANES), F32)],
        compiler_params=_params(("parallel", "arbitrary")),
        name="layer0_inproj",
    )(x, mod, nw.reshape(1, d), wbig, wsm, fb, tri)
    return outs


def _fox_kernel(q_ref, k_ref, v_ref, c_ref, o_ref, kaug_ref, vaug_ref, m_ref, acc_ref, qaug_ref):
    pair = pl.program_id(1)
    i = pl.program_id(2)
    t = q_ref.shape[1]
    lane = lax.broadcasted_iota(jnp.int32, (1, LANES), 1)
    heads = (lane < HEAD_DIM, lane >= HEAD_DIM)
    spare = (HEAD_DIM, 0)

    @pl.when(i == 0)
    def _():
        k = k_ref[0].astype(F32)
        v = v_ref[0].astype(F32)
        cum = c_ref[0]
        for h in range(2):
            bias = jnp.sum(jnp.where(lane == 2 * pair + h, cum, 0.0), axis=1, keepdims=True) * (-LOG2E)
            ka = jnp.where(heads[h], k, 0.0)
            rest = bias
            for n in range(FOX_BIAS_PARTS):
                piece = rest.astype(BF16).astype(F32)
                rest = rest - piece
                ka = jnp.where(lane == spare[h] + n, piece, ka)
            kaug_ref[h] = ka.astype(BF16)
            vaug_ref[h, :, :LANES] = jnp.where(heads[h], v, 0.0).astype(BF16)
            vaug_ref[h, :, LANES:] = jnp.broadcast_to(jnp.where(heads[h], 1.0, 0.0), v.shape).astype(BF16)

    q = q_ref[0].astype(F32)
    for h in range(2):
        ones_here = jnp.where(lane >= spare[h], jnp.where(lane < spare[h] + FOX_BIAS_PARTS, 1.0, 0.0), 0.0)
        qaug_ref[h] = jnp.where(heads[h], q, ones_here).astype(BF16)
    m_ref[...] = jnp.full_like(m_ref, NEG_BIG)
    acc_ref[...] = jnp.zeros_like(acc_ref)

    def tile(j, diagonal):
        keys = pl.ds(pl.multiple_of(j * t, t), t)
        for r0 in range(0, t, FOX_ROWS):
            rows = slice(r0, r0 + FOX_ROWS)
            alphas = []
            upd = None
            for h in range(2):
                s = _nt_dot(qaug_ref[h, rows, :], kaug_ref[h, keys, :])
                if diagonal:
                    r = r0 + lax.broadcasted_iota(jnp.int32, (FOX_ROWS, t), 0)
                    c = lax.broadcasted_iota(jnp.int32, (FOX_ROWS, t), 1)
                    s = jnp.where(c <= r, s, NEG_BIG)
                m_prev = m_ref[h, rows, :]
                m_next = jnp.maximum(m_prev, jnp.max(s, axis=1, keepdims=True))
                m_ref[h, rows, :] = m_next
                p = jnp.exp2(s - jnp.concatenate([m_next] * (t // LANES), axis=1)).astype(BF16)
                alphas.append(jnp.exp2(m_prev - m_next))
                d = jnp.dot(p, vaug_ref[h, keys, :], preferred_element_type=F32)
                upd = d if upd is None else upd + d
            alpha = jnp.where(heads[0], alphas[0], alphas[1])
            acc_ref[rows, :] = jnp.concatenate([alpha, alpha], axis=1) * acc_ref[rows, :] + upd

    def body(jj, carry):
        tile(2 * jj, False)
        tile(2 * jj + 1, False)
        return carry

    lax.fori_loop(0, i // 2, body, 0)

    @pl.when(i % 2 == 1)
    def _():
        tile(i - 1, False)

    tile(i, True)
    o_ref[0] = (acc_ref[:, :LANES] / acc_ref[:, LANES:]).astype(o_ref.dtype)


def fox_attention(fq, fk, fv, cum, t=512):
    bsz, s, _ = fq.shape
    n_pair = FOX_HEADS // 2
    seq = lambda b, p, i: (b, 0, p)
    return pl.pallas_call(
        _fox_kernel,
        grid=(bsz, n_pair, s // t),
        in_specs=[pl.BlockSpec((1, t, LANES), lambda b, p, i: (b, i, p)),
                  pl.BlockSpec((1, s, LANES), seq),
                  pl.BlockSpec((1, s, LANES), seq),
                  pl.BlockSpec((1, s, LANES), lambda b, p, i: (b, 0, 0))],
        out_specs=pl.BlockSpec((1, t, LANES), lambda b, p, i: (b, i, p)),
        out_shape=jax.ShapeDtypeStruct((bsz, s, FOX_DIM), BF16),
        scratch_shapes=[pltpu.VMEM((2, s, LANES), BF16), pltpu.VMEM((2, s, 2 * LANES), BF16),
                        pltpu.VMEM((2, t, LANES), F32), pltpu.VMEM((t, 2 * LANES), F32),
                        pltpu.VMEM((2, t, LANES), BF16)],
        compiler_params=_params(("parallel", "parallel", "arbitrary")),
        name="fox_attention",
    )(fq, fk, fv, cum)


def _gla_pair_chunk(q, k, v, la, st, tri, eye, k_scr, b_scr):
    C, SUB = GLA_CHUNK, GLA_SUB
    nsub = C // SUB
    lane = lax.broadcasted_iota(jnp.int32, (1, LANES), 1)
    head_a = lane < GLA_DK
    b = _dot_exact_left(tri, la)
    b_last = b[C - 1:C, :]
    k_scr[...] = k
    b_scr[...] = b
    rowblk = lax.broadcasted_iota(jnp.int32, (C, 1), 0) // SUB

    refs = [b[m * SUB - 1:m * SUB, :] for m in range(1, nsub)]
    rsel = refs[-1]
    for m in range(nsub - 2, 0, -1):
        rsel = jnp.where(rowblk == m, refs[m - 1], rsel)
    qt = q * jnp.exp(jnp.minimum(b - rsel, 0.0))
    zero = jnp.zeros_like(q)
    qs = jnp.concatenate([jnp.where(rowblk == m, qt, zero) for m in range(1, nsub)], axis=1)
    ks = jnp.concatenate(
        [jnp.where(rowblk < m, k * jnp.exp(jnp.minimum(refs[m - 1] - b, 0.0)), zero)
         for m in range(1, nsub)], axis=1)
    lane3 = lax.broadcasted_iota(jnp.int32, (1, (nsub - 1) * LANES), 1)
    head_a3 = (lane3 & (LANES - 1)) < GLA_DK
    zero3 = jnp.zeros_like(ks)
    kstack = jnp.concatenate([jnp.where(head_a3, ks, zero3), jnp.where(head_a3, zero3, ks)], axis=0)
    a_off = _nt_dot(qs.astype(BF16), kstack.astype(BF16))

    row16 = lax.broadcasted_iota(jnp.int32, (SUB, 1), 0)
    blocks = []
    for i in range(nsub):
        qb = q[i * SUB:(i + 1) * SUB, :]
        bb = b[i * SUB:(i + 1) * SUB, :]
        d = jnp.zeros((SUB, LANES), F32)
        for sp in range(SUB):
            srow = i * SUB + sp
            e = qb * k_scr[srow:srow + 1, :] * jnp.exp(jnp.minimum(bb - b_scr[srow:srow + 1, :], 0.0))
            da = jnp.sum(jnp.where(head_a, e, 0.0), axis=1, keepdims=True)
            db = jnp.sum(jnp.where(head_a, 0.0, e), axis=1, keepdims=True)
            live = row16 >= sp
            da = jnp.where(live, da, 0.0)
            db = jnp.where(live, db, 0.0)
            d = jnp.where(lane == srow, da, jnp.where(lane == GLA_DK + srow, db, d))
        blocks.append(d)
    a_pair = a_off + jnp.concatenate(blocks, axis=0)

    lane_v = lax.broadcasted_iota(jnp.int32, (1, 2 * GLA_DV), 1)
    first_v = lane_v < GLA_DV
    zv = jnp.zeros_like(v)
    vbd = jnp.concatenate([jnp.where(first_v, v, zv), jnp.where(first_v, zv, v)], axis=0)
    qh = q * jnp.exp(b)
    o = (jnp.dot(a_pair.astype(BF16), vbd, preferred_element_type=F32)
         + _nt_dot(qh.astype(BF16), st.astype(BF16)))

    k_end = k * jnp.exp(b_last - b)
    v_t = _nt_dot(eye, v).astype(BF16)
    inc = jnp.dot(v_t, k_end.astype(BF16), preferred_element_type=F32)
    row_v = lax.broadcasted_iota(jnp.int32, (2 * GLA_DV, 1), 0)
    same_head = (row_v // GLA_DV) == (lane // GLA_DK)
    st_new = st * jnp.exp(b_last) + jnp.where(same_head, inc, 0.0)
    return o, st_new


def _gla_kernel(gq_ref, gk_ref, gv_ref, gg_ref, sm_ref, w2_ref, b2_ref, gn_ref, tri_ref, eye_ref,
                o_ref, st_ref, la_ref, k_scr, b_scr):
    i = pl.program_id(1)
    tm = gq_ref.shape[1]
    n_pair = GLA_HEADS // 2
    scale = GLA_DK ** -0.5

    @pl.when(i == 0)
    def _():
        st_ref[...] = jnp.zeros_like(st_ref)

    z = _dot_f32ish(sm_ref[0], w2_ref[0], w2_ref[1]) + b2_ref[...]
    la_ref[...] = _log_sigmoid(z) * (1.0 / GLA_TAU)
    gn = gn_ref[...]

    def chunk(c, carry):
        base = pl.multiple_of(c * GLA_CHUNK, GLA_CHUNK)
        rows = pl.ds(base, GLA_CHUNK)
        for hp in range(n_pair):
            ql = slice(hp * LANES, (hp + 1) * LANES)
            vl = slice(hp * 2 * GLA_DV, (hp + 1) * 2 * GLA_DV)
            q = gq_ref[0, rows, ql].astype(F32) * scale
            k = gk_ref[0, rows, ql].astype(F32)
            v = gv_ref[0, rows, vl]
            o, st_new = _gla_pair_chunk(q, k, v, la_ref[rows, ql], st_ref[hp], tri_ref[...],
                                        eye_ref[...], k_scr.at[hp], b_scr.at[hp])
            st_ref[hp] = st_new
            gate = _silu(gg_ref[0, rows, vl].astype(F32))
            halves = [_rms(o[:, h * GLA_DV:(h + 1) * GLA_DV], gn) for h in range(2)]
            o_ref[0, rows, vl] = (jnp.concatenate(halves, axis=1) * gate).astype(o_ref.dtype)
        return carry

    lax.fori_loop(0, tm // GLA_CHUNK, chunk, 0, unroll=4)


def gla_attention(gq, gk, gv, gg, small, w2, b2, gnorm, tm=512):
    bsz, s, _ = gq.shape
    w2p = jnp.zeros((LANES, GLA_QK), F32).at[SMALL_LR0:SMALL_LR0 + GLA_RANK].set(w2)
    w2p = jnp.stack(_split_bf16(w2p, 2))
    tri = jnp.tril(jnp.ones((GLA_CHUNK, GLA_CHUNK), BF16))
    eye = jnp.eye(2 * GLA_DV, dtype=BF16)
    row = lambda b, i: (b, i, 0)
    const2 = lambda b, i: (0, 0)
    return pl.pallas_call(
        _gla_kernel,
        grid=(bsz, s // tm),
        in_specs=[pl.BlockSpec((1, tm, GLA_QK), row), pl.BlockSpec((1, tm, GLA_QK), row),
                  pl.BlockSpec((1, tm, GLA_V), row), pl.BlockSpec((1, tm, GLA_V), row),
                  pl.BlockSpec((1, tm, LANES), row),
                  pl.BlockSpec(w2p.shape, lambda b, i: (0, 0, 0)), pl.BlockSpec((1, GLA_QK), const2),
                  pl.BlockSpec((1, GLA_DV), const2), pl.BlockSpec(tri.shape, const2),
                  pl.BlockSpec(eye.shape, const2)],
        out_specs=pl.BlockSpec((1, tm, GLA_V), row),
        out_shape=jax.ShapeDtypeStruct((bsz, s, GLA_V), BF16),
        scratch_shapes=[pltpu.VMEM((GLA_HEADS // 2, 2 * GLA_DV, LANES), F32),
                        pltpu.VMEM((tm, GLA_QK), F32),
                        pltpu.VMEM((GLA_HEADS // 2, GLA_CHUNK, LANES), F32),
                        pltpu.VMEM((GLA_HEADS // 2, GLA_CHUNK, LANES), F32)],
        compiler_params=_params(("parallel", "arbitrary")),
        name="gla_attention",
    )(gq, gk, gv, gg, small, w2p, b2.reshape(1, GLA_QK), gnorm.reshape(1, GLA_DV), tri, eye)


def _mix_ffn_kernel(x_ref, a_ref, b_ref, wo_ref, modm_ref, modf_ref, nw_ref, wg_ref, wu_ref, wd_ref,
                    o_ref, acc_ref, *, tf):
    na = a_ref.shape[2]
    y = (jnp.dot(a_ref[0], wo_ref[:na, :], preferred_element_type=F32)
         + jnp.dot(b_ref[0], wo_ref[na:, :], preferred_element_type=F32))
    x1 = x_ref[0] + modm_ref[0][2:3] * y
    hb = _norm_mod(x1, nw_ref[...], modf_ref[0]).astype(BF16)
    for j in range(wg_ref.shape[1] // tf):
        cols = slice(j * tf, (j + 1) * tf)
        act = (_silu(jnp.dot(hb, wg_ref[:, cols], preferred_element_type=F32))
               * jnp.dot(hb, wu_ref[:, cols], preferred_element_type=F32))
        part = jnp.dot(act.astype(BF16), wd_ref[cols, :], preferred_element_type=F32)
        if j == 0:
            acc_ref[...] = part
        else:
            acc_ref[...] += part
    o_ref[0] = x1 + modf_ref[0][2:3] * acc_ref[...]


def mixer_out_and_ffn(x, a, b, w_o, mod_mix, mod_ffn, nw, w_gate, w_up, w_down, tm=512, tf=256):
    bsz, s, d = x.shape
    row = lambda bb, i: (bb, i, 0)
    per_b = lambda bb, i: (bb, 0, 0)
    const2 = lambda bb, i: (0, 0)
    resident = lambda arr: pl.BlockSpec(arr.shape, const2, pipeline_mode=pl.Buffered(1))
    return pl.pallas_call(
        functools.partial(_mix_ffn_kernel, tf=tf),
        grid=(bsz, s // tm),
        in_specs=[pl.BlockSpec((1, tm, d), row),
                  pl.BlockSpec((1, tm, a.shape[2]), row),
                  pl.BlockSpec((1, tm, b.shape[2]), row),
                  resident(w_o),
                  pl.BlockSpec((1, 3, d), per_b),
                  pl.BlockSpec((1, 3, d), per_b),
                  pl.BlockSpec((1, d), const2),
                  resident(w_gate), resident(w_up), resident(w_down)],
        out_specs=pl.BlockSpec((1, tm, d), row),
        out_shape=jax.ShapeDtypeStruct(x.shape, F32),
        scratch_shapes=[pltpu.VMEM((tm, d), F32)],
        compiler_params=_params(("parallel", "parallel")),
        name="mixer_out_and_ffn",
    )(x, a, b, w_o, mod_mix, mod_ffn, nw.reshape(1, d), w_gate, w_up, w_down)


S5_SEG = 8


def _s5_disc_kernel(lre_ref, lim_ref, ldt_ref, bre_ref, bim_ref,
                    bbre_ref, bbim_ref, pre_ref, pim_ref):
    lre = lre_ref[...]
    lim = lim_ref[...]
    dt = jnp.exp(ldt_ref[...])
    mag = jnp.exp(lre * dt)
    a_re = mag * jnp.cos(lim * dt)
    a_im = mag * jnp.sin(lim * dt)
    den = lre * lre + lim * lim
    nr = a_re - 1.0
    ni = a_im
    f_re = (nr * lre + ni * lim) / den
    f_im = (ni * lre - nr * lim) / den
    bre = bre_ref[...]
    bim = bim_ref[...]
    bbre_ref[...] = f_re[:, None, :] * bre - f_im[:, None, :] * bim
    bbim_ref[...] = f_re[:, None, :] * bim + f_im[:, None, :] * bre
    n = (lax.broadcasted_iota(jnp.int32, pre_ref.shape, 0) + 1).astype(F32)
    mag_n = jnp.exp(n * (lre * dt)[None])
    ang_n = n * (lim * dt)[None]
    pre_ref[...] = mag_n * jnp.cos(ang_n)
    pim_ref[...] = mag_n * jnp.sin(ang_n)


def s5_discretise(lam_re, lam_im, log_dt, b_re, b_im, seg_len):
    g, p = lam_re.shape
    k = b_re.shape[2]
    return pl.pallas_call(
        _s5_disc_kernel,
        out_shape=[jax.ShapeDtypeStruct((g, k, p), F32)] * 2 + [jax.ShapeDtypeStruct((seg_len, g, p), F32)] * 2,
        name="s5_discretise",
    )(lam_re, lam_im, log_dt.reshape(g, 1), b_re.transpose(0, 2, 1), b_im.transpose(0, 2, 1))


def _s5_kernel(x_ref, mod_ref, nw_ref, perm_ref, permt_ref, win_ref, bbd_ref, cbd_ref, apow_ref, dsk_ref,
               wglu_ref, wo_ref, o_ref, carry_ref, bu_ref, xs_ref, xsb_ref):
    i = pl.program_id(1)
    tb = x_ref.shape[1]
    nblk = bu_ref.shape[0]
    nch = bu_ref.shape[2] // (2 * LANES)
    seg_len = tb // S5_SEG
    re_l = [slice(ch * LANES, (ch + 1) * LANES) for ch in range(nch)]
    im_l = [slice((nch + ch) * LANES, (nch + ch + 1) * LANES) for ch in range(nch)]

    @pl.when(i == 0)
    def _():
        carry_ref[...] = jnp.zeros_like(carry_ref)

    x = x_ref[0]
    h = _norm_mod(x, nw_ref[...], mod_ref[0]).astype(BF16)
    hp = jnp.dot(perm_ref[...], h, preferred_element_type=F32).astype(BF16)
    u = jnp.dot(hp, win_ref[...], preferred_element_type=F32)
    ub = u.astype(BF16)
    for jb in range(nblk):
        bu_ref[jb] = jnp.dot(ub[:, jb * LANES:(jb + 1) * LANES], bbd_ref[jb], preferred_element_type=F32)

    seg_row = lax.broadcasted_iota(jnp.int32, (S5_SEG, 1), 0)

    def channel_block(jb, blk_carry):
        def bcast(n, lanes):
            return jnp.broadcast_to(apow_ref[jb, n:n + 1, lanes], (S5_SEG, LANES))

        a1 = [(bcast(0, re_l[ch]), bcast(0, im_l[ch])) for ch in range(nch)]

        def step(t, state):
            rows = pl.ds(pl.multiple_of(t * S5_SEG, S5_SEG), S5_SEG)
            out = []
            for ch in range(nch):
                xr, xi = state[2 * ch], state[2 * ch + 1]
                ar, ai = a1[ch]
                nr = ar * xr - ai * xi + bu_ref[jb, rows, re_l[ch]]
                ni = ar * xi + ai * xr + bu_ref[jb, rows, im_l[ch]]
                xs_ref[jb, rows, re_l[ch]] = nr
                xs_ref[jb, rows, im_l[ch]] = ni
                out += [nr, ni]
            return tuple(out)

        zero = jnp.zeros((S5_SEG, LANES), F32)
        ends = lax.fori_loop(0, seg_len, step, (zero,) * (2 * nch), unroll=4)

        starts = []
        for ch in range(nch):
            alr = apow_ref[jb, seg_len - 1:seg_len, re_l[ch]]
            ali = apow_ref[jb, seg_len - 1:seg_len, im_l[ch]]
            er, ei = ends[2 * ch], ends[2 * ch + 1]
            zr = carry_ref[jb, :, re_l[ch]]
            zi = carry_ref[jb, :, im_l[ch]]
            tr = jnp.zeros((S5_SEG, LANES), F32)
            ti = jnp.zeros((S5_SEG, LANES), F32)
            for s in range(S5_SEG):
                tr = jnp.where(seg_row == s, zr, tr)
                ti = jnp.where(seg_row == s, zi, ti)
                zr, zi = (alr * zr - ali * zi + er[s:s + 1], alr * zi + ali * zr + ei[s:s + 1])
            carry_ref[jb, :, re_l[ch]] = zr
            carry_ref[jb, :, im_l[ch]] = zi
            starts.append((jnp.concatenate([tr, tr], axis=0), jnp.concatenate([ti, ti], axis=0)))

        for t2 in range(seg_len // 2):
            rows = slice(t2 * 2 * S5_SEG, (t2 + 1) * 2 * S5_SEG)
            for ch in range(nch):
                pr = jnp.concatenate([bcast(2 * t2, re_l[ch]), bcast(2 * t2 + 1, re_l[ch])], axis=0)
                pi = jnp.concatenate([bcast(2 * t2, im_l[ch]), bcast(2 * t2 + 1, im_l[ch])], axis=0)
                zr, zi = starts[ch]
                xsb_ref[jb, rows, re_l[ch]] = (xs_ref[jb, rows, re_l[ch]] + pr * zr - pi * zi).astype(BF16)
                xsb_ref[jb, rows, im_l[ch]] = (xs_ref[jb, rows, im_l[ch]] + pr * zi + pi * zr).astype(BF16)
        return blk_carry

    lax.fori_loop(0, nblk, channel_block, 0)

    ys = [jnp.dot(xsb_ref[jb], cbd_ref[jb], preferred_element_type=F32) for jb in range(nblk)]
    y = jnp.concatenate(ys, axis=1) + dsk_ref[...] * u
    y = _gelu_tanh(y)
    y = y * jax.nn.sigmoid(jnp.dot(y.astype(BF16), wglu_ref[...], preferred_element_type=F32))
    z = jnp.dot(y.astype(BF16), wo_ref[...], preferred_element_type=F32)
    z_hi = z.astype(BF16)
    z_lo = (z - z_hi.astype(F32)).astype(BF16)
    z = (jnp.dot(permt_ref[...], z_hi, preferred_element_type=F32)
         + jnp.dot(permt_ref[...], z_lo, preferred_element_type=F32))
    o_ref[0] = x + mod_ref[0][2:3] * z


def s5_layer(x, mod, nw, w_in, bb_re, bb_im, pow_re, pow_im, c_re, c_im, d_skip, w_glu, w_o, tb):
    bsz, s, d = x.shape
    g, k, p = bb_re.shape
    nblk = g // S5_GB
    seg_len = pow_re.shape[0]
    assert tb == seg_len * S5_SEG
    eye = jnp.eye(S5_GB, dtype=F32)

    def blockdiag_in(bb):
        t = jnp.einsum('jgkp,gh->jgkhp', bb.reshape(nblk, S5_GB, k, p), eye)
        return t.reshape(nblk, S5_GB * k, S5_GB * p)

    def blockdiag_out(c):
        t = jnp.einsum('jgkp,gh->jgphk', c.reshape(nblk, S5_GB, k, p), eye)
        return t.reshape(nblk, S5_GB * p, S5_GB * k)

    bbd = jnp.concatenate([blockdiag_in(bb_re), blockdiag_in(bb_im)], axis=2).astype(BF16)
    cbd = jnp.concatenate([blockdiag_out(c_re), -blockdiag_out(c_im)], axis=1).astype(BF16)
    half = S5_GB * p
    apow = jnp.concatenate([pow_re.reshape(seg_len, nblk, half), pow_im.reshape(seg_len, nblk, half)],
                           axis=2).transpose(1, 0, 2)
    new_row = jnp.arange(tb)
    old_row = (new_row % S5_SEG) * seg_len + new_row // S5_SEG
    perm = (old_row[:, None] == jnp.arange(tb)[None, :]).astype(BF16)
    row = lambda b, i: (b, i, 0)
    const2 = lambda b, i: (0, 0)
    const3 = lambda b, i: (0, 0, 0)
    return pl.pallas_call(
        _s5_kernel,
        grid=(bsz, s // tb),
        in_specs=[pl.BlockSpec((1, tb, d), row),
                  pl.BlockSpec((1, 3, d), lambda b, i: (b, 0, 0)),
                  pl.BlockSpec((1, d), const2),
                  pl.BlockSpec((tb, tb), const2), pl.BlockSpec((tb, tb), const2),
                  pl.BlockSpec((d, d), const2),
                  pl.BlockSpec(bbd.shape, const3),
                  pl.BlockSpec(cbd.shape, const3),
                  pl.BlockSpec(apow.shape, const3),
                  pl.BlockSpec((1, d), const2),
                  pl.BlockSpec((d, d), const2), pl.BlockSpec((d, d), const2)],
        out_specs=pl.BlockSpec((1, tb, d), row),
        out_shape=jax.ShapeDtypeStruct(x.shape, F32),
        scratch_shapes=[pltpu.VMEM((nblk, 1, 2 * half), F32),
                        pltpu.VMEM((nblk, tb, 2 * half), F32),
                        pltpu.VMEM((nblk, tb, 2 * half), F32),
                        pltpu.VMEM((nblk, tb, 2 * half), BF16)],
        compiler_params=_params(("parallel", "arbitrary")),
        name="s5_layer",
    )(x, mod, nw.reshape(1, d), perm, perm.T, w_in.astype(BF16), bbd, cbd, apow, d_skip.reshape(1, d),
      w_glu.astype(BF16), w_o.astype(BF16))


TOK_E, TOK_RANK, TOK_W = 0, 2, 4
W_PIECES = 3
MOE_TILE = 512
MOE_WIN = 192
ROW_ALIGN = 16
COMBINE_MAX_WIN = (2 * MOE_TILE + N_EXPERTS * (ROW_ALIGN - 1)) // MOE_WIN + N_EXPERTS + 1


def _router_kernel(x_ref, mod_ref, nw_ref, wr_ref, tri_ref, sel_ref,
                   hb_ref, tok_ref, tokt_ref, tokw_ref, before_ref, total_ref, carry_ref):
    blk = pl.program_id(0)
    h = _norm_mod(x_ref[0], nw_ref[...], mod_ref[0])
    hb_ref[0] = h.astype(BF16)
    logits = _dot_f32ish(h, wr_ref[0], wr_ref[1])
    lane = lax.broadcasted_iota(jnp.int32, logits.shape, 1).astype(F32)
    logits = jnp.where(lane < N_EXPERTS, logits, -jnp.inf)
    m1 = jnp.max(logits, axis=1, keepdims=True)
    i1 = jnp.min(jnp.where(logits == m1, lane, float(LANES)), axis=1, keepdims=True)
    rest = jnp.where(lane == i1, -jnp.inf, logits)
    m2 = jnp.max(rest, axis=1, keepdims=True)
    i2 = jnp.min(jnp.where(rest == m2, lane, float(LANES)), axis=1, keepdims=True)
    e2 = jnp.exp(m2 - m1)
    w1 = 1.0 / (1.0 + e2)
    w2 = e2 / (1.0 + e2)

    @pl.when(blk == 0)
    def _():
        carry_ref[...] = jnp.zeros_like(carry_ref)

    routed = jnp.where(lane == i1, 1.0, jnp.where(lane == i2, 1.0, 0.0))
    before = carry_ref[...]
    rank = jnp.dot(tri_ref[...], routed.astype(BF16), preferred_element_type=F32) + before
    r1 = jnp.sum(jnp.where(lane == i1, rank, 0.0), axis=1, keepdims=True)
    r2 = jnp.sum(jnp.where(lane == i2, rank, 0.0), axis=1, keepdims=True)
    fields = ((TOK_E, i1), (TOK_E + 1, i2), (TOK_RANK, r1), (TOK_RANK + 1, r2), (TOK_W, w1), (TOK_W + 1, w2))
    tok = jnp.zeros_like(logits)
    for ln, val in fields:
        tok = jnp.where(lane == ln, val, tok)
    tok_ref[0] = tok
    tokt_ref[0] = sum(_nt_dot(sel_ref[...], p) for p in _split_bf16(tok, 3))
    pieces = jnp.zeros_like(logits)
    for k, w in enumerate((w1, w2)):
        rest = w
        for n in range(W_PIECES):
            piece = rest.astype(BF16).astype(F32)
            rest = rest - piece
            pieces = jnp.where(lane == k * W_PIECES + n, piece, pieces)
    tokw_ref[0] = pieces.astype(BF16)
    before_ref[0] = before
    total = before + jnp.sum(routed, axis=0, keepdims=True)
    carry_ref[...] = total
    total_ref[...] = total


def moe_router(x, mod, nw, w_router):
    bsz, s, d = x.shape
    nb = MOE_TILE
    per_b = s // nb
    nblk = bsz * per_b
    wr = jnp.stack(_split_bf16(jnp.zeros((d, LANES), F32).at[:, :N_EXPERTS].set(w_router), 2))
    tri = jnp.tril(jnp.ones((nb, nb), BF16), k=-1)
    sel = jnp.eye(8, LANES, dtype=BF16)
    const2 = lambda i: (0, 0)
    return pl.pallas_call(
        _router_kernel,
        grid=(nblk,),
        in_specs=[pl.BlockSpec((1, nb, d), lambda i: (i // per_b, i % per_b, 0)),
                  pl.BlockSpec((1, 3, d), lambda i: (i // per_b, 0, 0)),
                  pl.BlockSpec((1, d), const2),
                  pl.BlockSpec((2, d, LANES), lambda i: (0, 0, 0)),
                  pl.BlockSpec((nb, nb), const2),
                  pl.BlockSpec((8, LANES), const2)],
        out_specs=[pl.BlockSpec((1, nb, d), lambda i: (i, 0, 0)),
                   pl.BlockSpec((1, nb, LANES), lambda i: (i, 0, 0)),
                   pl.BlockSpec((1, 8, nb), lambda i: (i, 0, 0)),
                   pl.BlockSpec((1, nb, LANES), lambda i: (i, 0, 0)),
                   pl.BlockSpec((1, 1, LANES), lambda i: (i, 0, 0)),
                   pl.BlockSpec((1, LANES), const2)],
        out_shape=[jax.ShapeDtypeStruct((nblk, nb, d), BF16),
                   jax.ShapeDtypeStruct((nblk, nb, LANES), F32),
                   jax.ShapeDtypeStruct((nblk, 8, nb), F32),
                   jax.ShapeDtypeStruct((nblk, nb, LANES), BF16),
                   jax.ShapeDtypeStruct((nblk, 1, LANES), F32),
                   jax.ShapeDtypeStruct((1, LANES), F32)],
        scratch_shapes=[pltpu.VMEM((1, LANES), F32)],
        compiler_params=_params(("arbitrary",)),
        name="moe_router",
    )(x, mod, nw.reshape(1, d), wr, tri, sel)


def _moe_plan(before, total, n_tiles, max_pairs):
    tm = MOE_TILE
    nblk = before.shape[0]
    ntile = (total + tm - 1) // tm
    tile_end = jnp.cumsum(ntile)
    tile_start = tile_end - ntile
    n_valid = tile_end[-1]
    r = jnp.arange(n_tiles, dtype=jnp.int32)
    texp = jnp.minimum(jnp.sum((tile_end[None, :] <= r[:, None]).astype(jnp.int32), axis=1), N_EXPERTS - 1)
    r0 = (r - tile_start[texp]) * tm
    after = jnp.concatenate([before[1:], total[None, :]], axis=0)
    lo = before[:, texp].T
    hi = after[:, texp].T
    meets = (lo < (r0 + tm)[:, None]) & (hi > r0[:, None]) & (r < n_valid)[:, None]
    n_pairs = jnp.sum(meets).astype(jnp.int32)
    slot = jnp.arange(max_pairs, dtype=jnp.int32)

    def pair_list(mat):
        flat = jnp.nonzero(mat.reshape(-1), size=max_pairs, fill_value=0)[0].astype(jnp.int32)
        return jnp.where(slot < n_pairs, flat, flat[jnp.maximum(n_pairs - 1, 0)])

    by_tile = pair_list(meets)
    start = jnp.clip(lo - r0[:, None], 0, tm)
    end = jnp.clip(hi - r0[:, None], 0, tm)
    first = (start // ROW_ALIGN) * ROW_ALIGN
    nwin = jnp.where(meets, (end - first + MOE_WIN - 1) // MOE_WIN, 0).astype(jnp.int32)
    first = first.astype(jnp.int32)
    per_tile = jnp.sum(meets, axis=1).astype(jnp.int32)
    return dict(texp=texp.astype(jnp.int32), n_valid=n_valid.astype(jnp.int32).reshape(1),
                off=(tile_start * tm).astype(jnp.int32),
                g_start=(jnp.cumsum(per_tile) - per_tile).astype(jnp.int32), g_count=per_tile,
                g_blk=by_tile % nblk,
                g_first=first.reshape(-1)[by_tile], g_nwin=nwin.reshape(-1)[by_tile])


def _slot_positions(tok_fields, k, off_ref, along_lanes):
    if along_lanes:
        e = tok_fields[TOK_E + k:TOK_E + k + 1, :]
        rank = tok_fields[TOK_RANK + k:TOK_RANK + k + 1, :]
    else:
        e = tok_fields[:, TOK_E + k:TOK_E + k + 1]
        rank = tok_fields[:, TOK_RANK + k:TOK_RANK + k + 1]
    pos = rank
    for ex in range(N_EXPERTS):
        pos = pos + jnp.where(e == float(ex), off_ref[ex].astype(F32), 0.0)
    return pos


def _window_rows(want, tile, tm, shape, axis):
    base = pl.multiple_of(jnp.minimum(want, tm - MOE_WIN), ROW_ALIGN)
    local = base + lax.broadcasted_iota(jnp.int32, shape, axis)
    rowid = jnp.where(local >= want, tile * tm + local, -1).astype(F32)
    return base, rowid


def _gather_kernel(start_ref, count_ref, blk_ref, first_ref, nwin_ref, off_ref, tokt_hbm, tokw_hbm, hb_hbm,
                   hs_ref, ws_ref, tokt_buf, tokw_buf, hb_buf, sem_ref, acc_ref, wacc_ref):
    r = pl.program_id(0)
    tm = hs_ref.shape[0]
    p0 = start_ref[r]
    n = count_ref[r]
    lane = lax.broadcasted_iota(jnp.int32, (1, LANES), 1)

    def block_copies(p, slot):
        blk = blk_ref[p]
        return [pltpu.make_async_copy(src.at[blk], dst.at[slot], sem_ref.at[which, slot])
                for which, (src, dst) in enumerate(((tokt_hbm, tokt_buf), (tokw_hbm, tokw_buf), (hb_hbm, hb_buf)))]

    @pl.when(n > 0)
    def _():
        for c in block_copies(p0, 0):
            c.start()

    acc_ref[...] = jnp.zeros_like(acc_ref)
    wacc_ref[...] = jnp.zeros_like(wacc_ref)

    def pair(k, carry):
        p = p0 + k
        slot = k & 1

        @pl.when(k + 1 < n)
        def _():
            for c in block_copies(p + 1, 1 - slot):
                c.start()

        for c in block_copies(p, slot):
            c.wait()
        tokt = tokt_buf[slot]
        pos = [_slot_positions(tokt, s, off_ref, True) for s in range(2)]

        def window(i, wcarry):
            base, rowid = _window_rows(first_ref[p] + i * MOE_WIN, r, tm, (MOE_WIN, 1), 0)
            onehots = [jnp.where(pos[s] == rowid, 1.0, 0.0).astype(BF16) for s in range(2)]
            rows = pl.ds(base, MOE_WIN)
            acc_ref[rows, :] += jnp.dot(onehots[0] + onehots[1], hb_buf[slot], preferred_element_type=F32)
            picked = [jnp.dot(onehots[s], tokw_buf[slot], preferred_element_type=F32) for s in range(2)]
            wacc_ref[rows, :] += jnp.where(lane < W_PIECES, picked[0],
                                           jnp.where(lane < 2 * W_PIECES, picked[1], 0.0))
            return wcarry

        lax.fori_loop(0, nwin_ref[p], window, 0)
        return carry

    lax.fori_loop(0, n, pair, 0)
    hs_ref[...] = acc_ref[...].astype(hs_ref.dtype)
    ws_ref[...] = wacc_ref[...]


def moe_gather(plan, tokt, tokw, hb, n_tiles):
    nblk, nb, d = hb.shape
    tm = MOE_TILE
    by_tile2 = lambda r, *_: (r, 0)
    hbm = pl.BlockSpec(memory_space=pl.ANY)
    return pl.pallas_call(
        _gather_kernel,
        grid_spec=pltpu.PrefetchScalarGridSpec(
            num_scalar_prefetch=6,
            grid=(n_tiles,),
            in_specs=[hbm, hbm, hbm],
            out_specs=[pl.BlockSpec((tm, d), by_tile2), pl.BlockSpec((tm, LANES), by_tile2)],
            scratch_shapes=[pltpu.VMEM((2, 8, nb), F32), pltpu.VMEM((2, nb, LANES), BF16),
                            pltpu.VMEM((2, nb, d), BF16), pltpu.SemaphoreType.DMA((3, 2)),
                            pltpu.VMEM((tm, d), F32), pltpu.VMEM((tm, LANES), F32)]),
        out_shape=[jax.ShapeDtypeStruct((n_tiles * tm, d), BF16),
                   jax.ShapeDtypeStruct((n_tiles * tm, LANES), F32)],
        compiler_params=_params(("arbitrary",)),
        name="moe_gather",
    )(plan["g_start"], plan["g_count"], plan["g_blk"], plan["g_first"], plan["g_nwin"], plan["off"],
      tokt, tokw, hb)


def _experts_kernel(texp_ref, nvalid_ref, hs_ref, ws_ref, wg_ref, wu_ref, wd_ref, ys_ref, acc_ref, *, tf):
    r = pl.program_id(0)
    valid = r < nvalid_ref[0]

    @pl.when(valid)
    def _():
        hb = hs_ref[...]
        for j in range(wg_ref.shape[2] // tf):
            cols = slice(j * tf, (j + 1) * tf)
            act = (_silu(jnp.dot(hb, wg_ref[0, :, cols], preferred_element_type=F32))
                   * jnp.dot(hb, wu_ref[0, :, cols], preferred_element_type=F32))
            part = jnp.dot(act.astype(BF16), wd_ref[0, cols, :], preferred_element_type=F32)
            if j == 0:
                acc_ref[...] = part
            else:
                acc_ref[...] += part
        w = jnp.sum(ws_ref[...], axis=1, keepdims=True)
        ys_ref[...] = (acc_ref[...] * w).astype(ys_ref.dtype)

    @pl.when(jnp.logical_not(valid))
    def _():
        ys_ref[...] = jnp.zeros_like(ys_ref)


def moe_experts(plan, hs, ws, w_gate, w_up, w_down, tf=256):
    rows, d = hs.shape
    tm = MOE_TILE
    f = w_gate.shape[2]
    rows2 = lambda r, te, nv: (r, 0)
    expert3 = lambda r, te, nv: (te[r], 0, 0)
    return pl.pallas_call(
        functools.partial(_experts_kernel, tf=tf),
        grid_spec=pltpu.PrefetchScalarGridSpec(
            num_scalar_prefetch=2,
            grid=(rows // tm,),
            in_specs=[pl.BlockSpec((tm, d), rows2),
                      pl.BlockSpec((tm, LANES), rows2),
                      pl.BlockSpec((1, d, f), expert3),
                      pl.BlockSpec((1, d, f), expert3),
                      pl.BlockSpec((1, f, d), expert3)],
            out_specs=pl.BlockSpec((tm, d), rows2),
            scratch_shapes=[pltpu.VMEM((tm, d), F32)]),
        out_shape=jax.ShapeDtypeStruct((rows, d), BF16),
        compiler_params=_params(("arbitrary",)),
        name="moe_experts",
    )(plan["texp"], plan["n_valid"], hs, ws, w_gate, w_up, w_down)


def _combine_windows(before, total, off, n_rows):
    nblk = before.shape[0]
    after = jnp.concatenate([before[1:], total[None, :]], axis=0)
    lo = off[None, :] + before
    hi = off[None, :] + after
    first = (lo // ROW_ALIGN) * ROW_ALIGN
    nwin = jnp.where(hi > lo, (hi - first + MOE_WIN - 1) // MOE_WIN, 0)
    ends = jnp.cumsum(nwin, axis=1)
    k = jnp.arange(COMBINE_MAX_WIN, dtype=jnp.int32)
    e_of = jnp.minimum(jnp.sum((ends[:, None, :] <= k[None, :, None]).astype(jnp.int32), axis=2), N_EXPERTS - 1)
    i_of = k[None, :] - jnp.take_along_axis(ends - nwin, e_of, axis=1)
    want = jnp.take_along_axis(first, e_of, axis=1) + i_of * MOE_WIN
    start = jnp.clip(want, 0, n_rows - MOE_WIN)
    limit = jnp.take_along_axis(hi, e_of, axis=1)
    flat = lambda a: a.reshape(-1).astype(jnp.int32)
    return flat(start), flat(want), flat(limit), ends[:, -1].astype(jnp.int32)


def _combine_kernel(start_ref, want_ref, limit_ref, nwin_ref, off_ref, tok_ref, x_ref, mod_ref, fnw_ref,
                    ys_hbm, o_ref, buf_ref, sem_ref, acc_ref):
    b = pl.program_id(0)
    n = nwin_ref[b]
    tok = tok_ref[0]
    pos = [_slot_positions(tok, k, off_ref, False) for k in range(2)]

    def window_copy(k, slot):
        start = pl.multiple_of(start_ref[b * COMBINE_MAX_WIN + k], ROW_ALIGN)
        return pltpu.make_async_copy(ys_hbm.at[pl.ds(start, MOE_WIN), :], buf_ref.at[slot], sem_ref.at[slot])

    @pl.when(n > 0)
    def _():
        window_copy(0, 0).start()

    acc_ref[...] = jnp.zeros_like(acc_ref)

    def window(k, carry):
        slot = k & 1

        @pl.when(k + 1 < n)
        def _():
            window_copy(k + 1, 1 - slot).start()

        window_copy(k, slot).wait()
        w = b * COMBINE_MAX_WIN + k
        local = start_ref[w] + lax.broadcasted_iota(jnp.int32, (1, MOE_WIN), 1)
        rowid = jnp.where(local >= want_ref[w], jnp.where(local < limit_ref[w], local, -1), -1).astype(F32)
        onehot = (jnp.where(pos[0] == rowid, 1.0, 0.0) + jnp.where(pos[1] == rowid, 1.0, 0.0))
        acc_ref[...] += jnp.dot(onehot.astype(BF16), buf_ref[slot], preferred_element_type=F32)
        return carry

    lax.fori_loop(0, n, window, 0)
    o_ref[0] = _rms(x_ref[0] + mod_ref[0][2:3] * acc_ref[...], fnw_ref[...])


def moe_combine(plan, before, total, tok, ys, x, mod, final_nw):
    bsz, s, d = x.shape
    nb = MOE_TILE
    per_b = s // nb
    start, want, limit, nwin = _combine_windows(before, total, plan["off"], ys.shape[0])
    tok_blk = lambda b, *_: (b, 0, 0)
    x_blk = lambda b, *_: (b // per_b, b % per_b, 0)
    return pl.pallas_call(
        _combine_kernel,
        grid_spec=pltpu.PrefetchScalarGridSpec(
            num_scalar_prefetch=5,
            grid=(bsz * per_b,),
            in_specs=[pl.BlockSpec((1, nb, LANES), tok_blk),
                      pl.BlockSpec((1, nb, d), x_blk),
                      pl.BlockSpec((1, 3, d), lambda b, *_: (b // per_b, 0, 0)),
                      pl.BlockSpec((1, d), lambda b, *_: (0, 0)),
                      pl.BlockSpec(memory_space=pl.ANY)],
            out_specs=pl.BlockSpec((1, nb, d), x_blk),
            scratch_shapes=[pltpu.VMEM((2, MOE_WIN, d), BF16), pltpu.SemaphoreType.DMA((2,)),
                            pltpu.VMEM((nb, d), F32)]),
        out_shape=jax.ShapeDtypeStruct(x.shape, F32),
        compiler_params=_params(("arbitrary",)),
        name="moe_combine",
    )(start, want, limit, nwin, plan["off"], tok, x, mod, final_nw.reshape(1, d), ys)


def moe_layer(x, mod, nw, w_router, w_gate, w_up, w_down, final_nw):
    bsz, s, d = x.shape
    nblk = bsz * s // MOE_TILE
    n_tiles = 2 * nblk + N_EXPERTS
    max_pairs = n_tiles + N_EXPERTS * nblk
    hb, tok, tokt, tokw, before, total = moe_router(x, mod, nw, w_router)
    before = before[:, 0, :N_EXPERTS].astype(jnp.int32)
    total = total[0, :N_EXPERTS].astype(jnp.int32)
    plan = _moe_plan(before, total, n_tiles, max_pairs)
    hs, ws = moe_gather(plan, tokt, tokw, hb, n_tiles)
    ys = moe_experts(plan, hs, ws, w_gate, w_up, w_down)
    return moe_combine(plan, before, total, tok, ys, x, mod, final_nw)


def kernel(x, c, e_norm_mix, e_mod_mix_w, e_mod_mix_b, e_w_in, e_fox_fb, e_gla_w2, e_gla_b2, e_gla_norm, e_w_o, e_norm_ffn, e_mod_ffn_w, e_mod_ffn_b, e_ffn_gate, e_ffn_up, e_ffn_down, o_norm_mix, o_mod_mix_w, o_mod_mix_b, o_w_in, o_lam_re, o_lam_im, o_log_dt, o_b_re, o_b_im, o_c_re, o_c_im, o_d_skip, o_w_glu, o_w_o, o_norm_ffn, o_mod_ffn_w, o_mod_ffn_b, o_router, o_exp_gate, o_exp_up, o_exp_down, final_norm):
    bsz, s, d = x.shape

    mod = adaln_mod(c, e_mod_mix_w[0], e_mod_mix_b[0])
    fq, fk, fv, gq, gk, gv, gg, small, cum = layer0_inproj(x, mod, e_norm_mix[0], e_w_in[0], e_fox_fb[0])
    fox = fox_attention(fq, fk, fv, cum)
    gla = gla_attention(gq, gk, gv, gg, small, e_gla_w2[0], e_gla_b2[0], e_gla_norm[0])
    mod_ffn = adaln_mod(c, e_mod_ffn_w[0], e_mod_ffn_b[0])
    x = mixer_out_and_ffn(x, fox, gla, e_w_o[0].astype(BF16), mod, mod_ffn, e_norm_ffn[0],
                          e_ffn_gate[0].astype(BF16), e_ffn_up[0].astype(BF16), e_ffn_down[0].astype(BF16))

    mod = adaln_mod(c, o_mod_mix_w[0], o_mod_mix_b[0])
    s5_tb = 256
    bb_re, bb_im, pow_re, pow_im = s5_discretise(o_lam_re[0], o_lam_im[0], o_log_dt[0], o_b_re[0], o_b_im[0],
                                                 s5_tb // S5_SEG)
    x = s5_layer(x, mod, o_norm_mix[0], o_w_in[0], bb_re, bb_im, pow_re, pow_im, o_c_re[0], o_c_im[0],
                 o_d_skip[0], o_w_glu[0], o_w_o[0], s5_tb)
    mod = adaln_mod(c, o_mod_ffn_w[0], o_mod_ffn_b[0])
    return moe_layer(x, mod, o_norm_ffn[0], o_router[0], o_exp_gate[0].astype(BF16),
                     o_exp_up[0].astype(BF16), o_exp_down[0].astype(BF16), final_norm)
```

```python
import functools

import jax
import jax.numpy as jnp
from jax import lax
from jax.experimental import pallas as pl
from jax.experimental.pallas import tpu as pltpu

F32 = jnp.float32
BF16 = jnp.bfloat16
EPS = 1e-6
NEG_BIG = -1e30
LOG2E = 1.4426950408889634
FOX_BIAS_PARTS = 3
FOX_ROWS = 128
FOX_UNROLL = 4

LANES = 128
VMEM_LIMIT = 56 * 1024 * 1024

HEAD_DIM = 64
FOX_HEADS = 8
FOX_DIM = FOX_HEADS * HEAD_DIM
GLA_HEADS = 4
GLA_DK = 64
GLA_DV = 128
GLA_QK = GLA_HEADS * GLA_DK
GLA_V = GLA_HEADS * GLA_DV
GLA_RANK = 16
GLA_TAU = 16.0
GLA_CHUNK = 64
GLA_SUB = 8
S5_GROUP = 16
S5_STATE = 64
S5_GB = 8
N_EXPERTS = 8
SMALL_FF0 = 0
SMALL_LR0 = 8


def _params(sem):
    return pltpu.CompilerParams(dimension_semantics=sem, vmem_limit_bytes=VMEM_LIMIT)


def _log_sigmoid(x):
    return jnp.minimum(x, 0.0) - jnp.log(1.0 + jnp.exp(-jnp.abs(x)))


def _silu(x):
    return x * jax.nn.sigmoid(x)


def _gelu_tanh(x):
    c = 0.7978845608028654
    return 0.5 * x * (1.0 + jnp.tanh(c * (x + 0.044715 * (x * x * x))))


def _rms(x, w):
    return x * lax.rsqrt(jnp.mean(x * x, axis=-1, keepdims=True) + EPS) * w


def _norm_mod(x, nw, mod):
    return _rms(x, nw) * (1.0 + mod[1:2]) + mod[0:1]


def _split_bf16(x, pieces):
    out = []
    for _ in range(pieces):
        p = x.astype(BF16)
        out.append(p)
        x = x - p.astype(F32)
    return out


def _dot_f32ish(a, b_hi, b_lo):
    a_hi, a_lo = _split_bf16(a, 2)
    return (jnp.dot(a_hi, b_hi, preferred_element_type=F32) + jnp.dot(a_hi, b_lo, preferred_element_type=F32)
            + jnp.dot(a_lo, b_hi, preferred_element_type=F32))


def _dot_exact_left(a_bf16, b):
    return sum(jnp.dot(a_bf16, p, preferred_element_type=F32) for p in _split_bf16(b, 3))


def _nt_dot(a, b):
    return lax.dot_general(a, b, (((1,), (1,)), ((), ())), preferred_element_type=F32)


def _mod_kernel(ct_ref, w_ref, b_ref, o_ref):
    s = _silu(ct_ref[...])
    w = w_ref[...]
    for b in range(s.shape[1]):
        o_ref[b:b + 1, :] = jnp.sum(s[:, b:b + 1] * w, axis=0, keepdims=True) + b_ref[...]


def adaln_mod(c, w, b):
    bsz, d = c.shape
    n = w.shape[1]
    tn = 512
    out = pl.pallas_call(
        _mod_kernel,
        grid=(n // tn,),
        in_specs=[pl.BlockSpec((d, bsz), lambda j: (0, 0)),
                  pl.BlockSpec((d, tn), lambda j: (0, j)),
                  pl.BlockSpec((1, tn), lambda j: (0, j))],
        out_specs=pl.BlockSpec((bsz, tn), lambda j: (0, j)),
        out_shape=jax.ShapeDtypeStruct((bsz, n), F32),
        compiler_params=_params(("parallel",)),
        name="adaln_mod",
    )(c.T, w, b.reshape(1, n))
    return out.reshape(bsz, 3, d)


def _inproj_kernel(x_ref, mod_ref, nw_ref, wbig_ref, wsm_ref, fb_ref, tri_ref,
                   fq_ref, fk_ref, fv_ref, gq_ref, gk_ref, gv_ref, gg_ref, sm_ref, cum_ref,
                   carry_ref):
    i = pl.program_id(1)
    tm = x_ref.shape[1]
    h = _norm_mod(x_ref[0], nw_ref[...], mod_ref[0])
    hb = h.astype(BF16)
    col = 0
    for ref in (fq_ref, fk_ref, fv_ref, gq_ref, gk_ref, gv_ref, gg_ref):
        n = ref.shape[2]
        y = jnp.dot(hb, wbig_ref[:, col:col + n], preferred_element_type=F32)
        if ref is fq_ref:
            y = y * (LOG2E * HEAD_DIM ** -0.5)
        ref[0] = y.astype(ref.dtype)
        col += n
    small = _dot_f32ish(h, wsm_ref[0], wsm_ref[1])
    sm_ref[0] = small
    logf = _log_sigmoid(small + fb_ref[...])

    @pl.when(i == 0)
    def _():
        carry_ref[...] = jnp.zeros_like(carry_ref)

    cum = _dot_exact_left(tri_ref[...], logf) + carry_ref[...]
    cum_ref[0] = cum
    carry_ref[...] = cum[tm - 1:tm, :]


def layer0_inproj(x, mod, nw, w_in, fox_fb, tm=512):
    bsz, s, d = x.shape
    c0 = 3 * FOX_DIM
    c_ff = c0
    c_g = c_ff + FOX_HEADS
    c_lr = c_g + 2 * GLA_QK + 2 * GLA_V
    wbig = jnp.concatenate([w_in[:, :c0], w_in[:, c_g:c_lr]], axis=1).astype(BF16)
    wsm = jnp.zeros((d, LANES), F32)
    wsm = wsm.at[:, SMALL_FF0:SMALL_FF0 + FOX_HEADS].set(w_in[:, c_ff:c_g])
    wsm = wsm.at[:, SMALL_LR0:SMALL_LR0 + GLA_RANK].set(w_in[:, c_lr:])
    fb = jnp.zeros((1, LANES), F32).at[0, SMALL_FF0:SMALL_FF0 + FOX_HEADS].set(fox_fb)
    wsm = jnp.stack(_split_bf16(wsm, 2))
    tri = jnp.tril(jnp.ones((tm, tm), BF16))
    widths = (FOX_DIM, FOX_DIM, FOX_DIM, GLA_QK, GLA_QK, GLA_V, GLA_V)
    row = lambda b, i: (b, i, 0)
    const2 = lambda b, i: (0, 0)
    outs = pl.pallas_call(
        _inproj_kernel,
        grid=(bsz, s // tm),
        in_specs=[pl.BlockSpec((1, tm, d), row),
                  pl.BlockSpec((1, 3, d), lambda b, i: (b, 0, 0)),
                  pl.BlockSpec((1, d), const2),
                  pl.BlockSpec(wbig.shape, const2),
                  pl.BlockSpec(wsm.shape, lambda b, i: (0, 0, 0)),
                  pl.BlockSpec((1, LANES), const2),
                  pl.BlockSpec((tm, tm), const2)],
        out_specs=[pl.BlockSpec((1, tm, n), row) for n in widths]
                  + [pl.BlockSpec((1, tm, LANES), row)] * 2,
        out_shape=[jax.ShapeDtypeStruct((bsz, s, n), BF16) for n in widths]
                  + [jax.ShapeDtypeStruct((bsz, s, LANES), F32)] * 2,
        scratch_shapes=[pltpu.VMEM((1, LANES), F32)],
        compiler_params=_params(("parallel", "arbitrary")),
        name="layer0_inproj",
    )(x, mod, nw.reshape(1, d), wbig, wsm, fb, tri)
    return outs


def _fox_kernel(q_ref, k_ref, v_ref, c_ref, o_ref, kaug_ref, vaug_ref, m_ref, acc_ref, qaug_ref):
    pair = pl.program_id(1)
    i = pl.program_id(2)
    t = q_ref.shape[1]
    lane = lax.broadcasted_iota(jnp.int32, (1, LANES), 1)
    heads = (lane < HEAD_DIM, lane >= HEAD_DIM)
    spare = (HEAD_DIM, 0)

    @pl.when(i == 0)
    def _():
        k = k_ref[0].astype(F32)
        v = v_ref[0].astype(F32)
        cum = c_ref[0]
        for h in range(2):
            bias = jnp.sum(jnp.where(lane == 2 * pair + h, cum, 0.0), axis=1, keepdims=True) * (-LOG2E)
            ka = jnp.where(heads[h], k, 0.0)
            rest = bias
            for n in range(FOX_BIAS_PARTS):
                piece = rest.astype(BF16).astype(F32)
                rest = rest - piece
                ka = jnp.where(lane == spare[h] + n, piece, ka)
            kaug_ref[h] = ka.astype(BF16)
            vaug_ref[h, :, :LANES] = jnp.where(heads[h], v, 0.0).astype(BF16)
            vaug_ref[h, :, LANES:] = jnp.broadcast_to(jnp.where(heads[h], 1.0, 0.0), v.shape).astype(BF16)

    q = q_ref[0].astype(F32)
    for h in range(2):
        ones_here = jnp.where(lane >= spare[h], jnp.where(lane < spare[h] + FOX_BIAS_PARTS, 1.0, 0.0), 0.0)
        qaug_ref[h] = jnp.where(heads[h], q, ones_here).astype(BF16)
    m_ref[...] = jnp.full_like(m_ref, NEG_BIG)
    acc_ref[...] = jnp.zeros_like(acc_ref)

    def tile(j, diagonal):
        keys = pl.ds(pl.multiple_of(j * t, t), t)
        for r0 in range(0, t, FOX_ROWS):
            rows = slice(r0, r0 + FOX_ROWS)
            alphas = []
            upd = None
            for h in range(2):
                s = _nt_dot(qaug_ref[h, rows, :], kaug_ref[h, keys, :])
                if diagonal:
                    r = r0 + lax.broadcasted_iota(jnp.int32, (FOX_ROWS, t), 0)
                    c = lax.broadcasted_iota(jnp.int32, (FOX_ROWS, t), 1)
                    s = jnp.where(c <= r, s, NEG_BIG)
                m_prev = m_ref[h, rows, :]
                m_next = jnp.maximum(m_prev, jnp.max(s, axis=1, keepdims=True))
                m_ref[h, rows, :] = m_next
                p = jnp.exp2(s - jnp.concatenate([m_next] * (t // LANES), axis=1)).astype(BF16)
                alphas.append(jnp.exp2(m_prev - m_next))
                d = jnp.dot(p, vaug_ref[h, keys, :], preferred_element_type=F32)
                upd = d if upd is None else upd + d
            alpha = jnp.where(heads[0], alphas[0], alphas[1])
            acc_ref[rows, :] = jnp.concatenate([alpha, alpha], axis=1) * acc_ref[rows, :] + upd

    done = 0
    width = FOX_UNROLL
    while width >= 1:
        def group(jj, carry, width=width, done=done):
            for u in range(width):
                tile(done + width * jj + u, False)
            return carry

        trips = (i - done) // width
        lax.fori_loop(0, trips, group, 0)
        done = done + trips * width
        width //= 2
    tile(i, True)
    o_ref[0] = (acc_ref[:, :LANES] / acc_ref[:, LANES:]).astype(o_ref.dtype)


def fox_attention(fq, fk, fv, cum, t=512):
    bsz, s, _ = fq.shape
    n_pair = FOX_HEADS // 2
    seq = lambda b, p, i: (b, 0, p)
    return pl.pallas_call(
        _fox_kernel,
        grid=(bsz, n_pair, s // t),
        in_specs=[pl.BlockSpec((1, t, LANES), lambda b, p, i: (b, i, p)),
                  pl.BlockSpec((1, s, LANES), seq),
                  pl.BlockSpec((1, s, LANES), seq),
                  pl.BlockSpec((1, s, LANES), lambda b, p, i: (b, 0, 0))],
        out_specs=pl.BlockSpec((1, t, LANES), lambda b, p, i: (b, i, p)),
        out_shape=jax.ShapeDtypeStruct((bsz, s, FOX_DIM), BF16),
        scratch_shapes=[pltpu.VMEM((2, s, LANES), BF16), pltpu.VMEM((2, s, 2 * LANES), BF16),
                        pltpu.VMEM((2, t, LANES), F32), pltpu.VMEM((t, 2 * LANES), F32),
                        pltpu.VMEM((2, t, LANES), BF16)],
        compiler_params=_params(("parallel", "parallel", "arbitrary")),
        name="fox_attention",
    )(fq, fk, fv, cum)


def _gla_pair_chunk(q, k, v, la, st, tri, eye, k_scr, b_scr):
    C, SUB = GLA_CHUNK, GLA_SUB
    nsub = C // SUB
    lane = lax.broadcasted_iota(jnp.int32, (1, LANES), 1)
    head_a = lane < GLA_DK
    b = _dot_exact_left(tri, la)
    b_last = b[C - 1:C, :]
    k_scr[...] = k
    b_scr[...] = b
    rowblk = lax.broadcasted_iota(jnp.int32, (C, 1), 0) // SUB

    refs = [b[m * SUB - 1:m * SUB, :] for m in range(1, nsub)]
    rsel = refs[-1]
    for m in range(nsub - 2, 0, -1):
        rsel = jnp.where(rowblk == m, refs[m - 1], rsel)
    qt = q * jnp.exp(jnp.minimum(b - rsel, 0.0))
    zero = jnp.zeros_like(q)
    qs = jnp.concatenate([jnp.where(rowblk == m, qt, zero) for m in range(1, nsub)], axis=1)
    ks = jnp.concatenate(
        [jnp.where(rowblk < m, k * jnp.exp(jnp.minimum(refs[m - 1] - b, 0.0)), zero)
         for m in range(1, nsub)], axis=1)
    lane3 = lax.broadcasted_iota(jnp.int32, (1, (nsub - 1) * LANES), 1)
    head_a3 = (lane3 & (LANES - 1)) < GLA_DK
    zero3 = jnp.zeros_like(ks)
    kstack = jnp.concatenate([jnp.where(head_a3, ks, zero3), jnp.where(head_a3, zero3, ks)], axis=0)
    a_off = _nt_dot(qs.astype(BF16), kstack.astype(BF16))

    row16 = lax.broadcasted_iota(jnp.int32, (SUB, 1), 0)
    blocks = []
    for i in range(nsub):
        qb = q[i * SUB:(i + 1) * SUB, :]
        bb = b[i * SUB:(i + 1) * SUB, :]
        d = jnp.zeros((SUB, LANES), F32)
        for sp in range(SUB):
            srow = i * SUB + sp
            e = qb * k_scr[srow:srow + 1, :] * jnp.exp(jnp.minimum(bb - b_scr[srow:srow + 1, :], 0.0))
            da = jnp.sum(jnp.where(head_a, e, 0.0), axis=1, keepdims=True)
            db = jnp.sum(jnp.where(head_a, 0.0, e), axis=1, keepdims=True)
            live = row16 >= sp
            da = jnp.where(live, da, 0.0)
            db = jnp.where(live, db, 0.0)
            d = jnp.where(lane == srow, da, jnp.where(lane == GLA_DK + srow, db, d))
        blocks.append(d)
    a_pair = a_off + jnp.concatenate(blocks, axis=0)

    lane_v = lax.broadcasted_iota(jnp.int32, (1, 2 * GLA_DV), 1)
    first_v = lane_v < GLA_DV
    zv = jnp.zeros_like(v)
    vbd = jnp.concatenate([jnp.where(first_v, v, zv), jnp.where(first_v, zv, v)], axis=0)
    qh = q * jnp.exp(b)
    o = (jnp.dot(a_pair.astype(BF16), vbd, preferred_element_type=F32)
         + _nt_dot(qh.astype(BF16), st.astype(BF16)))

    k_end = k * jnp.exp(b_last - b)
    v_t = _nt_dot(eye, v).astype(BF16)
    inc = jnp.dot(v_t, k_end.astype(BF16), preferred_element_type=F32)
    row_v = lax.broadcasted_iota(jnp.int32, (2 * GLA_DV, 1), 0)
    same_head = (row_v // GLA_DV) == (lane // GLA_DK)
    st_new = st * jnp.exp(b_last) + jnp.where(same_head, inc, 0.0)
    return o, st_new


def _gla_kernel(gq_ref, gk_ref, gv_ref, gg_ref, sm_ref, w2_ref, b2_ref, gn_ref, tri_ref, eye_ref,
                o_ref, st_ref, la_ref, k_scr, b_scr):
    i = pl.program_id(1)
    tm = gq_ref.shape[1]
    n_pair = GLA_HEADS // 2
    scale = GLA_DK ** -0.5

    @pl.when(i == 0)
    def _():
        st_ref[...] = jnp.zeros_like(st_ref)

    z = _dot_f32ish(sm_ref[0], w2_ref[0], w2_ref[1]) + b2_ref[...]
    la_ref[...] = _log_sigmoid(z) * (1.0 / GLA_TAU)
    gn = gn_ref[...]

    def chunk(c, carry):
        base = pl.multiple_of(c * GLA_CHUNK, GLA_CHUNK)
        rows = pl.ds(base, GLA_CHUNK)
        for hp in range(n_pair):
            ql = slice(hp * LANES, (hp + 1) * LANES)
            vl = slice(hp * 2 * GLA_DV, (hp + 1) * 2 * GLA_DV)
            q = gq_ref[0, rows, ql].astype(F32) * scale
            k = gk_ref[0, rows, ql].astype(F32)
            v = gv_ref[0, rows, vl]
            o, st_new = _gla_pair_chunk(q, k, v, la_ref[rows, ql], st_ref[hp], tri_ref[...],
                                        eye_ref[...], k_scr.at[hp], b_scr.at[hp])
            st_ref[hp] = st_new
            gate = _silu(gg_ref[0, rows, vl].astype(F32))
            halves = [_rms(o[:, h * GLA_DV:(h + 1) * GLA_DV], gn) for h in range(2)]
            o_ref[0, rows, vl] = (jnp.concatenate(halves, axis=1) * gate).astype(o_ref.dtype)
        return carry

    lax.fori_loop(0, tm // GLA_CHUNK, chunk, 0, unroll=4)


def gla_attention(gq, gk, gv, gg, small, w2, b2, gnorm, tm=512):
    bsz, s, _ = gq.shape
    w2p = jnp.zeros((LANES, GLA_QK), F32).at[SMALL_LR0:SMALL_LR0 + GLA_RANK].set(w2)
    w2p = jnp.stack(_split_bf16(w2p, 2))
    tri = jnp.tril(jnp.ones((GLA_CHUNK, GLA_CHUNK), BF16))
    eye = jnp.eye(2 * GLA_DV, dtype=BF16)
    row = lambda b, i: (b, i, 0)
    const2 = lambda b, i: (0, 0)
    return pl.pallas_call(
        _gla_kernel,
        grid=(bsz, s // tm),
        in_specs=[pl.BlockSpec((1, tm, GLA_QK), row), pl.BlockSpec((1, tm, GLA_QK), row),
                  pl.BlockSpec((1, tm, GLA_V), row), pl.BlockSpec((1, tm, GLA_V), row),
                  pl.BlockSpec((1, tm, LANES), row),
                  pl.BlockSpec(w2p.shape, lambda b, i: (0, 0, 0)), pl.BlockSpec((1, GLA_QK), const2),
                  pl.BlockSpec((1, GLA_DV), const2), pl.BlockSpec(tri.shape, const2),
                  pl.BlockSpec(eye.shape, const2)],
        out_specs=pl.BlockSpec((1, tm, GLA_V), row),
        out_shape=jax.ShapeDtypeStruct((bsz, s, GLA_V), BF16),
        scratch_shapes=[pltpu.VMEM((GLA_HEADS // 2, 2 * GLA_DV, LANES), F32),
                        pltpu.VMEM((tm, GLA_QK), F32),
                        pltpu.VMEM((GLA_HEADS // 2, GLA_CHUNK, LANES), F32),
                        pltpu.VMEM((GLA_HEADS // 2, GLA_CHUNK, LANES), F32)],
        compiler_params=_params(("parallel", "arbitrary")),
        name="gla_attention",
    )(gq, gk, gv, gg, small, w2p, b2.reshape(1, GLA_QK), gnorm.reshape(1, GLA_DV), tri, eye)


def _mix_ffn_kernel(x_ref, a_ref, b_ref, wo_ref, modm_ref, modf_ref, nw_ref, wg_ref, wu_ref, wd_ref,
                    o_ref, acc_ref, *, tf):
    na = a_ref.shape[2]
    y = (jnp.dot(a_ref[0], wo_ref[:na, :], preferred_element_type=F32)
         + jnp.dot(b_ref[0], wo_ref[na:, :], preferred_element_type=F32))
    x1 = x_ref[0] + modm_ref[0][2:3] * y
    hb = _norm_mod(x1, nw_ref[...], modf_ref[0]).astype(BF16)
    for j in range(wg_ref.shape[1] // tf):
        cols = slice(j * tf, (j + 1) * tf)
        act = (_silu(jnp.dot(hb, wg_ref[:, cols], preferred_element_type=F32))
               * jnp.dot(hb, wu_ref[:, cols], preferred_element_type=F32))
        part = jnp.dot(act.astype(BF16), wd_ref[cols, :], preferred_element_type=F32)
        if j == 0:
            acc_ref[...] = part
        else:
            acc_ref[...] += part
    o_ref[0] = x1 + modf_ref[0][2:3] * acc_ref[...]


def mixer_out_and_ffn(x, a, b, w_o, mod_mix, mod_ffn, nw, w_gate, w_up, w_down, tm=512, tf=256):
    bsz, s, d = x.shape
    row = lambda bb, i: (bb, i, 0)
    per_b = lambda bb, i: (bb, 0, 0)
    const2 = lambda bb, i: (0, 0)
    resident = lambda arr: pl.BlockSpec(arr.shape, const2, pipeline_mode=pl.Buffered(1))
    return pl.pallas_call(
        functools.partial(_mix_ffn_kernel, tf=tf),
        grid=(bsz, s // tm),
        in_specs=[pl.BlockSpec((1, tm, d), row),
                  pl.BlockSpec((1, tm, a.shape[2]), row),
                  pl.BlockSpec((1, tm, b.shape[2]), row),
                  resident(w_o),
                  pl.BlockSpec((1, 3, d), per_b),
                  pl.BlockSpec((1, 3, d), per_b),
                  pl.BlockSpec((1, d), const2),
                  resident(w_gate), resident(w_up), resident(w_down)],
        out_specs=pl.BlockSpec((1, tm, d), row),
        out_shape=jax.ShapeDtypeStruct(x.shape, F32),
        scratch_shapes=[pltpu.VMEM((tm, d), F32)],
        compiler_params=_params(("parallel", "parallel")),
        name="mixer_out_and_ffn",
    )(x, a, b, w_o, mod_mix, mod_ffn, nw.reshape(1, d), w_gate, w_up, w_down)


S5_SEG = 8


def _s5_disc_kernel(lre_ref, lim_ref, ldt_ref, bre_ref, bim_ref,
                    bbre_ref, bbim_ref, pre_ref, pim_ref):
    lre = lre_ref[...]
    lim = lim_ref[...]
    dt = jnp.exp(ldt_ref[...])
    mag = jnp.exp(lre * dt)
    a_re = mag * jnp.cos(lim * dt)
    a_im = mag * jnp.sin(lim * dt)
    den = lre * lre + lim * lim
    nr = a_re - 1.0
    ni = a_im
    f_re = (nr * lre + ni * lim) / den
    f_im = (ni * lre - nr * lim) / den
    bre = bre_ref[...]
    bim = bim_ref[...]
    bbre_ref[...] = f_re[:, None, :] * bre - f_im[:, None, :] * bim
    bbim_ref[...] = f_re[:, None, :] * bim + f_im[:, None, :] * bre
    n = (lax.broadcasted_iota(jnp.int32, pre_ref.shape, 0) + 1).astype(F32)
    mag_n = jnp.exp(n * (lre * dt)[None])
    ang_n = n * (lim * dt)[None]
    pre_ref[...] = mag_n * jnp.cos(ang_n)
    pim_ref[...] = mag_n * jnp.sin(ang_n)


def s5_discretise(lam_re, lam_im, log_dt, b_re, b_im, seg_len):
    g, p = lam_re.shape
    k = b_re.shape[2]
    return pl.pallas_call(
        _s5_disc_kernel,
        out_shape=[jax.ShapeDtypeStruct((g, k, p), F32)] * 2 + [jax.ShapeDtypeStruct((seg_len, g, p), F32)] * 2,
        name="s5_discretise",
    )(lam_re, lam_im, log_dt.reshape(g, 1), b_re.transpose(0, 2, 1), b_im.transpose(0, 2, 1))


def _s5_kernel(x_ref, mod_ref, nw_ref, perm_ref, permt_ref, win_ref, bbd_ref, cbd_ref, apow_ref, dsk_ref,
               wglu_ref, wo_ref, o_ref, carry_ref, bu_ref, xs_ref, xsb_ref):
    i = pl.program_id(1)
    tb = x_ref.shape[1]
    nblk = bu_ref.shape[0]
    nch = bu_ref.shape[2] // (2 * LANES)
    seg_len = tb // S5_SEG
    re_l = [slice(ch * LANES, (ch + 1) * LANES) for ch in range(nch)]
    im_l = [slice((nch + ch) * LANES, (nch + ch + 1) * LANES) for ch in range(nch)]

    @pl.when(i == 0)
    def _():
        carry_ref[...] = jnp.zeros_like(carry_ref)

    x = x_ref[0]
    h = _norm_mod(x, nw_ref[...], mod_ref[0]).astype(BF16)
    hp = jnp.dot(perm_ref[...], h, preferred_element_type=F32).astype(BF16)
    u = jnp.dot(hp, win_ref[...], preferred_element_type=F32)
    ub = u.astype(BF16)
    for jb in range(nblk):
        bu_ref[jb] = jnp.dot(ub[:, jb * LANES:(jb + 1) * LANES], bbd_ref[jb], preferred_element_type=F32)

    seg_row = lax.broadcasted_iota(jnp.int32, (S5_SEG, 1), 0)

    def channel_block(jb, blk_carry):
        def bcast(n, lanes):
            return jnp.broadcast_to(apow_ref[jb, n:n + 1, lanes], (S5_SEG, LANES))

        a1 = [(bcast(0, re_l[ch]), bcast(0, im_l[ch])) for ch in range(nch)]

        def step(t, state):
            rows = pl.ds(pl.multiple_of(t * S5_SEG, S5_SEG), S5_SEG)
            out = []
            for ch in range(nch):
                xr, xi = state[2 * ch], state[2 * ch + 1]
                ar, ai = a1[ch]
                nr = ar * xr - ai * xi + bu_ref[jb, rows, re_l[ch]]
                ni = ar * xi + ai * xr + bu_ref[jb, rows, im_l[ch]]
                xs_ref[jb, rows, re_l[ch]] = nr
                xs_ref[jb, rows, im_l[ch]] = ni
                out += [nr, ni]
            return tuple(out)

        zero = jnp.zeros((S5_SEG, LANES), F32)
        ends = lax.fori_loop(0, seg_len, step, (zero,) * (2 * nch), unroll=4)

        starts = []
        for ch in range(nch):
            alr = apow_ref[jb, seg_len - 1:seg_len, re_l[ch]]
            ali = apow_ref[jb, seg_len - 1:seg_len, im_l[ch]]
            er, ei = ends[2 * ch], ends[2 * ch + 1]
            zr = carry_ref[jb, :, re_l[ch]]
            zi = carry_ref[jb, :, im_l[ch]]
            tr = jnp.zeros((S5_SEG, LANES), F32)
            ti = jnp.zeros((S5_SEG, LANES), F32)
            for s in range(S5_SEG):
                tr = jnp.where(seg_row == s, zr, tr)
                ti = jnp.where(seg_row == s, zi, ti)
                zr, zi = (alr * zr - ali * zi + er[s:s + 1], alr * zi + ali * zr + ei[s:s + 1])
            carry_ref[jb, :, re_l[ch]] = zr
            carry_ref[jb, :, im_l[ch]] = zi
            starts.append((jnp.concatenate([tr, tr], axis=0), jnp.concatenate([ti, ti], axis=0)))

        for t2 in range(seg_len // 2):
            rows = slice(t2 * 2 * S5_SEG, (t2 + 1) * 2 * S5_SEG)
            for ch in range(nch):
                pr = jnp.concatenate([bcast(2 * t2, re_l[ch]), bcast(2 * t2 + 1, re_l[ch])], axis=0)
                pi = jnp.concatenate([bcast(2 * t2, im_l[ch]), bcast(2 * t2 + 1, im_l[ch])], axis=0)
                zr, zi = starts[ch]
                xsb_ref[jb, rows, re_l[ch]] = (xs_ref[jb, rows, re_l[ch]] + pr * zr - pi * zi).astype(BF16)
                xsb_ref[jb, rows, im_l[ch]] = (xs_ref[jb, rows, im_l[ch]] + pr * zi + pi * zr).astype(BF16)
        return blk_carry

    lax.fori_loop(0, nblk, channel_block, 0)

    ys = [jnp.dot(xsb_ref[jb], cbd_ref[jb], preferred_element_type=F32) for jb in range(nblk)]
    y = jnp.concatenate(ys, axis=1) + dsk_ref[...] * u
    y = _gelu_tanh(y)
    y = y * jax.nn.sigmoid(jnp.dot(y.astype(BF16), wglu_ref[...], preferred_element_type=F32))
    z = jnp.dot(y.astype(BF16), wo_ref[...], preferred_element_type=F32)
    z_hi = z.astype(BF16)
    z_lo = (z - z_hi.astype(F32)).astype(BF16)
    z = (jnp.dot(permt_ref[...], z_hi, preferred_element_type=F32)
         + jnp.dot(permt_ref[...], z_lo, preferred_element_type=F32))
    o_ref[0] = x + mod_ref[0][2:3] * z


def s5_layer(x, mod, nw, w_in, bb_re, bb_im, pow_re, pow_im, c_re, c_im, d_skip, w_glu, w_o, tb):
    bsz, s, d = x.shape
    g, k, p = bb_re.shape
    nblk = g // S5_GB
    seg_len = pow_re.shape[0]
    assert tb == seg_len * S5_SEG
    eye = jnp.eye(S5_GB, dtype=F32)

    def blockdiag_in(bb):
        t = jnp.einsum('jgkp,gh->jgkhp', bb.reshape(nblk, S5_GB, k, p), eye)
        return t.reshape(nblk, S5_GB * k, S5_GB * p)

    def blockdiag_out(c):
        t = jnp.einsum('jgkp,gh->jgphk', c.reshape(nblk, S5_GB, k, p), eye)
        return t.reshape(nblk, S5_GB * p, S5_GB * k)

    bbd = jnp.concatenate([blockdiag_in(bb_re), blockdiag_in(bb_im)], axis=2).astype(BF16)
    cbd = jnp.concatenate([blockdiag_out(c_re), -blockdiag_out(c_im)], axis=1).astype(BF16)
    half = S5_GB * p
    apow = jnp.concatenate([pow_re.reshape(seg_len, nblk, half), pow_im.reshape(seg_len, nblk, half)],
                           axis=2).transpose(1, 0, 2)
    new_row = jnp.arange(tb)
    old_row = (new_row % S5_SEG) * seg_len + new_row // S5_SEG
    perm = (old_row[:, None] == jnp.arange(tb)[None, :]).astype(BF16)
    row = lambda b, i: (b, i, 0)
    const2 = lambda b, i: (0, 0)
    const3 = lambda b, i: (0, 0, 0)
    return pl.pallas_call(
        _s5_kernel,
        grid=(bsz, s // tb),
        in_specs=[pl.BlockSpec((1, tb, d), row),
                  pl.BlockSpec((1, 3, d), lambda b, i: (b, 0, 0)),
                  pl.BlockSpec((1, d), const2),
                  pl.BlockSpec((tb, tb), const2), pl.BlockSpec((tb, tb), const2),
                  pl.BlockSpec((d, d), const2),
                  pl.BlockSpec(bbd.shape, const3),
                  pl.BlockSpec(cbd.shape, const3),
                  pl.BlockSpec(apow.shape, const3),
                  pl.BlockSpec((1, d), const2),
                  pl.BlockSpec((d, d), const2), pl.BlockSpec((d, d), const2)],
        out_specs=pl.BlockSpec((1, tb, d), row),
        out_shape=jax.ShapeDtypeStruct(x.shape, F32),
        scratch_shapes=[pltpu.VMEM((nblk, 1, 2 * half), F32),
                        pltpu.VMEM((nblk, tb, 2 * half), F32),
                        pltpu.VMEM((nblk, tb, 2 * half), F32),
                        pltpu.VMEM((nblk, tb, 2 * half), BF16)],
        compiler_params=_params(("parallel", "arbitrary")),
        name="s5_layer",
    )(x, mod, nw.reshape(1, d), perm, perm.T, w_in.astype(BF16), bbd, cbd, apow, d_skip.reshape(1, d),
      w_glu.astype(BF16), w_o.astype(BF16))


TOK_E, TOK_RANK, TOK_W = 0, 2, 4
W_PIECES = 3
MOE_TILE = 512
MOE_WIN = 192
ROW_ALIGN = 16
COMBINE_MAX_WIN = (2 * MOE_TILE + N_EXPERTS * (ROW_ALIGN - 1)) // MOE_WIN + N_EXPERTS + 1


def _router_kernel(x_ref, mod_ref, nw_ref, wr_ref, tri_ref, sel_ref,
                   hb_ref, tok_ref, tokt_ref, tokw_ref, before_ref, total_ref, carry_ref):
    blk = pl.program_id(0)
    h = _norm_mod(x_ref[0], nw_ref[...], mod_ref[0])
    hb_ref[0] = h.astype(BF16)
    logits = _dot_f32ish(h, wr_ref[0], wr_ref[1])
    lane = lax.broadcasted_iota(jnp.int32, logits.shape, 1).astype(F32)
    logits = jnp.where(lane < N_EXPERTS, logits, -jnp.inf)
    m1 = jnp.max(logits, axis=1, keepdims=True)
    i1 = jnp.min(jnp.where(logits == m1, lane, float(LANES)), axis=1, keepdims=True)
    rest = jnp.where(lane == i1, -jnp.inf, logits)
    m2 = jnp.max(rest, axis=1, keepdims=True)
    i2 = jnp.min(jnp.where(rest == m2, lane, float(LANES)), axis=1, keepdims=True)
    e2 = jnp.exp(m2 - m1)
    w1 = 1.0 / (1.0 + e2)
    w2 = e2 / (1.0 + e2)

    @pl.when(blk == 0)
    def _():
        carry_ref[...] = jnp.zeros_like(carry_ref)

    routed = jnp.where(lane == i1, 1.0, jnp.where(lane == i2, 1.0, 0.0))
    before = carry_ref[...]
    rank = jnp.dot(tri_ref[...], routed.astype(BF16), preferred_element_type=F32) + before
    r1 = jnp.sum(jnp.where(lane == i1, rank, 0.0), axis=1, keepdims=True)
    r2 = jnp.sum(jnp.where(lane == i2, rank, 0.0), axis=1, keepdims=True)
    fields = ((TOK_E, i1), (TOK_E + 1, i2), (TOK_RANK, r1), (TOK_RANK + 1, r2), (TOK_W, w1), (TOK_W + 1, w2))
    tok = jnp.zeros_like(logits)
    for ln, val in fields:
        tok = jnp.where(lane == ln, val, tok)
    tok_ref[0] = tok
    tokt_ref[0] = sum(_nt_dot(sel_ref[...], p) for p in _split_bf16(tok, 3))
    pieces = jnp.zeros_like(logits)
    for k, w in enumerate((w1, w2)):
        rest = w
        for n in range(W_PIECES):
            piece = rest.astype(BF16).astype(F32)
            rest = rest - piece
            pieces = jnp.where(lane == k * W_PIECES + n, piece, pieces)
    tokw_ref[0] = pieces.astype(BF16)
    before_ref[0] = before
    total = before + jnp.sum(routed, axis=0, keepdims=True)
    carry_ref[...] = total
    total_ref[...] = total


def moe_router(x, mod, nw, w_router):
    bsz, s, d = x.shape
    nb = MOE_TILE
    per_b = s // nb
    nblk = bsz * per_b
    wr = jnp.stack(_split_bf16(jnp.zeros((d, LANES), F32).at[:, :N_EXPERTS].set(w_router), 2))
    tri = jnp.tril(jnp.ones((nb, nb), BF16), k=-1)
    sel = jnp.eye(8, LANES, dtype=BF16)
    const2 = lambda i: (0, 0)
    return pl.pallas_call(
        _router_kernel,
        grid=(nblk,),
        in_specs=[pl.BlockSpec((1, nb, d), lambda i: (i // per_b, i % per_b, 0)),
                  pl.BlockSpec((1, 3, d), lambda i: (i // per_b, 0, 0)),
                  pl.BlockSpec((1, d), const2),
                  pl.BlockSpec((2, d, LANES), lambda i: (0, 0, 0)),
                  pl.BlockSpec((nb, nb), const2),
                  pl.BlockSpec((8, LANES), const2)],
        out_specs=[pl.BlockSpec((1, nb, d), lambda i: (i, 0, 0)),
                   pl.BlockSpec((1, nb, LANES), lambda i: (i, 0, 0)),
                   pl.BlockSpec((1, 8, nb), lambda i: (i, 0, 0)),
                   pl.BlockSpec((1, nb, LANES), lambda i: (i, 0, 0)),
                   pl.BlockSpec((1, 1, LANES), lambda i: (i, 0, 0)),
                   pl.BlockSpec((1, LANES), const2)],
        out_shape=[jax.ShapeDtypeStruct((nblk, nb, d), BF16),
                   jax.ShapeDtypeStruct((nblk, nb, LANES), F32),
                   jax.ShapeDtypeStruct((nblk, 8, nb), F32),
                   jax.ShapeDtypeStruct((nblk, nb, LANES), BF16),
                   jax.ShapeDtypeStruct((nblk, 1, LANES), F32),
                   jax.ShapeDtypeStruct((1, LANES), F32)],
        scratch_shapes=[pltpu.VMEM((1, LANES), F32)],
        compiler_params=_params(("arbitrary",)),
        name="moe_router",
    )(x, mod, nw.reshape(1, d), wr, tri, sel)


def _moe_plan(before, total, n_tiles):
    tm = MOE_TILE
    ntile = (total + tm - 1) // tm
    tile_end = jnp.cumsum(ntile)
    tile_start = tile_end - ntile
    n_valid = tile_end[-1]
    r = jnp.arange(n_tiles, dtype=jnp.int32)
    texp = jnp.minimum(jnp.sum((tile_end[None, :] <= r[:, None]).astype(jnp.int32), axis=1), N_EXPERTS - 1)
    r0 = (r - tile_start[texp]) * tm
    after = jnp.concatenate([before[1:], total[None, :]], axis=0)
    lo = before[:, texp].T
    hi = after[:, texp].T
    meets = (lo < (r0 + tm)[:, None]) & (hi > r0[:, None]) & (r < n_valid)[:, None]
    count = jnp.sum(meets, axis=1).astype(jnp.int32)
    first_blk = jnp.sum(jnp.cumsum(meets, axis=1) == 0, axis=1).astype(jnp.int32)
    start = jnp.clip(lo - r0[:, None], 0, tm)
    end = jnp.clip(hi - r0[:, None], 0, tm)
    first = (start // ROW_ALIGN) * ROW_ALIGN
    nwin = jnp.where(meets, (end - first + MOE_WIN - 1) // MOE_WIN, 0)
    flat = lambda a: a.reshape(-1).astype(jnp.int32)
    return dict(texp=texp.astype(jnp.int32), n_valid=n_valid.astype(jnp.int32).reshape(1),
                off=(tile_start * tm).astype(jnp.int32),
                g_blk0=jnp.where(count > 0, first_blk, 0), g_count=count,
                g_seq=(jnp.cumsum(count) - count).astype(jnp.int32),
                g_first=flat(first), g_nwin=flat(nwin))


def _slot_positions(tok_fields, k, off_ref, along_lanes):
    if along_lanes:
        e = tok_fields[TOK_E + k:TOK_E + k + 1, :]
        rank = tok_fields[TOK_RANK + k:TOK_RANK + k + 1, :]
    else:
        e = tok_fields[:, TOK_E + k:TOK_E + k + 1]
        rank = tok_fields[:, TOK_RANK + k:TOK_RANK + k + 1]
    pos = rank
    for ex in range(N_EXPERTS):
        pos = pos + jnp.where(e == float(ex), off_ref[ex].astype(F32), 0.0)
    return pos


def _window_rows(want, tile, tm, shape, axis):
    base = pl.multiple_of(jnp.minimum(want, tm - MOE_WIN), ROW_ALIGN)
    local = base + lax.broadcasted_iota(jnp.int32, shape, axis)
    rowid = jnp.where(local >= want, tile * tm + local, -1).astype(F32)
    return base, rowid


def _gather_kernel(blk0_ref, count_ref, seq_ref, first_ref, nwin_ref, off_ref, tokt_hbm, tokw_hbm, hb_hbm,
                   hs_ref, ws_ref, tokt_buf, tokw_buf, hb_buf, sem_ref, acc_ref, wacc_ref):
    r = pl.program_id(0)
    tm = hs_ref.shape[0]
    nblk = hb_hbm.shape[0]
    b0 = blk0_ref[r]
    n = count_ref[r]
    nxt = jnp.minimum(r + 1, pl.num_programs(0) - 1)
    next_tile_has_work = jnp.logical_and(r + 1 < pl.num_programs(0), count_ref[nxt] > 0)
    lane = lax.broadcasted_iota(jnp.int32, (1, LANES), 1)

    def block_copies(blk, slot):
        return [pltpu.make_async_copy(src.at[blk], dst.at[slot], sem_ref.at[which, slot])
                for which, (src, dst) in enumerate(((tokt_hbm, tokt_buf), (tokw_hbm, tokw_buf), (hb_hbm, hb_buf)))]

    @pl.when(jnp.logical_and(r == 0, n > 0))
    def _():
        for c in block_copies(b0, 0):
            c.start()

    acc_ref[...] = jnp.zeros_like(acc_ref)
    wacc_ref[...] = jnp.zeros_like(wacc_ref)

    def pair(k, carry):
        blk = b0 + k
        slot = (seq_ref[r] + k) & 1
        more_here = k + 1 < n

        @pl.when(jnp.logical_or(more_here, next_tile_has_work))
        def _():
            for c in block_copies(jnp.where(more_here, blk + 1, blk0_ref[nxt]), 1 - slot):
                c.start()

        for c in block_copies(blk, slot):
            c.wait()
        tokt = tokt_buf[slot]
        pos = [_slot_positions(tokt, s, off_ref, True) for s in range(2)]
        p = r * nblk + blk

        def window(i, wcarry):
            base, rowid = _window_rows(first_ref[p] + i * MOE_WIN, r, tm, (MOE_WIN, 1), 0)
            onehots = [jnp.where(pos[s] == rowid, 1.0, 0.0).astype(BF16) for s in range(2)]
            rows = pl.ds(base, MOE_WIN)
            acc_ref[rows, :] += jnp.dot(onehots[0] + onehots[1], hb_buf[slot], preferred_element_type=F32)
            picked = [jnp.dot(onehots[s], tokw_buf[slot], preferred_element_type=F32) for s in range(2)]
            wacc_ref[rows, :] += jnp.where(lane < W_PIECES, picked[0],
                                           jnp.where(lane < 2 * W_PIECES, picked[1], 0.0))
            return wcarry

        lax.fori_loop(0, nwin_ref[p], window, 0)
        return carry

    lax.fori_loop(0, n, pair, 0)
    hs_ref[...] = acc_ref[...].astype(hs_ref.dtype)
    ws_ref[...] = wacc_ref[...]


def moe_gather(plan, tokt, tokw, hb, n_tiles):
    nblk, nb, d = hb.shape
    tm = MOE_TILE
    by_tile2 = lambda r, *_: (r, 0)
    hbm = pl.BlockSpec(memory_space=pl.ANY)
    return pl.pallas_call(
        _gather_kernel,
        grid_spec=pltpu.PrefetchScalarGridSpec(
            num_scalar_prefetch=6,
            grid=(n_tiles,),
            in_specs=[hbm, hbm, hbm],
            out_specs=[pl.BlockSpec((tm, d), by_tile2), pl.BlockSpec((tm, LANES), by_tile2)],
            scratch_shapes=[pltpu.VMEM((2, 8, nb), F32), pltpu.VMEM((2, nb, LANES), BF16),
                            pltpu.VMEM((2, nb, d), BF16), pltpu.SemaphoreType.DMA((3, 2)),
                            pltpu.VMEM((tm, d), F32), pltpu.VMEM((tm, LANES), F32)]),
        out_shape=[jax.ShapeDtypeStruct((n_tiles * tm, d), BF16),
                   jax.ShapeDtypeStruct((n_tiles * tm, LANES), F32)],
        compiler_params=_params(("arbitrary",)),
        name="moe_gather",
    )(plan["g_blk0"], plan["g_count"], plan["g_seq"], plan["g_first"], plan["g_nwin"], plan["off"],
      tokt, tokw, hb)


def _experts_kernel(texp_ref, nvalid_ref, hs_ref, ws_ref, wg_ref, wu_ref, wd_ref, ys_ref, acc_ref, *, tf):
    r = pl.program_id(0)
    valid = r < nvalid_ref[0]

    @pl.when(valid)
    def _():
        hb = hs_ref[...]
        for j in range(wg_ref.shape[2] // tf):
            cols = slice(j * tf, (j + 1) * tf)
            act = (_silu(jnp.dot(hb, wg_ref[0, :, cols], preferred_element_type=F32))
                   * jnp.dot(hb, wu_ref[0, :, cols], preferred_element_type=F32))
            part = jnp.dot(act.astype(BF16), wd_ref[0, cols, :], preferred_element_type=F32)
            if j == 0:
                acc_ref[...] = part
            else:
                acc_ref[...] += part
        w = jnp.sum(ws_ref[...], axis=1, keepdims=True)
        ys_ref[...] = (acc_ref[...] * w).astype(ys_ref.dtype)

    @pl.when(jnp.logical_not(valid))
    def _():
        ys_ref[...] = jnp.zeros_like(ys_ref)


def moe_experts(plan, hs, ws, w_gate, w_up, w_down, tf=256):
    rows, d = hs.shape
    tm = MOE_TILE
    f = w_gate.shape[2]
    rows2 = lambda r, te, nv: (r, 0)
    expert3 = lambda r, te, nv: (te[r], 0, 0)
    return pl.pallas_call(
        functools.partial(_experts_kernel, tf=tf),
        grid_spec=pltpu.PrefetchScalarGridSpec(
            num_scalar_prefetch=2,
            grid=(rows // tm,),
            in_specs=[pl.BlockSpec((tm, d), rows2),
                      pl.BlockSpec((tm, LANES), rows2),
                      pl.BlockSpec((1, d, f), expert3),
                      pl.BlockSpec((1, d, f), expert3),
                      pl.BlockSpec((1, f, d), expert3)],
            out_specs=pl.BlockSpec((tm, d), rows2),
            scratch_shapes=[pltpu.VMEM((tm, d), F32)]),
        out_shape=jax.ShapeDtypeStruct((rows, d), BF16),
        compiler_params=_params(("arbitrary",)),
        name="moe_experts",
    )(plan["texp"], plan["n_valid"], hs, ws, w_gate, w_up, w_down)


def _combine_windows(before, total, off, n_rows):
    nblk = before.shape[0]
    after = jnp.concatenate([before[1:], total[None, :]], axis=0)
    lo = off[None, :] + before
    hi = off[None, :] + after
    first = (lo // ROW_ALIGN) * ROW_ALIGN
    nwin = jnp.where(hi > lo, (hi - first + MOE_WIN - 1) // MOE_WIN, 0)
    ends = jnp.cumsum(nwin, axis=1)
    k = jnp.arange(COMBINE_MAX_WIN, dtype=jnp.int32)
    e_of = jnp.minimum(jnp.sum((ends[:, None, :] <= k[None, :, None]).astype(jnp.int32), axis=2), N_EXPERTS - 1)
    i_of = k[None, :] - jnp.take_along_axis(ends - nwin, e_of, axis=1)
    want = jnp.take_along_axis(first, e_of, axis=1) + i_of * MOE_WIN
    start = jnp.clip(want, 0, n_rows - MOE_WIN)
    limit = jnp.take_along_axis(hi, e_of, axis=1)
    flat = lambda a: a.reshape(-1).astype(jnp.int32)
    return flat(start), flat(want), flat(limit), ends[:, -1].astype(jnp.int32)


def _combine_kernel(start_ref, want_ref, limit_ref, nwin_ref, off_ref, tok_ref, x_ref, mod_ref, fnw_ref,
                    ys_hbm, o_ref, buf_ref, sem_ref, acc_ref):
    b = pl.program_id(0)
    n = nwin_ref[b]
    tok = tok_ref[0]
    pos = [_slot_positions(tok, k, off_ref, False) for k in range(2)]

    def window_copy(k, slot):
        start = pl.multiple_of(start_ref[b * COMBINE_MAX_WIN + k], ROW_ALIGN)
        return pltpu.make_async_copy(ys_hbm.at[pl.ds(start, MOE_WIN), :], buf_ref.at[slot], sem_ref.at[slot])

    @pl.when(n > 0)
    def _():
        window_copy(0, 0).start()

    acc_ref[...] = jnp.zeros_like(acc_ref)

    def window(k, carry):
        slot = k & 1

        @pl.when(k + 1 < n)
        def _():
            window_copy(k + 1, 1 - slot).start()

        window_copy(k, slot).wait()
        w = b * COMBINE_MAX_WIN + k
        local = start_ref[w] + lax.broadcasted_iota(jnp.int32, (1, MOE_WIN), 1)
        rowid = jnp.where(local >= want_ref[w], jnp.where(local < limit_ref[w], local, -1), -1).astype(F32)
        onehot = (jnp.where(pos[0] == rowid, 1.0, 0.0) + jnp.where(pos[1] == rowid, 1.0, 0.0))
        acc_ref[...] += jnp.dot(onehot.astype(BF16), buf_ref[slot], preferred_element_type=F32)
        return carry

    lax.fori_loop(0, n, window, 0)
    o_ref[0] = _rms(x_ref[0] + mod_ref[0][2:3] * acc_ref[...], fnw_ref[...])


def moe_combine(plan, before, total, tok, ys, x, mod, final_nw):
    bsz, s, d = x.shape
    nb = MOE_TILE
    per_b = s // nb
    start, want, limit, nwin = _combine_windows(before, total, plan["off"], ys.shape[0])
    tok_blk = lambda b, *_: (b, 0, 0)
    x_blk = lambda b, *_: (b // per_b, b % per_b, 0)
    return pl.pallas_call(
        _combine_kernel,
        grid_spec=pltpu.PrefetchScalarGridSpec(
            num_scalar_prefetch=5,
            grid=(bsz * per_b,),
            in_specs=[pl.BlockSpec((1, nb, LANES), tok_blk),
                      pl.BlockSpec((1, nb, d), x_blk),
                      pl.BlockSpec((1, 3, d), lambda b, *_: (b // per_b, 0, 0)),
                      pl.BlockSpec((1, d), lambda b, *_: (0, 0)),
                      pl.BlockSpec(memory_space=pl.ANY)],
            out_specs=pl.BlockSpec((1, nb, d), x_blk),
            scratch_shapes=[pltpu.VMEM((2, MOE_WIN, d), BF16), pltpu.SemaphoreType.DMA((2,)),
                            pltpu.VMEM((nb, d), F32)]),
        out_shape=jax.ShapeDtypeStruct(x.shape, F32),
        compiler_params=_params(("arbitrary",)),
        name="moe_combine",
    )(start, want, limit, nwin, plan["off"], tok, x, mod, final_nw.reshape(1, d), ys)


def moe_layer(x, mod, nw, w_router, w_gate, w_up, w_down, final_nw):
    bsz, s, d = x.shape
    nblk = bsz * s // MOE_TILE
    n_tiles = 2 * nblk + N_EXPERTS
    hb, tok, tokt, tokw, before, total = moe_router(x, mod, nw, w_router)
    before = before[:, 0, :N_EXPERTS].astype(jnp.int32)
    total = total[0, :N_EXPERTS].astype(jnp.int32)
    plan = _moe_plan(before, total, n_tiles)
    hs, ws = moe_gather(plan, tokt, tokw, hb, n_tiles)
    ys = moe_experts(plan, hs, ws, w_gate, w_up, w_down)
    return moe_combine(plan, before, total, tok, ys, x, mod, final_nw)


def kernel(x, c, e_norm_mix, e_mod_mix_w, e_mod_mix_b, e_w_in, e_fox_fb, e_gla_w2, e_gla_b2, e_gla_norm, e_w_o, e_norm_ffn, e_mod_ffn_w, e_mod_ffn_b, e_ffn_gate, e_ffn_up, e_ffn_down, o_norm_mix, o_mod_mix_w, o_mod_mix_b, o_w_in, o_lam_re, o_lam_im, o_log_dt, o_b_re, o_b_im, o_c_re, o_c_im, o_d_skip, o_w_glu, o_w_o, o_norm_ffn, o_mod_ffn_w, o_mod_ffn_b, o_router, o_exp_gate, o_exp_up, o_exp_down, final_norm):
    bsz, s, d = x.shape

    mod = adaln_mod(c, e_mod_mix_w[0], e_mod_mix_b[0])
    fq, fk, fv, gq, gk, gv, gg, small, cum = layer0_inproj(x, mod, e_norm_mix[0], e_w_in[0], e_fox_fb[0])
    fox = fox_attention(fq, fk, fv, cum)
    gla = gla_attention(gq, gk, gv, gg, small, e_gla_w2[0], e_gla_b2[0], e_gla_norm[0])
    mod_ffn = adaln_mod(c, e_mod_ffn_w[0], e_mod_ffn_b[0])
    x = mixer_out_and_ffn(x, fox, gla, e_w_o[0].astype(BF16), mod, mod_ffn, e_norm_ffn[0],
                          e_ffn_gate[0].astype(BF16), e_ffn_up[0].astype(BF16), e_ffn_down[0].astype(BF16))

    mod = adaln_mod(c, o_mod_mix_w[0], o_mod_mix_b[0])
    s5_tb = 256
    bb_re, bb_im, pow_re, pow_im = s5_discretise(o_lam_re[0], o_lam_im[0], o_log_dt[0], o_b_re[0], o_b_im[0],
                                                 s5_tb // S5_SEG)
    x = s5_layer(x, mod, o_norm_mix[0], o_w_in[0], bb_re, bb_im, pow_re, pow_im, o_c_re[0], o_c_im[0],
                 o_d_skip[0], o_w_glu[0], o_w_o[0], s5_tb)
    mod = adaln_mod(c, o_mod_ffn_w[0], o_mod_ffn_b[0])
    return moe_layer(x, mod, o_norm_ffn[0], o_router[0], o_exp_gate[0].astype(BF16),
                     o_exp_up[0].astype(BF16), o_exp_down[0].astype(BF16), final_norm)
```

```python
import functools

import jax
import jax.numpy as jnp
from jax import lax
from jax.experimental import pallas as pl
from jax.experimental.pallas import tpu as pltpu

F32 = jnp.float32
BF16 = jnp.bfloat16
EPS = 1e-6
NEG_BIG = -1e30
LOG2E = 1.4426950408889634
FOX_BIAS_PARTS = 3
FOX_ROWS = 128
FOX_UNROLL = 4

LANES = 128
VMEM_LIMIT = 56 * 1024 * 1024

HEAD_DIM = 64
FOX_HEADS = 8
FOX_DIM = FOX_HEADS * HEAD_DIM
GLA_HEADS = 4
GLA_DK = 64
GLA_DV = 128
GLA_QK = GLA_HEADS * GLA_DK
GLA_V = GLA_HEADS * GLA_DV
GLA_RANK = 16
GLA_TAU = 16.0
GLA_CHUNK = 64
GLA_SUB = 8
S5_GROUP = 16
S5_STATE = 64
S5_GB = 8
N_EXPERTS = 8
SMALL_FF0 = 0
SMALL_LR0 = 8


def _params(sem):
    return pltpu.CompilerParams(dimension_semantics=sem, vmem_limit_bytes=VMEM_LIMIT)


def _log_sigmoid(x):
    return jnp.minimum(x, 0.0) - jnp.log(1.0 + jnp.exp(-jnp.abs(x)))


def _silu(x):
    return x * jax.nn.sigmoid(x)


def _gelu_tanh(x):
    c = 0.7978845608028654
    return 0.5 * x * (1.0 + jnp.tanh(c * (x + 0.044715 * (x * x * x))))


def _rms(x, w):
    return x * lax.rsqrt(jnp.mean(x * x, axis=-1, keepdims=True) + EPS) * w


def _norm_mod(x, nw, mod):
    return _rms(x, nw) * (1.0 + mod[1:2]) + mod[0:1]


def _split_bf16(x, pieces):
    out = []
    for _ in range(pieces):
        p = x.astype(BF16)
        out.append(p)
        x = x - p.astype(F32)
    return out


def _dot_f32ish(a, b_hi, b_lo):
    a_hi, a_lo = _split_bf16(a, 2)
    return (jnp.dot(a_hi, b_hi, preferred_element_type=F32) + jnp.dot(a_hi, b_lo, preferred_element_type=F32)
            + jnp.dot(a_lo, b_hi, preferred_element_type=F32))


def _dot_exact_left(a_bf16, b):
    return sum(jnp.dot(a_bf16, p, preferred_element_type=F32) for p in _split_bf16(b, 3))


def _nt_dot(a, b):
    return lax.dot_general(a, b, (((1,), (1,)), ((), ())), preferred_element_type=F32)


def _mod_kernel(ct_ref, w_ref, b_ref, o_ref):
    s = _silu(ct_ref[...])
    w = w_ref[...]
    for b in range(s.shape[1]):
        o_ref[b:b + 1, :] = jnp.sum(s[:, b:b + 1] * w, axis=0, keepdims=True) + b_ref[...]


def adaln_mod(c, w, b):
    bsz, d = c.shape
    n = w.shape[1]
    tn = 512
    out = pl.pallas_call(
        _mod_kernel,
        grid=(n // tn,),
        in_specs=[pl.BlockSpec((d, bsz), lambda j: (0, 0)),
                  pl.BlockSpec((d, tn), lambda j: (0, j)),
                  pl.BlockSpec((1, tn), lambda j: (0, j))],
        out_specs=pl.BlockSpec((bsz, tn), lambda j: (0, j)),
        out_shape=jax.ShapeDtypeStruct((bsz, n), F32),
        compiler_params=_params(("parallel",)),
        name="adaln_mod",
    )(c.T, w, b.reshape(1, n))
    return out.reshape(bsz, 3, d)


def _inproj_kernel(x_ref, mod_ref, nw_ref, wbig_ref, wsm_ref, fb_ref, tri_ref,
                   fq_ref, fk_ref, fv_ref, gq_ref, gk_ref, gv_ref, gg_ref, sm_ref, cum_ref,
                   carry_ref):
    i = pl.program_id(1)
    tm = x_ref.shape[1]
    h = _norm_mod(x_ref[0], nw_ref[...], mod_ref[0])
    hb = h.astype(BF16)
    col = 0
    for ref in (fq_ref, fk_ref, fv_ref, gq_ref, gk_ref, gv_ref, gg_ref):
        n = ref.shape[2]
        y = jnp.dot(hb, wbig_ref[:, col:col + n], preferred_element_type=F32)
        if ref is fq_ref:
            y = y * (LOG2E * HEAD_DIM ** -0.5)
        ref[0] = y.astype(ref.dtype)
        col += n
    small = _dot_f32ish(h, wsm_ref[0], wsm_ref[1])
    sm_ref[0] = small
    logf = _log_sigmoid(small + fb_ref[...])

    @pl.when(i == 0)
    def _():
        carry_ref[...] = jnp.zeros_like(carry_ref)

    cum = _dot_exact_left(tri_ref[...], logf) + carry_ref[...]
    cum_ref[0] = cum
    carry_ref[...] = cum[tm - 1:tm, :]


def layer0_inproj(x, mod, nw, w_in, fox_fb, tm=512):
    bsz, s, d = x.shape
    c0 = 3 * FOX_DIM
    c_ff = c0
    c_g = c_ff + FOX_HEADS
    c_lr = c_g + 2 * GLA_QK + 2 * GLA_V
    wbig = jnp.concatenate([w_in[:, :c0], w_in[:, c_g:c_lr]], axis=1).astype(BF16)
    wsm = jnp.zeros((d, LANES), F32)
    wsm = wsm.at[:, SMALL_FF0:SMALL_FF0 + FOX_HEADS].set(w_in[:, c_ff:c_g])
    wsm = wsm.at[:, SMALL_LR0:SMALL_LR0 + GLA_RANK].set(w_in[:, c_lr:])
    fb = jnp.zeros((1, LANES), F32).at[0, SMALL_FF0:SMALL_FF0 + FOX_HEADS].set(fox_fb)
    wsm = jnp.stack(_split_bf16(wsm, 2))
    tri = jnp.tril(jnp.ones((tm, tm), BF16))
    widths = (FOX_DIM, FOX_DIM, FOX_DIM, GLA_QK, GLA_QK, GLA_V, GLA_V)
    row = lambda b, i: (b, i, 0)
    const2 = lambda b, i: (0, 0)
    outs = pl.pallas_call(
        _inproj_kernel,
        grid=(bsz, s // tm),
        in_specs=[pl.BlockSpec((1, tm, d), row),
                  pl.BlockSpec((1, 3, d), lambda b, i: (b, 0, 0)),
                  pl.BlockSpec((1, d), const2),
                  pl.BlockSpec(wbig.shape, const2),
                  pl.BlockSpec(wsm.shape, lambda b, i: (0, 0, 0)),
                  pl.BlockSpec((1, LANES), const2),
                  pl.BlockSpec((tm, tm), const2)],
        out_specs=[pl.BlockSpec((1, tm, n), row) for n in widths]
                  + [pl.BlockSpec((1, tm, LANES), row)] * 2,
        out_shape=[jax.ShapeDtypeStruct((bsz, s, n), BF16) for n in widths]
                  + [jax.ShapeDtypeStruct((bsz, s, LANES), F32)] * 2,
        scratch_shapes=[pltpu.VMEM((1, LANES), F32)],
        compiler_params=_params(("parallel", "arbitrary")),
        name="layer0_inproj",
    )(x, mod, nw.reshape(1, d), wbig, wsm, fb, tri)
    return outs


def _fox_kernel(q_ref, k_ref, v_ref, c_ref, o_ref, kaug_ref, vaug_ref, m_ref, acc_ref, qaug_ref):
    pair = pl.program_id(1)
    i = pl.program_id(2)
    t = q_ref.shape[1]
    lane = lax.broadcasted_iota(jnp.int32, (1, LANES), 1)
    heads = (lane < HEAD_DIM, lane >= HEAD_DIM)
    spare = (HEAD_DIM, 0)

    @pl.when(i == 0)
    def _():
        k = k_ref[0].astype(F32)
        v = v_ref[0].astype(F32)
        cum = c_ref[0]
        for h in range(2):
            bias = jnp.sum(jnp.where(lane == 2 * pair + h, cum, 0.0), axis=1, keepdims=True) * (-LOG2E)
            ka = jnp.where(heads[h], k, 0.0)
            rest = bias
            for n in range(FOX_BIAS_PARTS):
                piece = rest.astype(BF16).astype(F32)
                rest = rest - piece
                ka = jnp.where(lane == spare[h] + n, piece, ka)
            kaug_ref[h] = ka.astype(BF16)
            vaug_ref[h, :, :LANES] = jnp.where(heads[h], v, 0.0).astype(BF16)
            vaug_ref[h, :, LANES:] = jnp.broadcast_to(jnp.where(heads[h], 1.0, 0.0), v.shape).astype(BF16)

    q = q_ref[0].astype(F32)
    for h in range(2):
        ones_here = jnp.where(lane >= spare[h], jnp.where(lane < spare[h] + FOX_BIAS_PARTS, 1.0, 0.0), 0.0)
        qaug_ref[h] = jnp.where(heads[h], q, ones_here).astype(BF16)
    m_ref[...] = jnp.full_like(m_ref, NEG_BIG)
    acc_ref[...] = jnp.zeros_like(acc_ref)

    def tile(j, diagonal):
        keys = pl.ds(pl.multiple_of(j * t, t), t)
        for r0 in range(0, t, FOX_ROWS):
            rows = slice(r0, r0 + FOX_ROWS)
            alphas = []
            upd = None
            for h in range(2):
                s = _nt_dot(qaug_ref[h, rows, :], kaug_ref[h, keys, :])
                if diagonal:
                    r = r0 + lax.broadcasted_iota(jnp.int32, (FOX_ROWS, t), 0)
                    c = lax.broadcasted_iota(jnp.int32, (FOX_ROWS, t), 1)
                    s = jnp.where(c <= r, s, NEG_BIG)
                m_prev = m_ref[h, rows, :]
                m_next = jnp.maximum(m_prev, jnp.max(s, axis=1, keepdims=True))
                m_ref[h, rows, :] = m_next
                p = jnp.exp2(s - jnp.concatenate([m_next] * (t // LANES), axis=1)).astype(BF16)
                alphas.append(jnp.exp2(m_prev - m_next))
                d = jnp.dot(p, vaug_ref[h, keys, :], preferred_element_type=F32)
                upd = d if upd is None else upd + d
            alpha = jnp.where(heads[0], alphas[0], alphas[1])
            acc_ref[rows, :] = jnp.concatenate([alpha, alpha], axis=1) * acc_ref[rows, :] + upd

    done = 0
    width = FOX_UNROLL
    while width >= 1:
        def group(jj, carry, width=width, done=done):
            for u in range(width):
                tile(done + width * jj + u, False)
            return carry

        trips = (i - done) // width
        lax.fori_loop(0, trips, group, 0)
        done = done + trips * width
        width //= 2
    tile(i, True)
    o_ref[0] = (acc_ref[:, :LANES] / acc_ref[:, LANES:]).astype(o_ref.dtype)


def fox_attention(fq, fk, fv, cum, t=512):
    bsz, s, _ = fq.shape
    n_pair = FOX_HEADS // 2
    seq = lambda b, p, i: (b, 0, p)
    return pl.pallas_call(
        _fox_kernel,
        grid=(bsz, n_pair, s // t),
        in_specs=[pl.BlockSpec((1, t, LANES), lambda b, p, i: (b, i, p)),
                  pl.BlockSpec((1, s, LANES), seq),
                  pl.BlockSpec((1, s, LANES), seq),
                  pl.BlockSpec((1, s, LANES), lambda b, p, i: (b, 0, 0))],
        out_specs=pl.BlockSpec((1, t, LANES), lambda b, p, i: (b, i, p)),
        out_shape=jax.ShapeDtypeStruct((bsz, s, FOX_DIM), BF16),
        scratch_shapes=[pltpu.VMEM((2, s, LANES), BF16), pltpu.VMEM((2, s, 2 * LANES), BF16),
                        pltpu.VMEM((2, t, LANES), F32), pltpu.VMEM((t, 2 * LANES), F32),
                        pltpu.VMEM((2, t, LANES), BF16)],
        compiler_params=_params(("parallel", "parallel", "arbitrary")),
        name="fox_attention",
    )(fq, fk, fv, cum)


def _gla_pair_chunk(q, k, v, la, st, tri, eye, k_scr, b_scr):
    C, SUB = GLA_CHUNK, GLA_SUB
    nsub = C // SUB
    lane = lax.broadcasted_iota(jnp.int32, (1, LANES), 1)
    head_a = lane < GLA_DK
    b = _dot_exact_left(tri, la)
    b_last = b[C - 1:C, :]
    k_scr[...] = k
    b_scr[...] = b
    rowblk = lax.broadcasted_iota(jnp.int32, (C, 1), 0) // SUB

    refs = [b[m * SUB - 1:m * SUB, :] for m in range(1, nsub)]
    rsel = refs[-1]
    for m in range(nsub - 2, 0, -1):
        rsel = jnp.where(rowblk == m, refs[m - 1], rsel)
    qt = q * jnp.exp(jnp.minimum(b - rsel, 0.0))
    zero = jnp.zeros_like(q)
    qs = jnp.concatenate([jnp.where(rowblk == m, qt, zero) for m in range(1, nsub)], axis=1)
    ks = jnp.concatenate(
        [jnp.where(rowblk < m, k * jnp.exp(jnp.minimum(refs[m - 1] - b, 0.0)), zero)
         for m in range(1, nsub)], axis=1)
    lane3 = lax.broadcasted_iota(jnp.int32, (1, (nsub - 1) * LANES), 1)
    head_a3 = (lane3 & (LANES - 1)) < GLA_DK
    zero3 = jnp.zeros_like(ks)
    kstack = jnp.concatenate([jnp.where(head_a3, ks, zero3), jnp.where(head_a3, zero3, ks)], axis=0)
    a_off = _nt_dot(qs.astype(BF16), kstack.astype(BF16))

    row16 = lax.broadcasted_iota(jnp.int32, (SUB, 1), 0)
    blocks = []
    for i in range(nsub):
        qb = q[i * SUB:(i + 1) * SUB, :]
        bb = b[i * SUB:(i + 1) * SUB, :]
        d = jnp.zeros((SUB, LANES), F32)
        for sp in range(SUB):
            srow = i * SUB + sp
            e = qb * k_scr[srow:srow + 1, :] * jnp.exp(jnp.minimum(bb - b_scr[srow:srow + 1, :], 0.0))
            da = jnp.sum(jnp.where(head_a, e, 0.0), axis=1, keepdims=True)
            db = jnp.sum(jnp.where(head_a, 0.0, e), axis=1, keepdims=True)
            live = row16 >= sp
            da = jnp.where(live, da, 0.0)
            db = jnp.where(live, db, 0.0)
            d = jnp.where(lane == srow, da, jnp.where(lane == GLA_DK + srow, db, d))
        blocks.append(d)
    a_pair = a_off + jnp.concatenate(blocks, axis=0)

    lane_v = lax.broadcasted_iota(jnp.int32, (1, 2 * GLA_DV), 1)
    first_v = lane_v < GLA_DV
    zv = jnp.zeros_like(v)
    vbd = jnp.concatenate([jnp.where(first_v, v, zv), jnp.where(first_v, zv, v)], axis=0)
    qh = q * jnp.exp(b)
    o = (jnp.dot(a_pair.astype(BF16), vbd, preferred_element_type=F32)
         + _nt_dot(qh.astype(BF16), st.astype(BF16)))

    k_end = k * jnp.exp(b_last - b)
    v_t = _nt_dot(eye, v).astype(BF16)
    inc = jnp.dot(v_t, k_end.astype(BF16), preferred_element_type=F32)
    row_v = lax.broadcasted_iota(jnp.int32, (2 * GLA_DV, 1), 0)
    same_head = (row_v // GLA_DV) == (lane // GLA_DK)
    st_new = st * jnp.exp(b_last) + jnp.where(same_head, inc, 0.0)
    return o, st_new


def _gla_kernel(gq_ref, gk_ref, gv_ref, gg_ref, sm_ref, w2_ref, b2_ref, gn_ref, tri_ref, eye_ref,
                o_ref, st_ref, la_ref, k_scr, b_scr):
    i = pl.program_id(1)
    tm = gq_ref.shape[1]
    n_pair = GLA_HEADS // 2
    scale = GLA_DK ** -0.5

    @pl.when(i == 0)
    def _():
        st_ref[...] = jnp.zeros_like(st_ref)

    z = _dot_f32ish(sm_ref[0], w2_ref[0], w2_ref[1]) + b2_ref[...]
    la_ref[...] = _log_sigmoid(z) * (1.0 / GLA_TAU)
    gn = gn_ref[...]

    def chunk(c, carry):
        base = pl.multiple_of(c * GLA_CHUNK, GLA_CHUNK)
        rows = pl.ds(base, GLA_CHUNK)
        for hp in range(n_pair):
            ql = slice(hp * LANES, (hp + 1) * LANES)
            vl = slice(hp * 2 * GLA_DV, (hp + 1) * 2 * GLA_DV)
            q = gq_ref[0, rows, ql].astype(F32) * scale
            k = gk_ref[0, rows, ql].astype(F32)
            v = gv_ref[0, rows, vl]
            o, st_new = _gla_pair_chunk(q, k, v, la_ref[rows, ql], st_ref[hp], tri_ref[...],
                                        eye_ref[...], k_scr.at[hp], b_scr.at[hp])
            st_ref[hp] = st_new
            gate = _silu(gg_ref[0, rows, vl].astype(F32))
            halves = [_rms(o[:, h * GLA_DV:(h + 1) * GLA_DV], gn) for h in range(2)]
            o_ref[0, rows, vl] = (jnp.concatenate(halves, axis=1) * gate).astype(o_ref.dtype)
        return carry

    lax.fori_loop(0, tm // GLA_CHUNK, chunk, 0, unroll=4)


def gla_attention(gq, gk, gv, gg, small, w2, b2, gnorm, tm=512):
    bsz, s, _ = gq.shape
    w2p = jnp.zeros((LANES, GLA_QK), F32).at[SMALL_LR0:SMALL_LR0 + GLA_RANK].set(w2)
    w2p = jnp.stack(_split_bf16(w2p, 2))
    tri = jnp.tril(jnp.ones((GLA_CHUNK, GLA_CHUNK), BF16))
    eye = jnp.eye(2 * GLA_DV, dtype=BF16)
    row = lambda b, i: (b, i, 0)
    const2 = lambda b, i: (0, 0)
    return pl.pallas_call(
        _gla_kernel,
        grid=(bsz, s // tm),
        in_specs=[pl.BlockSpec((1, tm, GLA_QK), row), pl.BlockSpec((1, tm, GLA_QK), row),
                  pl.BlockSpec((1, tm, GLA_V), row), pl.BlockSpec((1, tm, GLA_V), row),
                  pl.BlockSpec((1, tm, LANES), row),
                  pl.BlockSpec(w2p.shape, lambda b, i: (0, 0, 0)), pl.BlockSpec((1, GLA_QK), const2),
                  pl.BlockSpec((1, GLA_DV), const2), pl.BlockSpec(tri.shape, const2),
                  pl.BlockSpec(eye.shape, const2)],
        out_specs=pl.BlockSpec((1, tm, GLA_V), row),
        out_shape=jax.ShapeDtypeStruct((bsz, s, GLA_V), BF16),
        scratch_shapes=[pltpu.VMEM((GLA_HEADS // 2, 2 * GLA_DV, LANES), F32),
                        pltpu.VMEM((tm, GLA_QK), F32),
                        pltpu.VMEM((GLA_HEADS // 2, GLA_CHUNK, LANES), F32),
                        pltpu.VMEM((GLA_HEADS // 2, GLA_CHUNK, LANES), F32)],
        compiler_params=_params(("parallel", "arbitrary")),
        name="gla_attention",
    )(gq, gk, gv, gg, small, w2p, b2.reshape(1, GLA_QK), gnorm.reshape(1, GLA_DV), tri, eye)


def _mix_ffn_kernel(x_ref, a_ref, b_ref, wo_ref, modm_ref, modf_ref, nw_ref, wg_ref, wu_ref, wd_ref,
                    o_ref, acc_ref, *, tf):
    na = a_ref.shape[2]
    y = (jnp.dot(a_ref[0], wo_ref[:na, :], preferred_element_type=F32)
         + jnp.dot(b_ref[0], wo_ref[na:, :], preferred_element_type=F32))
    x1 = x_ref[0] + modm_ref[0][2:3] * y
    hb = _norm_mod(x1, nw_ref[...], modf_ref[0]).astype(BF16)
    for j in range(wg_ref.shape[1] // tf):
        cols = slice(j * tf, (j + 1) * tf)
        act = (_silu(jnp.dot(hb, wg_ref[:, cols], preferred_element_type=F32))
               * jnp.dot(hb, wu_ref[:, cols], preferred_element_type=F32))
        part = jnp.dot(act.astype(BF16), wd_ref[cols, :], preferred_element_type=F32)
        if j == 0:
            acc_ref[...] = part
        else:
            acc_ref[...] += part
    o_ref[0] = x1 + modf_ref[0][2:3] * acc_ref[...]


def mixer_out_and_ffn(x, a, b, w_o, mod_mix, mod_ffn, nw, w_gate, w_up, w_down, tm=512, tf=256):
    bsz, s, d = x.shape
    row = lambda bb, i: (bb, i, 0)
    per_b = lambda bb, i: (bb, 0, 0)
    const2 = lambda bb, i: (0, 0)
    resident = lambda arr: pl.BlockSpec(arr.shape, const2, pipeline_mode=pl.Buffered(1))
    return pl.pallas_call(
        functools.partial(_mix_ffn_kernel, tf=tf),
        grid=(bsz, s // tm),
        in_specs=[pl.BlockSpec((1, tm, d), row),
                  pl.BlockSpec((1, tm, a.shape[2]), row),
                  pl.BlockSpec((1, tm, b.shape[2]), row),
                  resident(w_o),
                  pl.BlockSpec((1, 3, d), per_b),
                  pl.BlockSpec((1, 3, d), per_b),
                  pl.BlockSpec((1, d), const2),
                  resident(w_gate), resident(w_up), resident(w_down)],
        out_specs=pl.BlockSpec((1, tm, d), row),
        out_shape=jax.ShapeDtypeStruct(x.shape, F32),
        scratch_shapes=[pltpu.VMEM((tm, d), F32)],
        compiler_params=_params(("parallel", "parallel")),
        name="mixer_out_and_ffn",
    )(x, a, b, w_o, mod_mix, mod_ffn, nw.reshape(1, d), w_gate, w_up, w_down)


S5_SEG = 8


def _s5_disc_kernel(lre_ref, lim_ref, ldt_ref, bre_ref, bim_ref,
                    bbre_ref, bbim_ref, pre_ref, pim_ref):
    lre = lre_ref[...]
    lim = lim_ref[...]
    dt = jnp.exp(ldt_ref[...])
    mag = jnp.exp(lre * dt)
    a_re = mag * jnp.cos(lim * dt)
    a_im = mag * jnp.sin(lim * dt)
    den = lre * lre + lim * lim
    nr = a_re - 1.0
    ni = a_im
    f_re = (nr * lre + ni * lim) / den
    f_im = (ni * lre - nr * lim) / den
    bre = bre_ref[...]
    bim = bim_ref[...]
    bbre_ref[...] = f_re[:, None, :] * bre - f_im[:, None, :] * bim
    bbim_ref[...] = f_re[:, None, :] * bim + f_im[:, None, :] * bre
    n = (lax.broadcasted_iota(jnp.int32, pre_ref.shape, 0) + 1).astype(F32)
    mag_n = jnp.exp(n * (lre * dt)[None])
    ang_n = n * (lim * dt)[None]
    pre_ref[...] = mag_n * jnp.cos(ang_n)
    pim_ref[...] = mag_n * jnp.sin(ang_n)


def s5_discretise(lam_re, lam_im, log_dt, b_re, b_im, seg_len):
    g, p = lam_re.shape
    k = b_re.shape[2]
    return pl.pallas_call(
        _s5_disc_kernel,
        out_shape=[jax.ShapeDtypeStruct((g, k, p), F32)] * 2 + [jax.ShapeDtypeStruct((seg_len, g, p), F32)] * 2,
        name="s5_discretise",
    )(lam_re, lam_im, log_dt.reshape(g, 1), b_re.transpose(0, 2, 1), b_im.transpose(0, 2, 1))


def _s5_kernel(x_ref, mod_ref, nw_ref, perm_ref, permt_ref, win_ref, bbd_ref, cbd_ref, apow_ref, dsk_ref,
               wglu_ref, wo_ref, o_ref, carry_ref, bu_ref, xs_ref, xsb_ref):
    i = pl.program_id(1)
    tb = x_ref.shape[1]
    nblk = bu_ref.shape[0]
    nch = bu_ref.shape[2] // (2 * LANES)
    seg_len = tb // S5_SEG
    re_l = [slice(ch * LANES, (ch + 1) * LANES) for ch in range(nch)]
    im_l = [slice((nch + ch) * LANES, (nch + ch + 1) * LANES) for ch in range(nch)]

    @pl.when(i == 0)
    def _():
        carry_ref[...] = jnp.zeros_like(carry_ref)

    x = x_ref[0]
    h = _norm_mod(x, nw_ref[...], mod_ref[0]).astype(BF16)
    hp = jnp.dot(perm_ref[...], h, preferred_element_type=F32).astype(BF16)
    u = jnp.dot(hp, win_ref[...], preferred_element_type=F32)
    ub = u.astype(BF16)
    for jb in range(nblk):
        bu_ref[jb] = jnp.dot(ub[:, jb * LANES:(jb + 1) * LANES], bbd_ref[jb], preferred_element_type=F32)

    seg_row = lax.broadcasted_iota(jnp.int32, (S5_SEG, 1), 0)

    def channel_block(jb, blk_carry):
        def bcast(n, lanes):
            return jnp.broadcast_to(apow_ref[jb, n:n + 1, lanes], (S5_SEG, LANES))

        a1 = [(bcast(0, re_l[ch]), bcast(0, im_l[ch])) for ch in range(nch)]

        def step(t, state):
            rows = pl.ds(pl.multiple_of(t * S5_SEG, S5_SEG), S5_SEG)
            out = []
            for ch in range(nch):
                xr, xi = state[2 * ch], state[2 * ch + 1]
                ar, ai = a1[ch]
                nr = ar * xr - ai * xi + bu_ref[jb, rows, re_l[ch]]
                ni = ar * xi + ai * xr + bu_ref[jb, rows, im_l[ch]]
                xs_ref[jb, rows, re_l[ch]] = nr
                xs_ref[jb, rows, im_l[ch]] = ni
                out += [nr, ni]
            return tuple(out)

        zero = jnp.zeros((S5_SEG, LANES), F32)
        ends = lax.fori_loop(0, seg_len, step, (zero,) * (2 * nch), unroll=4)

        starts = []
        for ch in range(nch):
            alr = apow_ref[jb, seg_len - 1:seg_len, re_l[ch]]
            ali = apow_ref[jb, seg_len - 1:seg_len, im_l[ch]]
            er, ei = ends[2 * ch], ends[2 * ch + 1]
            zr = carry_ref[jb, :, re_l[ch]]
            zi = carry_ref[jb, :, im_l[ch]]
            tr = jnp.zeros((S5_SEG, LANES), F32)
            ti = jnp.zeros((S5_SEG, LANES), F32)
            for s in range(S5_SEG):
                tr = jnp.where(seg_row == s, zr, tr)
                ti = jnp.where(seg_row == s, zi, ti)
                zr, zi = (alr * zr - ali * zi + er[s:s + 1], alr * zi + ali * zr + ei[s:s + 1])
            carry_ref[jb, :, re_l[ch]] = zr
            carry_ref[jb, :, im_l[ch]] = zi
            starts.append((jnp.concatenate([tr, tr], axis=0), jnp.concatenate([ti, ti], axis=0)))

        for t2 in range(seg_len // 2):
            rows = slice(t2 * 2 * S5_SEG, (t2 + 1) * 2 * S5_SEG)
            for ch in range(nch):
                pr = jnp.concatenate([bcast(2 * t2, re_l[ch]), bcast(2 * t2 + 1, re_l[ch])], axis=0)
                pi = jnp.concatenate([bcast(2 * t2, im_l[ch]), bcast(2 * t2 + 1, im_l[ch])], axis=0)
                zr, zi = starts[ch]
                xsb_ref[jb, rows, re_l[ch]] = (xs_ref[jb, rows, re_l[ch]] + pr * zr - pi * zi).astype(BF16)
                xsb_ref[jb, rows, im_l[ch]] = (xs_ref[jb, rows, im_l[ch]] + pr * zi + pi * zr).astype(BF16)
        return blk_carry

    lax.fori_loop(0, nblk, channel_block, 0)

    ys = [jnp.dot(xsb_ref[jb], cbd_ref[jb], preferred_element_type=F32) for jb in range(nblk)]
    y = jnp.concatenate(ys, axis=1) + dsk_ref[...] * u
    y = _gelu_tanh(y)
    y = y * jax.nn.sigmoid(jnp.dot(y.astype(BF16), wglu_ref[...], preferred_element_type=F32))
    z = jnp.dot(y.astype(BF16), wo_ref[...], preferred_element_type=F32)
    z_hi = z.astype(BF16)
    z_lo = (z - z_hi.astype(F32)).astype(BF16)
    z = (jnp.dot(permt_ref[...], z_hi, preferred_element_type=F32)
         + jnp.dot(permt_ref[...], z_lo, preferred_element_type=F32))
    o_ref[0] = x + mod_ref[0][2:3] * z


def s5_layer(x, mod, nw, w_in, bb_re, bb_im, pow_re, pow_im, c_re, c_im, d_skip, w_glu, w_o, tb):
    bsz, s, d = x.shape
    g, k, p = bb_re.shape
    nblk = g // S5_GB
    seg_len = pow_re.shape[0]
    assert tb == seg_len * S5_SEG
    eye = jnp.eye(S5_GB, dtype=F32)

    def blockdiag_in(bb):
        t = jnp.einsum('jgkp,gh->jgkhp', bb.reshape(nblk, S5_GB, k, p), eye)
        return t.reshape(nblk, S5_GB * k, S5_GB * p)

    def blockdiag_out(c):
        t = jnp.einsum('jgkp,gh->jgphk', c.reshape(nblk, S5_GB, k, p), eye)
        return t.reshape(nblk, S5_GB * p, S5_GB * k)

    bbd = jnp.concatenate([blockdiag_in(bb_re), blockdiag_in(bb_im)], axis=2).astype(BF16)
    cbd = jnp.concatenate([blockdiag_out(c_re), -blockdiag_out(c_im)], axis=1).astype(BF16)
    half = S5_GB * p
    apow = jnp.concatenate([pow_re.reshape(seg_len, nblk, half), pow_im.reshape(seg_len, nblk, half)],
                           axis=2).transpose(1, 0, 2)
    new_row = jnp.arange(tb)
    old_row = (new_row % S5_SEG) * seg_len + new_row // S5_SEG
    perm = (old_row[:, None] == jnp.arange(tb)[None, :]).astype(BF16)
    row = lambda b, i: (b, i, 0)
    const2 = lambda b, i: (0, 0)
    const3 = lambda b, i: (0, 0, 0)
    return pl.pallas_call(
        _s5_kernel,
        grid=(bsz, s // tb),
        in_specs=[pl.BlockSpec((1, tb, d), row),
                  pl.BlockSpec((1, 3, d), lambda b, i: (b, 0, 0)),
                  pl.BlockSpec((1, d), const2),
                  pl.BlockSpec((tb, tb), const2), pl.BlockSpec((tb, tb), const2),
                  pl.BlockSpec((d, d), const2),
                  pl.BlockSpec(bbd.shape, const3),
                  pl.BlockSpec(cbd.shape, const3),
                  pl.BlockSpec(apow.shape, const3),
                  pl.BlockSpec((1, d), const2),
                  pl.BlockSpec((d, d), const2), pl.BlockSpec((d, d), const2)],
        out_specs=pl.BlockSpec((1, tb, d), row),
        out_shape=jax.ShapeDtypeStruct(x.shape, F32),
        scratch_shapes=[pltpu.VMEM((nblk, 1, 2 * half), F32),
                        pltpu.VMEM((nblk, tb, 2 * half), F32),
                        pltpu.VMEM((nblk, tb, 2 * half), F32),
                        pltpu.VMEM((nblk, tb, 2 * half), BF16)],
        compiler_params=_params(("parallel", "arbitrary")),
        name="s5_layer",
    )(x, mod, nw.reshape(1, d), perm, perm.T, w_in.astype(BF16), bbd, cbd, apow, d_skip.reshape(1, d),
      w_glu.astype(BF16), w_o.astype(BF16))


TOK_E, TOK_RANK, TOK_W = 0, 2, 4
W_PIECES = 3
MOE_TILE = 512
MOE_WIN = 192
ROW_ALIGN = 16
COMBINE_GROUP = 2
COMBINE_MAX_WIN = -(-((2 * MOE_TILE + N_EXPERTS * (ROW_ALIGN - 1)) // MOE_WIN + N_EXPERTS + 1)
                    // COMBINE_GROUP) * COMBINE_GROUP


def _router_kernel(x_ref, mod_ref, nw_ref, wr_ref, tri_ref, sel_ref,
                   hb_ref, tok_ref, tokt_ref, tokw_ref, before_ref, total_ref, carry_ref):
    blk = pl.program_id(0)
    h = _norm_mod(x_ref[0], nw_ref[...], mod_ref[0])
    hb_ref[0] = h.astype(BF16)
    logits = _dot_f32ish(h, wr_ref[0], wr_ref[1])
    lane = lax.broadcasted_iota(jnp.int32, logits.shape, 1).astype(F32)
    logits = jnp.where(lane < N_EXPERTS, logits, -jnp.inf)
    m1 = jnp.max(logits, axis=1, keepdims=True)
    i1 = jnp.min(jnp.where(logits == m1, lane, float(LANES)), axis=1, keepdims=True)
    rest = jnp.where(lane == i1, -jnp.inf, logits)
    m2 = jnp.max(rest, axis=1, keepdims=True)
    i2 = jnp.min(jnp.where(rest == m2, lane, float(LANES)), axis=1, keepdims=True)
    e2 = jnp.exp(m2 - m1)
    w1 = 1.0 / (1.0 + e2)
    w2 = e2 / (1.0 + e2)

    @pl.when(blk == 0)
    def _():
        carry_ref[...] = jnp.zeros_like(carry_ref)

    routed = jnp.where(lane == i1, 1.0, jnp.where(lane == i2, 1.0, 0.0))
    before = carry_ref[...]
    rank = jnp.dot(tri_ref[...], routed.astype(BF16), preferred_element_type=F32) + before
    r1 = jnp.sum(jnp.where(lane == i1, rank, 0.0), axis=1, keepdims=True)
    r2 = jnp.sum(jnp.where(lane == i2, rank, 0.0), axis=1, keepdims=True)
    fields = ((TOK_E, i1), (TOK_E + 1, i2), (TOK_RANK, r1), (TOK_RANK + 1, r2), (TOK_W, w1), (TOK_W + 1, w2))
    tok = jnp.zeros_like(logits)
    for ln, val in fields:
        tok = jnp.where(lane == ln, val, tok)
    tok_ref[0] = tok
    tokt_ref[0] = sum(_nt_dot(sel_ref[...], p) for p in _split_bf16(tok, 3))
    pieces = jnp.zeros_like(logits)
    for k, w in enumerate((w1, w2)):
        rest = w
        for n in range(W_PIECES):
            piece = rest.astype(BF16).astype(F32)
            rest = rest - piece
            pieces = jnp.where(lane == k * W_PIECES + n, piece, pieces)
    tokw_ref[0] = pieces.astype(BF16)
    before_ref[0] = before
    total = before + jnp.sum(routed, axis=0, keepdims=True)
    carry_ref[...] = total
    total_ref[...] = total


def moe_router(x, mod, nw, w_router):
    bsz, s, d = x.shape
    nb = MOE_TILE
    per_b = s // nb
    nblk = bsz * per_b
    wr = jnp.stack(_split_bf16(jnp.zeros((d, LANES), F32).at[:, :N_EXPERTS].set(w_router), 2))
    tri = jnp.tril(jnp.ones((nb, nb), BF16), k=-1)
    sel = jnp.eye(8, LANES, dtype=BF16)
    const2 = lambda i: (0, 0)
    return pl.pallas_call(
        _router_kernel,
        grid=(nblk,),
        in_specs=[pl.BlockSpec((1, nb, d), lambda i: (i // per_b, i % per_b, 0)),
                  pl.BlockSpec((1, 3, d), lambda i: (i // per_b, 0, 0)),
                  pl.BlockSpec((1, d), const2),
                  pl.BlockSpec((2, d, LANES), lambda i: (0, 0, 0)),
                  pl.BlockSpec((nb, nb), const2),
                  pl.BlockSpec((8, LANES), const2)],
        out_specs=[pl.BlockSpec((1, nb, d), lambda i: (i, 0, 0)),
                   pl.BlockSpec((1, nb, LANES), lambda i: (i, 0, 0)),
                   pl.BlockSpec((1, 8, nb), lambda i: (i, 0, 0)),
                   pl.BlockSpec((1, nb, LANES), lambda i: (i, 0, 0)),
                   pl.BlockSpec((1, 1, LANES), lambda i: (i, 0, 0)),
                   pl.BlockSpec((1, LANES), const2)],
        out_shape=[jax.ShapeDtypeStruct((nblk, nb, d), BF16),
                   jax.ShapeDtypeStruct((nblk, nb, LANES), F32),
                   jax.ShapeDtypeStruct((nblk, 8, nb), F32),
                   jax.ShapeDtypeStruct((nblk, nb, LANES), BF16),
                   jax.ShapeDtypeStruct((nblk, 1, LANES), F32),
                   jax.ShapeDtypeStruct((1, LANES), F32)],
        scratch_shapes=[pltpu.VMEM((1, LANES), F32)],
        compiler_params=_params(("arbitrary",)),
        name="moe_router",
    )(x, mod, nw.reshape(1, d), wr, tri, sel)


def _moe_plan(before, total, n_tiles):
    tm = MOE_TILE
    ntile = (total + tm - 1) // tm
    tile_end = jnp.cumsum(ntile)
    tile_start = tile_end - ntile
    n_valid = tile_end[-1]
    r = jnp.arange(n_tiles, dtype=jnp.int32)
    texp = jnp.minimum(jnp.sum((tile_end[None, :] <= r[:, None]).astype(jnp.int32), axis=1), N_EXPERTS - 1)
    r0 = (r - tile_start[texp]) * tm
    after = jnp.concatenate([before[1:], total[None, :]], axis=0)
    lo = before[:, texp].T
    hi = after[:, texp].T
    meets = (lo < (r0 + tm)[:, None]) & (hi > r0[:, None]) & (r < n_valid)[:, None]
    count = jnp.sum(meets, axis=1).astype(jnp.int32)
    first_blk = jnp.sum(jnp.cumsum(meets, axis=1) == 0, axis=1).astype(jnp.int32)
    start = jnp.clip(lo - r0[:, None], 0, tm)
    end = jnp.clip(hi - r0[:, None], 0, tm)
    first = (start // ROW_ALIGN) * ROW_ALIGN
    nwin = jnp.where(meets, (end - first + MOE_WIN - 1) // MOE_WIN, 0)
    flat = lambda a: a.reshape(-1).astype(jnp.int32)
    return dict(texp=texp.astype(jnp.int32), n_valid=n_valid.astype(jnp.int32).reshape(1),
                off=(tile_start * tm).astype(jnp.int32),
                g_blk0=jnp.where(count > 0, first_blk, 0), g_count=count,
                g_first=flat(first), g_nwin=flat(nwin))


def _slot_positions(tok_fields, k, off_ref, along_lanes):
    if along_lanes:
        e = tok_fields[TOK_E + k:TOK_E + k + 1, :]
        rank = tok_fields[TOK_RANK + k:TOK_RANK + k + 1, :]
    else:
        e = tok_fields[:, TOK_E + k:TOK_E + k + 1]
        rank = tok_fields[:, TOK_RANK + k:TOK_RANK + k + 1]
    pos = rank
    for ex in range(N_EXPERTS):
        pos = pos + jnp.where(e == float(ex), off_ref[ex].astype(F32), 0.0)
    return pos


def _window_rows(want, tile, tm, shape, axis):
    base = pl.multiple_of(jnp.minimum(want, tm - MOE_WIN), ROW_ALIGN)
    local = base + lax.broadcasted_iota(jnp.int32, shape, axis)
    rowid = jnp.where(local >= want, tile * tm + local, -1).astype(F32)
    return base, rowid


def _gather_kernel(blk0_ref, count_ref, first_ref, nwin_ref, off_ref, tokt_ref, tokw_ref, hb_ref,
                   hs_ref, ws_ref, acc_ref, wacc_ref):
    r = pl.program_id(0)
    tm = hs_ref.shape[0]
    nblk = hb_ref.shape[0]
    b0 = blk0_ref[r]
    lane = lax.broadcasted_iota(jnp.int32, (1, LANES), 1)
    acc_ref[...] = jnp.zeros_like(acc_ref)
    wacc_ref[...] = jnp.zeros_like(wacc_ref)

    def pair(k, carry):
        blk = b0 + k
        tokt = tokt_ref[blk]
        pos = [_slot_positions(tokt, s, off_ref, True) for s in range(2)]
        p = r * nblk + blk

        def window(i, wcarry):
            base, rowid = _window_rows(first_ref[p] + i * MOE_WIN, r, tm, (MOE_WIN, 1), 0)
            onehots = [jnp.where(pos[s] == rowid, 1.0, 0.0).astype(BF16) for s in range(2)]
            rows = pl.ds(base, MOE_WIN)
            acc_ref[rows, :] += jnp.dot(onehots[0] + onehots[1], hb_ref[blk], preferred_element_type=F32)
            picked = [jnp.dot(onehots[s], tokw_ref[blk], preferred_element_type=F32) for s in range(2)]
            wacc_ref[rows, :] += jnp.where(lane < W_PIECES, picked[0],
                                           jnp.where(lane < 2 * W_PIECES, picked[1], 0.0))
            return wcarry

        lax.fori_loop(0, nwin_ref[p], window, 0)
        return carry

    lax.fori_loop(0, count_ref[r], pair, 0)
    hs_ref[...] = acc_ref[...].astype(hs_ref.dtype)
    ws_ref[...] = wacc_ref[...]


def moe_gather(plan, tokt, tokw, hb, n_tiles):
    nblk, nb, d = hb.shape
    tm = MOE_TILE
    by_tile2 = lambda r, *_: (r, 0)
    resident = lambda arr: pl.BlockSpec(arr.shape, lambda r, *_: (0, 0, 0), pipeline_mode=pl.Buffered(1))
    return pl.pallas_call(
        _gather_kernel,
        grid_spec=pltpu.PrefetchScalarGridSpec(
            num_scalar_prefetch=5,
            grid=(n_tiles,),
            in_specs=[resident(tokt), resident(tokw), resident(hb)],
            out_specs=[pl.BlockSpec((tm, d), by_tile2), pl.BlockSpec((tm, LANES), by_tile2)],
            scratch_shapes=[pltpu.VMEM((tm, d), F32), pltpu.VMEM((tm, LANES), F32)]),
        out_shape=[jax.ShapeDtypeStruct((n_tiles * tm, d), BF16),
                   jax.ShapeDtypeStruct((n_tiles * tm, LANES), F32)],
        compiler_params=_params(("arbitrary",)),
        name="moe_gather",
    )(plan["g_blk0"], plan["g_count"], plan["g_first"], plan["g_nwin"], plan["off"], tokt, tokw, hb)


def _experts_kernel(texp_ref, nvalid_ref, hs_ref, ws_ref, wg_ref, wu_ref, wd_ref, ys_ref, acc_ref, *, tf):
    r = pl.program_id(0)
    valid = r < nvalid_ref[0]

    @pl.when(valid)
    def _():
        hb = hs_ref[...]
        for j in range(wg_ref.shape[2] // tf):
            cols = slice(j * tf, (j + 1) * tf)
            act = (_silu(jnp.dot(hb, wg_ref[0, :, cols], preferred_element_type=F32))
                   * jnp.dot(hb, wu_ref[0, :, cols], preferred_element_type=F32))
            part = jnp.dot(act.astype(BF16), wd_ref[0, cols, :], preferred_element_type=F32)
            if j == 0:
                acc_ref[...] = part
            else:
                acc_ref[...] += part
        w = jnp.sum(ws_ref[...], axis=1, keepdims=True)
        ys_ref[...] = (acc_ref[...] * w).astype(ys_ref.dtype)

    @pl.when(jnp.logical_not(valid))
    def _():
        ys_ref[...] = jnp.zeros_like(ys_ref)


def moe_experts(plan, hs, ws, w_gate, w_up, w_down, tf=256):
    rows, d = hs.shape
    tm = MOE_TILE
    f = w_gate.shape[2]
    rows2 = lambda r, te, nv: (r, 0)
    expert3 = lambda r, te, nv: (te[r], 0, 0)
    return pl.pallas_call(
        functools.partial(_experts_kernel, tf=tf),
        grid_spec=pltpu.PrefetchScalarGridSpec(
            num_scalar_prefetch=2,
            grid=(rows // tm,),
            in_specs=[pl.BlockSpec((tm, d), rows2),
                      pl.BlockSpec((tm, LANES), rows2),
                      pl.BlockSpec((1, d, f), expert3),
                      pl.BlockSpec((1, d, f), expert3),
                      pl.BlockSpec((1, f, d), expert3)],
            out_specs=pl.BlockSpec((tm, d), rows2),
            scratch_shapes=[pltpu.VMEM((tm, d), F32)]),
        out_shape=jax.ShapeDtypeStruct((rows, d), BF16),
        compiler_params=_params(("arbitrary",)),
        name="moe_experts",
    )(plan["texp"], plan["n_valid"], hs, ws, w_gate, w_up, w_down)


def _combine_windows(before, total, off, n_rows):
    nblk = before.shape[0]
    after = jnp.concatenate([before[1:], total[None, :]], axis=0)
    lo = off[None, :] + before
    hi = off[None, :] + after
    first = (lo // ROW_ALIGN) * ROW_ALIGN
    nwin = jnp.where(hi > lo, (hi - first + MOE_WIN - 1) // MOE_WIN, 0)
    ends = jnp.cumsum(nwin, axis=1)
    k = jnp.arange(COMBINE_MAX_WIN, dtype=jnp.int32)
    e_of = jnp.minimum(jnp.sum((ends[:, None, :] <= k[None, :, None]).astype(jnp.int32), axis=2), N_EXPERTS - 1)
    i_of = k[None, :] - jnp.take_along_axis(ends - nwin, e_of, axis=1)
    live = k[None, :] < ends[:, -1:]
    want = jnp.where(live, jnp.take_along_axis(first, e_of, axis=1) + i_of * MOE_WIN, 0)
    start = jnp.clip(want, 0, n_rows - MOE_WIN)
    limit = jnp.where(live, jnp.take_along_axis(hi, e_of, axis=1), 0)
    flat = lambda a: a.reshape(-1).astype(jnp.int32)
    return flat(start), flat(want), flat(limit), ends[:, -1].astype(jnp.int32)


def _combine_kernel(start_ref, want_ref, limit_ref, nwin_ref, off_ref, tok_ref, x_ref, mod_ref, fnw_ref,
                    ys_hbm, o_ref, buf_ref, sem_ref, acc_ref):
    b = pl.program_id(0)
    trips = (nwin_ref[b] + COMBINE_GROUP - 1) // COMBINE_GROUP
    tok = tok_ref[0]
    pos = [_slot_positions(tok, k, off_ref, False) for k in range(2)]

    def window_copies(t, slot):
        copies = []
        for g in range(COMBINE_GROUP):
            start = pl.multiple_of(start_ref[b * COMBINE_MAX_WIN + t * COMBINE_GROUP + g], ROW_ALIGN)
            copies.append(pltpu.make_async_copy(ys_hbm.at[pl.ds(start, MOE_WIN), :],
                                                buf_ref.at[slot, pl.ds(g * MOE_WIN, MOE_WIN), :],
                                                sem_ref.at[slot, g]))
        return copies

    @pl.when(trips > 0)
    def _():
        for c in window_copies(0, 0):
            c.start()

    acc_ref[...] = jnp.zeros_like(acc_ref)

    def trip(t, carry):
        slot = t & 1

        @pl.when(t + 1 < trips)
        def _():
            for c in window_copies(t + 1, 1 - slot):
                c.start()

        for c in window_copies(t, slot):
            c.wait()
        ids = []
        for g in range(COMBINE_GROUP):
            w = b * COMBINE_MAX_WIN + t * COMBINE_GROUP + g
            local = start_ref[w] + lax.broadcasted_iota(jnp.int32, (1, MOE_WIN), 1)
            ids.append(jnp.where(local >= want_ref[w], jnp.where(local < limit_ref[w], local, -1), -1))
        rowid = jnp.concatenate(ids, axis=1).astype(F32)
        onehot = (jnp.where(pos[0] == rowid, 1.0, 0.0) + jnp.where(pos[1] == rowid, 1.0, 0.0))
        acc_ref[...] += jnp.dot(onehot.astype(BF16), buf_ref[slot], preferred_element_type=F32)
        return carry

    lax.fori_loop(0, trips, trip, 0)
    o_ref[0] = _rms(x_ref[0] + mod_ref[0][2:3] * acc_ref[...], fnw_ref[...])


def moe_combine(plan, before, total, tok, ys, x, mod, final_nw):
    bsz, s, d = x.shape
    nb = MOE_TILE
    per_b = s // nb
    start, want, limit, nwin = _combine_windows(before, total, plan["off"], ys.shape[0])
    tok_blk = lambda b, *_: (b, 0, 0)
    x_blk = lambda b, *_: (b // per_b, b % per_b, 0)
    return pl.pallas_call(
        _combine_kernel,
        grid_spec=pltpu.PrefetchScalarGridSpec(
            num_scalar_prefetch=5,
            grid=(bsz * per_b,),
            in_specs=[pl.BlockSpec((1, nb, LANES), tok_blk),
                      pl.BlockSpec((1, nb, d), x_blk),
                      pl.BlockSpec((1, 3, d), lambda b, *_: (b // per_b, 0, 0)),
                      pl.BlockSpec((1, d), lambda b, *_: (0, 0)),
                      pl.BlockSpec(memory_space=pl.ANY)],
            out_specs=pl.BlockSpec((1, nb, d), x_blk),
            scratch_shapes=[pltpu.VMEM((2, COMBINE_GROUP * MOE_WIN, d), BF16),
                            pltpu.SemaphoreType.DMA((2, COMBINE_GROUP)),
                            pltpu.VMEM((nb, d), F32)]),
        out_shape=jax.ShapeDtypeStruct(x.shape, F32),
        compiler_params=_params(("arbitrary",)),
        name="moe_combine",
    )(start, want, limit, nwin, plan["off"], tok, x, mod, final_nw.reshape(1, d), ys)


def moe_layer(x, mod, nw, w_router, w_gate, w_up, w_down, final_nw):
    bsz, s, d = x.shape
    nblk = bsz * s // MOE_TILE
    n_tiles = 2 * nblk + N_EXPERTS
    hb, tok, tokt, tokw, before, total = moe_router(x, mod, nw, w_router)
    before = before[:, 0, :N_EXPERTS].astype(jnp.int32)
    total = total[0, :N_EXPERTS].astype(jnp.int32)
    plan = _moe_plan(before, total, n_tiles)
    hs, ws = moe_gather(plan, tokt, tokw, hb, n_tiles)
    ys = moe_experts(plan, hs, ws, w_gate, w_up, w_down)
    return moe_combine(plan, before, total, tok, ys, x, mod, final_nw)


def kernel(x, c, e_norm_mix, e_mod_mix_w, e_mod_mix_b, e_w_in, e_fox_fb, e_gla_w2, e_gla_b2, e_gla_norm, e_w_o, e_norm_ffn, e_mod_ffn_w, e_mod_ffn_b, e_ffn_gate, e_ffn_up, e_ffn_down, o_norm_mix, o_mod_mix_w, o_mod_mix_b, o_w_in, o_lam_re, o_lam_im, o_log_dt, o_b_re, o_b_im, o_c_re, o_c_im, o_d_skip, o_w_glu, o_w_o, o_norm_ffn, o_mod_ffn_w, o_mod_ffn_b, o_router, o_exp_gate, o_exp_up, o_exp_down, final_norm):
    bsz, s, d = x.shape

    mod = adaln_mod(c, e_mod_mix_w[0], e_mod_mix_b[0])
    fq, fk, fv, gq, gk, gv, gg, small, cum = layer0_inproj(x, mod, e_norm_mix[0], e_w_in[0], e_fox_fb[0])
    fox = fox_attention(fq, fk, fv, cum)
    gla = gla_attention(gq, gk, gv, gg, small, e_gla_w2[0], e_gla_b2[0], e_gla_norm[0])
    mod_ffn = adaln_mod(c, e_mod_ffn_w[0], e_mod_ffn_b[0])
    x = mixer_out_and_ffn(x, fox, gla, e_w_o[0].astype(BF16), mod, mod_ffn, e_norm_ffn[0],
                          e_ffn_gate[0].astype(BF16), e_ffn_up[0].astype(BF16), e_ffn_down[0].astype(BF16))

    mod = adaln_mod(c, o_mod_mix_w[0], o_mod_mix_b[0])
    s5_tb = 256
    bb_re, bb_im, pow_re, pow_im = s5_discretise(o_lam_re[0], o_lam_im[0], o_log_dt[0], o_b_re[0], o_b_im[0],
                                                 s5_tb // S5_SEG)
    x = s5_layer(x, mod, o_norm_mix[0], o_w_in[0], bb_re, bb_im, pow_re, pow_im, o_c_re[0], o_c_im[0],
                 o_d_skip[0], o_w_glu[0], o_w_o[0], s5_tb)
    mod = adaln_mod(c, o_mod_ffn_w[0], o_mod_ffn_b[0])
    return moe_layer(x, mod, o_norm_ffn[0], o_router[0], o_exp_gate[0].astype(BF16),
                     o_exp_up[0].astype(BF16), o_exp_down[0].astype(BF16), final_norm)
```

```python
import functools

import jax
import jax.numpy as jnp
from jax import lax
from jax.experimental import pallas as pl
from jax.experimental.pallas import tpu as pltpu

F32 = jnp.float32
BF16 = jnp.bfloat16
EPS = 1e-6
NEG_BIG = -1e30
LOG2E = 1.4426950408889634
FOX_BIAS_PARTS = 3
FOX_ROWS = 128
FOX_UNROLL = 4

LANES = 128
VMEM_LIMIT = 56 * 1024 * 1024

HEAD_DIM = 64
FOX_HEADS = 8
FOX_DIM = FOX_HEADS * HEAD_DIM
GLA_HEADS = 4
GLA_DK = 64
GLA_DV = 128
GLA_QK = GLA_HEADS * GLA_DK
GLA_V = GLA_HEADS * GLA_DV
GLA_RANK = 16
GLA_TAU = 16.0
GLA_CHUNK = 64
GLA_SUB = 8
S5_GB = 8
N_EXPERTS = 8
SMALL_FF0 = 0
SMALL_LR0 = 8


def _params(sem):
    return pltpu.CompilerParams(dimension_semantics=sem, vmem_limit_bytes=VMEM_LIMIT)


def _log_sigmoid(x):
    return jnp.minimum(x, 0.0) - jnp.log(1.0 + jnp.exp(-jnp.abs(x)))


def _silu(x):
    return x * jax.nn.sigmoid(x)


def _gelu_tanh(x):
    c = 0.7978845608028654
    return 0.5 * x * (1.0 + jnp.tanh(c * (x + 0.044715 * (x * x * x))))


def _rms(x, w):
    return x * lax.rsqrt(jnp.mean(x * x, axis=-1, keepdims=True) + EPS) * w


def _norm_mod(x, nw, mod):
    return _rms(x, nw) * (1.0 + mod[1:2]) + mod[0:1]


def _split_bf16(x, pieces):
    out = []
    for _ in range(pieces):
        p = x.astype(BF16)
        out.append(p)
        x = x - p.astype(F32)
    return out


def _dot_f32ish(a, b_hi, b_lo):
    a_hi, a_lo = _split_bf16(a, 2)
    return (jnp.dot(a_hi, b_hi, preferred_element_type=F32) + jnp.dot(a_hi, b_lo, preferred_element_type=F32)
            + jnp.dot(a_lo, b_hi, preferred_element_type=F32))


def _dot_exact_left(a_bf16, b):
    return sum(jnp.dot(a_bf16, p, preferred_element_type=F32) for p in _split_bf16(b, 3))


def _nt_dot(a, b):
    return lax.dot_general(a, b, (((1,), (1,)), ((), ())), preferred_element_type=F32)


def _mod_kernel(ct_ref, w_ref, b_ref, o_ref):
    s = _silu(ct_ref[...])
    w = w_ref[...]
    for b in range(s.shape[1]):
        o_ref[b:b + 1, :] = jnp.sum(s[:, b:b + 1] * w, axis=0, keepdims=True) + b_ref[...]


def adaln_mod(c, w, b):
    bsz, d = c.shape
    n = w.shape[1]
    tn = n // 3
    out = pl.pallas_call(
        _mod_kernel,
        grid=(n // tn,),
        in_specs=[pl.BlockSpec((d, bsz), lambda j: (0, 0)),
                  pl.BlockSpec((d, tn), lambda j: (0, j)),
                  pl.BlockSpec((1, tn), lambda j: (0, j))],
        out_specs=pl.BlockSpec((bsz, tn), lambda j: (0, j)),
        out_shape=jax.ShapeDtypeStruct((bsz, n), F32),
        compiler_params=_params(("parallel",)),
        name="adaln_mod",
    )(c.T, w, b.reshape(1, n))
    return out.reshape(bsz, 3, d)


def _inproj_kernel(x_ref, mod_ref, nw_ref, wbig_ref, wsm_ref, fb_ref, tri_ref,
                   fq_ref, fk_ref, fv_ref, gq_ref, gk_ref, gv_ref, gg_ref, sm_ref, cum_ref,
                   carry_ref):
    i = pl.program_id(1)
    tm = x_ref.shape[1]
    h = _norm_mod(x_ref[0], nw_ref[...], mod_ref[0])
    hb = h.astype(BF16)
    col = 0
    for ref in (fq_ref, fk_ref, fv_ref, gq_ref, gk_ref, gv_ref, gg_ref):
        n = ref.shape[2]
        y = jnp.dot(hb, wbig_ref[:, col:col + n], preferred_element_type=F32)
        if ref is fq_ref:
            y = y * (LOG2E * HEAD_DIM ** -0.5)
        ref[0] = y.astype(ref.dtype)
        col += n
    small = _dot_f32ish(h, wsm_ref[0], wsm_ref[1])
    sm_ref[0] = small
    logf = _log_sigmoid(small + fb_ref[...])

    @pl.when(i == 0)
    def _():
        carry_ref[...] = jnp.zeros_like(carry_ref)

    cum = _dot_exact_left(tri_ref[...], logf) + carry_ref[...]
    cum_ref[0] = cum
    carry_ref[...] = cum[tm - 1:tm, :]


def layer0_inproj(x, mod, nw, w_in, fox_fb, tm=512):
    bsz, s, d = x.shape
    c0 = 3 * FOX_DIM
    c_ff = c0
    c_g = c_ff + FOX_HEADS
    c_lr = c_g + 2 * GLA_QK + 2 * GLA_V
    wbig = jnp.concatenate([w_in[:, :c0], w_in[:, c_g:c_lr]], axis=1).astype(BF16)
    wsm = jnp.zeros((d, LANES), F32)
    wsm = wsm.at[:, SMALL_FF0:SMALL_FF0 + FOX_HEADS].set(w_in[:, c_ff:c_g])
    wsm = wsm.at[:, SMALL_LR0:SMALL_LR0 + GLA_RANK].set(w_in[:, c_lr:])
    fb = jnp.zeros((1, LANES), F32).at[0, SMALL_FF0:SMALL_FF0 + FOX_HEADS].set(fox_fb)
    wsm = jnp.stack(_split_bf16(wsm, 2))
    tri = jnp.tril(jnp.ones((tm, tm), BF16))
    widths = (FOX_DIM, FOX_DIM, FOX_DIM, GLA_QK, GLA_QK, GLA_V, GLA_V)
    row = lambda b, i: (b, i, 0)
    const2 = lambda b, i: (0, 0)
    outs = pl.pallas_call(
        _inproj_kernel,
        grid=(bsz, s // tm),
        in_specs=[pl.BlockSpec((1, tm, d), row),
                  pl.BlockSpec((1, 3, d), lambda b, i: (b, 0, 0)),
                  pl.BlockSpec((1, d), const2),
                  pl.BlockSpec(wbig.shape, const2),
                  pl.BlockSpec(wsm.shape, lambda b, i: (0, 0, 0)),
                  pl.BlockSpec((1, LANES), const2),
                  pl.BlockSpec((tm, tm), const2)],
        out_specs=[pl.BlockSpec((1, tm, n), row) for n in widths]
                  + [pl.BlockSpec((1, tm, LANES), row)] * 2,
        out_shape=[jax.ShapeDtypeStruct((bsz, s, n), BF16) for n in widths]
                  + [jax.ShapeDtypeStruct((bsz, s, LANES), F32)] * 2,
        scratch_shapes=[pltpu.VMEM((1, LANES), F32)],
        compiler_params=_params(("parallel", "arbitrary")),
        name="layer0_inproj",
    )(x, mod, nw.reshape(1, d), wbig, wsm, fb, tri)
    return outs


def _fox_kernel(q_ref, k_ref, v_ref, c_ref, o_ref, kaug_ref, vaug_ref, m_ref, acc_ref, qaug_ref):
    pair = pl.program_id(1)
    i = pl.program_id(2)
    t = q_ref.shape[1]
    lane = lax.broadcasted_iota(jnp.int32, (1, LANES), 1)
    heads = (lane < HEAD_DIM, lane >= HEAD_DIM)
    spare = (HEAD_DIM, 0)

    @pl.when(i == 0)
    def _():
        k = k_ref[0].astype(F32)
        v = v_ref[0].astype(F32)
        cum = c_ref[0]
        for h in range(2):
            bias = jnp.sum(jnp.where(lane == 2 * pair + h, cum, 0.0), axis=1, keepdims=True) * (-LOG2E)
            ka = jnp.where(heads[h], k, 0.0)
            rest = bias
            for n in range(FOX_BIAS_PARTS):
                piece = rest.astype(BF16).astype(F32)
                rest = rest - piece
                ka = jnp.where(lane == spare[h] + n, piece, ka)
            kaug_ref[h] = ka.astype(BF16)
            vaug_ref[h, :, :LANES] = jnp.where(heads[h], v, 0.0).astype(BF16)
            vaug_ref[h, :, LANES:] = jnp.broadcast_to(jnp.where(heads[h], 1.0, 0.0), v.shape).astype(BF16)

    q = q_ref[0].astype(F32)
    for h in range(2):
        ones_here = jnp.where(lane >= spare[h], jnp.where(lane < spare[h] + FOX_BIAS_PARTS, 1.0, 0.0), 0.0)
        qaug_ref[h] = jnp.where(heads[h], q, ones_here).astype(BF16)
    m_ref[...] = jnp.full_like(m_ref, NEG_BIG)
    acc_ref[...] = jnp.zeros_like(acc_ref)

    def tile(j, diagonal):
        keys = pl.ds(pl.multiple_of(j * t, t), t)
        for r0 in range(0, t, FOX_ROWS):
            rows = slice(r0, r0 + FOX_ROWS)
            alphas = []
            upd = None
            for h in range(2):
                s = _nt_dot(qaug_ref[h, rows, :], kaug_ref[h, keys, :])
                if diagonal:
                    r = r0 + lax.broadcasted_iota(jnp.int32, (FOX_ROWS, t), 0)
                    c = lax.broadcasted_iota(jnp.int32, (FOX_ROWS, t), 1)
                    s = jnp.where(c <= r, s, NEG_BIG)
                m_prev = m_ref[h, rows, :]
                m_next = jnp.maximum(m_prev, jnp.max(s, axis=1, keepdims=True))
                m_ref[h, rows, :] = m_next
                p = jnp.exp2(s - jnp.concatenate([m_next] * (t // LANES), axis=1)).astype(BF16)
                alphas.append(jnp.exp2(m_prev - m_next))
                d = jnp.dot(p, vaug_ref[h, keys, :], preferred_element_type=F32)
                upd = d if upd is None else upd + d
            alpha = jnp.where(heads[0], alphas[0], alphas[1])
            acc_ref[rows, :] = jnp.concatenate([alpha, alpha], axis=1) * acc_ref[rows, :] + upd

    done = 0
    width = FOX_UNROLL
    while width >= 2:
        def group(jj, carry, width=width, done=done):
            for u in range(width):
                tile(done + width * jj + u, False)
            return carry

        trips = (i - done) // width
        lax.fori_loop(0, trips, group, 0)
        done = done + trips * width
        width //= 2

    @pl.when(done < i)
    def _():
        tile(i - 1, False)
        tile(i, True)

    @pl.when(done == i)
    def _():
        tile(i, True)

    o_ref[0] = (acc_ref[:, :LANES] / acc_ref[:, LANES:]).astype(o_ref.dtype)


def fox_attention(fq, fk, fv, cum, t=512):
    bsz, s, _ = fq.shape
    n_pair = FOX_HEADS // 2
    seq = lambda b, p, i: (b, 0, p)
    return pl.pallas_call(
        _fox_kernel,
        grid=(bsz, n_pair, s // t),
        in_specs=[pl.BlockSpec((1, t, LANES), lambda b, p, i: (b, i, p)),
                  pl.BlockSpec((1, s, LANES), seq),
                  pl.BlockSpec((1, s, LANES), seq),
                  pl.BlockSpec((1, s, LANES), lambda b, p, i: (b, 0, 0))],
        out_specs=pl.BlockSpec((1, t, LANES), lambda b, p, i: (b, i, p)),
        out_shape=jax.ShapeDtypeStruct((bsz, s, FOX_DIM), BF16),
        scratch_shapes=[pltpu.VMEM((2, s, LANES), BF16), pltpu.VMEM((2, s, 2 * LANES), BF16),
                        pltpu.VMEM((2, t, LANES), F32), pltpu.VMEM((t, 2 * LANES), F32),
                        pltpu.VMEM((2, t, LANES), BF16)],
        compiler_params=_params(("parallel", "parallel", "arbitrary")),
        name="fox_attention",
    )(fq, fk, fv, cum)


def _gla_pair_chunk(q, k, v, la, st, tri, eye, k_scr, b_scr):
    C, SUB = GLA_CHUNK, GLA_SUB
    nsub = C // SUB
    lane = lax.broadcasted_iota(jnp.int32, (1, LANES), 1)
    head_a = lane < GLA_DK
    b = _dot_exact_left(tri, la) * LOG2E
    b_last = b[C - 1:C, :]
    k_scr[...] = k
    b_scr[...] = b
    rowblk = lax.broadcasted_iota(jnp.int32, (C, 1), 0) // SUB

    refs = [b[m * SUB - 1:m * SUB, :] for m in range(1, nsub)]
    rsel = refs[-1]
    for m in range(nsub - 2, 0, -1):
        rsel = jnp.where(rowblk == m, refs[m - 1], rsel)
    qt = q * jnp.exp2(jnp.minimum(b - rsel, 0.0))
    zero = jnp.zeros_like(q)
    qs = jnp.concatenate([jnp.where(rowblk == m, qt, zero) for m in range(1, nsub)], axis=1)
    ks = jnp.concatenate(
        [jnp.where(rowblk < m, k * jnp.exp2(jnp.minimum(refs[m - 1] - b, 0.0)), zero)
         for m in range(1, nsub)], axis=1)
    lane3 = lax.broadcasted_iota(jnp.int32, (1, (nsub - 1) * LANES), 1)
    head_a3 = (lane3 & (LANES - 1)) < GLA_DK
    zero3 = jnp.zeros_like(ks)
    kstack = jnp.concatenate([jnp.where(head_a3, ks, zero3), jnp.where(head_a3, zero3, ks)], axis=0)
    a_off = _nt_dot(qs.astype(BF16), kstack.astype(BF16))

    row16 = lax.broadcasted_iota(jnp.int32, (SUB, 1), 0)
    blocks = []
    for i in range(nsub):
        qb = q[i * SUB:(i + 1) * SUB, :]
        bb = b[i * SUB:(i + 1) * SUB, :]
        d = jnp.zeros((SUB, LANES), F32)
        for sp in range(SUB):
            srow = i * SUB + sp
            e = qb * k_scr[srow:srow + 1, :] * jnp.exp2(jnp.minimum(bb - b_scr[srow:srow + 1, :], 0.0))
            da = jnp.sum(jnp.where(head_a, e, 0.0), axis=1, keepdims=True)
            db = jnp.sum(jnp.where(head_a, 0.0, e), axis=1, keepdims=True)
            live = row16 >= sp
            da = jnp.where(live, da, 0.0)
            db = jnp.where(live, db, 0.0)
            d = jnp.where(lane == srow, da, jnp.where(lane == GLA_DK + srow, db, d))
        blocks.append(d)
    a_pair = a_off + jnp.concatenate(blocks, axis=0)

    lane_v = lax.broadcasted_iota(jnp.int32, (1, 2 * GLA_DV), 1)
    first_v = lane_v < GLA_DV
    zv = jnp.zeros_like(v)
    vbd = jnp.concatenate([jnp.where(first_v, v, zv), jnp.where(first_v, zv, v)], axis=0)
    qh = q * jnp.exp2(b)
    o = (jnp.dot(a_pair.astype(BF16), vbd, preferred_element_type=F32)
         + _nt_dot(qh.astype(BF16), st.astype(BF16)))

    k_end = k * jnp.exp2(b_last - b)
    v_t = _nt_dot(eye, v).astype(BF16)
    inc = jnp.dot(v_t, k_end.astype(BF16), preferred_element_type=F32)
    row_v = lax.broadcasted_iota(jnp.int32, (2 * GLA_DV, 1), 0)
    same_head = (row_v // GLA_DV) == (lane // GLA_DK)
    st_new = st * jnp.exp2(b_last) + jnp.where(same_head, inc, 0.0)
    return o, st_new


def _gla_kernel(gq_ref, gk_ref, gv_ref, gg_ref, sm_ref, w2_ref, b2_ref, gn_ref, tri_ref, eye_ref,
                o_ref, st_ref, la_ref, k_scr, b_scr):
    i = pl.program_id(1)
    tm = gq_ref.shape[1]
    n_pair = GLA_HEADS // 2
    scale = GLA_DK ** -0.5

    @pl.when(i == 0)
    def _():
        st_ref[...] = jnp.zeros_like(st_ref)

    z = _dot_f32ish(sm_ref[0], w2_ref[0], w2_ref[1]) + b2_ref[...]
    la_ref[...] = _log_sigmoid(z) * (1.0 / GLA_TAU)
    gn = gn_ref[...]

    def chunk(c, carry):
        base = pl.multiple_of(c * GLA_CHUNK, GLA_CHUNK)
        rows = pl.ds(base, GLA_CHUNK)
        for hp in range(n_pair):
            ql = slice(hp * LANES, (hp + 1) * LANES)
            vl = slice(hp * 2 * GLA_DV, (hp + 1) * 2 * GLA_DV)
            q = gq_ref[0, rows, ql].astype(F32) * scale
            k = gk_ref[0, rows, ql].astype(F32)
            v = gv_ref[0, rows, vl]
            o, st_new = _gla_pair_chunk(q, k, v, la_ref[rows, ql], st_ref[hp], tri_ref[...],
                                        eye_ref[...], k_scr.at[hp], b_scr.at[hp])
            st_ref[hp] = st_new
            gate = _silu(gg_ref[0, rows, vl].astype(F32))
            halves = [_rms(o[:, h * GLA_DV:(h + 1) * GLA_DV], gn) for h in range(2)]
            o_ref[0, rows, vl] = (jnp.concatenate(halves, axis=1) * gate).astype(o_ref.dtype)
        return carry

    lax.fori_loop(0, tm // GLA_CHUNK, chunk, 0, unroll=4)


def gla_attention(gq, gk, gv, gg, small, w2, b2, gnorm, tm=512):
    bsz, s, _ = gq.shape
    w2p = jnp.zeros((LANES, GLA_QK), F32).at[SMALL_LR0:SMALL_LR0 + GLA_RANK].set(w2)
    w2p = jnp.stack(_split_bf16(w2p, 2))
    tri = jnp.tril(jnp.ones((GLA_CHUNK, GLA_CHUNK), BF16))
    eye = jnp.eye(2 * GLA_DV, dtype=BF16)
    row = lambda b, i: (b, i, 0)
    const2 = lambda b, i: (0, 0)
    return pl.pallas_call(
        _gla_kernel,
        grid=(bsz, s // tm),
        in_specs=[pl.BlockSpec((1, tm, GLA_QK), row), pl.BlockSpec((1, tm, GLA_QK), row),
                  pl.BlockSpec((1, tm, GLA_V), row), pl.BlockSpec((1, tm, GLA_V), row),
                  pl.BlockSpec((1, tm, LANES), row),
                  pl.BlockSpec(w2p.shape, lambda b, i: (0, 0, 0)), pl.BlockSpec((1, GLA_QK), const2),
                  pl.BlockSpec((1, GLA_DV), const2), pl.BlockSpec(tri.shape, const2),
                  pl.BlockSpec(eye.shape, const2)],
        out_specs=pl.BlockSpec((1, tm, GLA_V), row),
        out_shape=jax.ShapeDtypeStruct((bsz, s, GLA_V), BF16),
        scratch_shapes=[pltpu.VMEM((GLA_HEADS // 2, 2 * GLA_DV, LANES), F32),
                        pltpu.VMEM((tm, GLA_QK), F32),
                        pltpu.VMEM((GLA_HEADS // 2, GLA_CHUNK, LANES), F32),
                        pltpu.VMEM((GLA_HEADS // 2, GLA_CHUNK, LANES), F32)],
        compiler_params=_params(("parallel", "arbitrary")),
        name="gla_attention",
    )(gq, gk, gv, gg, small, w2p, b2.reshape(1, GLA_QK), gnorm.reshape(1, GLA_DV), tri, eye)


def _mix_ffn_kernel(x_ref, a_ref, b_ref, wo_ref, modm_ref, modf_ref, nw_ref, wg_ref, wu_ref, wd_ref,
                    o_ref, acc_ref, *, tf):
    na = a_ref.shape[2]
    y = (jnp.dot(a_ref[0], wo_ref[:na, :], preferred_element_type=F32)
         + jnp.dot(b_ref[0], wo_ref[na:, :], preferred_element_type=F32))
    x1 = x_ref[0] + modm_ref[0][2:3] * y
    hb = _norm_mod(x1, nw_ref[...], modf_ref[0]).astype(BF16)
    for j in range(wg_ref.shape[1] // tf):
        cols = slice(j * tf, (j + 1) * tf)
        act = (_silu(jnp.dot(hb, wg_ref[:, cols], preferred_element_type=F32))
               * jnp.dot(hb, wu_ref[:, cols], preferred_element_type=F32))
        part = jnp.dot(act.astype(BF16), wd_ref[cols, :], preferred_element_type=F32)
        if j == 0:
            acc_ref[...] = part
        else:
            acc_ref[...] += part
    o_ref[0] = x1 + modf_ref[0][2:3] * acc_ref[...]


def mixer_out_and_ffn(x, a, b, w_o, mod_mix, mod_ffn, nw, w_gate, w_up, w_down, tm=512, tf=256):
    bsz, s, d = x.shape
    row = lambda bb, i: (bb, i, 0)
    per_b = lambda bb, i: (bb, 0, 0)
    const2 = lambda bb, i: (0, 0)
    resident = lambda arr: pl.BlockSpec(arr.shape, const2, pipeline_mode=pl.Buffered(1))
    return pl.pallas_call(
        functools.partial(_mix_ffn_kernel, tf=tf),
        grid=(bsz, s // tm),
        in_specs=[pl.BlockSpec((1, tm, d), row),
                  pl.BlockSpec((1, tm, a.shape[2]), row),
                  pl.BlockSpec((1, tm, b.shape[2]), row),
                  resident(w_o),
                  pl.BlockSpec((1, 3, d), per_b),
                  pl.BlockSpec((1, 3, d), per_b),
                  pl.BlockSpec((1, d), const2),
                  resident(w_gate), resident(w_up), resident(w_down)],
        out_specs=pl.BlockSpec((1, tm, d), row),
        out_shape=jax.ShapeDtypeStruct(x.shape, F32),
        scratch_shapes=[pltpu.VMEM((tm, d), F32)],
        compiler_params=_params(("parallel", "parallel")),
        name="mixer_out_and_ffn",
    )(x, a, b, w_o, mod_mix, mod_ffn, nw.reshape(1, d), w_gate, w_up, w_down)


S5_SEG = 8


def _s5_disc_kernel(lre_ref, lim_ref, ldt_ref, bre_ref, bim_ref,
                    bbre_ref, bbim_ref, pre_ref, pim_ref):
    lre = lre_ref[...]
    lim = lim_ref[...]
    dt = jnp.exp(ldt_ref[...])
    mag = jnp.exp(lre * dt)
    a_re = mag * jnp.cos(lim * dt)
    a_im = mag * jnp.sin(lim * dt)
    den = lre * lre + lim * lim
    nr = a_re - 1.0
    ni = a_im
    f_re = (nr * lre + ni * lim) / den
    f_im = (ni * lre - nr * lim) / den
    bre = bre_ref[...]
    bim = bim_ref[...]
    bbre_ref[...] = f_re[:, None, :] * bre - f_im[:, None, :] * bim
    bbim_ref[...] = f_re[:, None, :] * bim + f_im[:, None, :] * bre
    n = (lax.broadcasted_iota(jnp.int32, pre_ref.shape, 0) + 1).astype(F32)
    mag_n = jnp.exp(n * (lre * dt)[None])
    ang_n = n * (lim * dt)[None]
    pre_ref[...] = mag_n * jnp.cos(ang_n)
    pim_ref[...] = mag_n * jnp.sin(ang_n)


def s5_discretise(lam_re, lam_im, log_dt, b_re, b_im, seg_len):
    g, p = lam_re.shape
    k = b_re.shape[2]
    return pl.pallas_call(
        _s5_disc_kernel,
        out_shape=[jax.ShapeDtypeStruct((g, k, p), F32)] * 2 + [jax.ShapeDtypeStruct((seg_len, g, p), F32)] * 2,
        name="s5_discretise",
    )(lam_re, lam_im, log_dt.reshape(g, 1), b_re.transpose(0, 2, 1), b_im.transpose(0, 2, 1))


def _s5_kernel(x_ref, mod_ref, nw_ref, perm_ref, permt_ref, win_ref, bbd_ref, cbd_ref, apow_ref, dsk_ref,
               wglu_ref, wo_ref, o_ref, carry_ref, bu_ref, xs_ref, xsb_ref):
    i = pl.program_id(1)
    tb = x_ref.shape[1]
    nblk = bu_ref.shape[0]
    nch = bu_ref.shape[2] // (2 * LANES)
    seg_len = tb // S5_SEG
    re_l = [slice(ch * LANES, (ch + 1) * LANES) for ch in range(nch)]
    im_l = [slice((nch + ch) * LANES, (nch + ch + 1) * LANES) for ch in range(nch)]

    @pl.when(i == 0)
    def _():
        carry_ref[...] = jnp.zeros_like(carry_ref)

    x = x_ref[0]
    h = _norm_mod(x, nw_ref[...], mod_ref[0]).astype(BF16)
    hp = jnp.dot(perm_ref[...], h, preferred_element_type=F32).astype(BF16)
    u = jnp.dot(hp, win_ref[...], preferred_element_type=F32)
    ub = u.astype(BF16)
    for jb in range(nblk):
        bu_ref[jb] = jnp.dot(ub[:, jb * LANES:(jb + 1) * LANES], bbd_ref[jb], preferred_element_type=F32)

    seg_row = lax.broadcasted_iota(jnp.int32, (S5_SEG, 1), 0)

    def channel_block(jb, blk_carry):
        def bcast(n, lanes):
            return jnp.broadcast_to(apow_ref[jb, n:n + 1, lanes], (S5_SEG, LANES))

        a1 = [(bcast(0, re_l[ch]), bcast(0, im_l[ch])) for ch in range(nch)]

        def step(t, state):
            rows = pl.ds(pl.multiple_of(t * S5_SEG, S5_SEG), S5_SEG)
            out = []
            for ch in range(nch):
                xr, xi = state[2 * ch], state[2 * ch + 1]
                ar, ai = a1[ch]
                nr = ar * xr - ai * xi + bu_ref[jb, rows, re_l[ch]]
                ni = ar * xi + ai * xr + bu_ref[jb, rows, im_l[ch]]
                xs_ref[jb, rows, re_l[ch]] = nr
                xs_ref[jb, rows, im_l[ch]] = ni
                out += [nr, ni]
            return tuple(out)

        zero = jnp.zeros((S5_SEG, LANES), F32)
        ends = lax.fori_loop(0, seg_len, step, (zero,) * (2 * nch), unroll=4)

        starts = []
        for ch in range(nch):
            alr = apow_ref[jb, seg_len - 1:seg_len, re_l[ch]]
            ali = apow_ref[jb, seg_len - 1:seg_len, im_l[ch]]
            er, ei = ends[2 * ch], ends[2 * ch + 1]
            zr = carry_ref[jb, :, re_l[ch]]
            zi = carry_ref[jb, :, im_l[ch]]
            tr = jnp.zeros((S5_SEG, LANES), F32)
            ti = jnp.zeros((S5_SEG, LANES), F32)
            for s in range(S5_SEG):
                tr = jnp.where(seg_row == s, zr, tr)
                ti = jnp.where(seg_row == s, zi, ti)
                zr, zi = (alr * zr - ali * zi + er[s:s + 1], alr * zi + ali * zr + ei[s:s + 1])
            carry_ref[jb, :, re_l[ch]] = zr
            carry_ref[jb, :, im_l[ch]] = zi
            starts.append((jnp.concatenate([tr, tr], axis=0), jnp.concatenate([ti, ti], axis=0)))

        for t2 in range(seg_len // 2):
            rows = slice(t2 * 2 * S5_SEG, (t2 + 1) * 2 * S5_SEG)
            for ch in range(nch):
                pr = jnp.concatenate([bcast(2 * t2, re_l[ch]), bcast(2 * t2 + 1, re_l[ch])], axis=0)
                pi = jnp.concatenate([bcast(2 * t2, im_l[ch]), bcast(2 * t2 + 1, im_l[ch])], axis=0)
                zr, zi = starts[ch]
                xsb_ref[jb, rows, re_l[ch]] = (xs_ref[jb, rows, re_l[ch]] + pr * zr - pi * zi).astype(BF16)
                xsb_ref[jb, rows, im_l[ch]] = (xs_ref[jb, rows, im_l[ch]] + pr * zi + pi * zr).astype(BF16)
        return blk_carry

    lax.fori_loop(0, nblk, channel_block, 0)

    ys = [jnp.dot(xsb_ref[jb], cbd_ref[jb], preferred_element_type=F32) for jb in range(nblk)]
    y = jnp.concatenate(ys, axis=1) + dsk_ref[...] * u
    y = _gelu_tanh(y)
    y = y * jax.nn.sigmoid(jnp.dot(y.astype(BF16), wglu_ref[...], preferred_element_type=F32))
    z = jnp.dot(y.astype(BF16), wo_ref[...], preferred_element_type=F32)
    z_hi = z.astype(BF16)
    z_lo = (z - z_hi.astype(F32)).astype(BF16)
    z = (jnp.dot(permt_ref[...], z_hi, preferred_element_type=F32)
         + jnp.dot(permt_ref[...], z_lo, preferred_element_type=F32))
    o_ref[0] = x + mod_ref[0][2:3] * z


def s5_layer(x, mod, nw, w_in, bb_re, bb_im, pow_re, pow_im, c_re, c_im, d_skip, w_glu, w_o, tb):
    bsz, s, d = x.shape
    g, k, p = bb_re.shape
    nblk = g // S5_GB
    seg_len = pow_re.shape[0]
    assert tb == seg_len * S5_SEG
    eye = jnp.eye(S5_GB, dtype=F32)

    def blockdiag_in(bb):
        t = jnp.einsum('jgkp,gh->jgkhp', bb.reshape(nblk, S5_GB, k, p), eye)
        return t.reshape(nblk, S5_GB * k, S5_GB * p)

    def blockdiag_out(c):
        t = jnp.einsum('jgkp,gh->jgphk', c.reshape(nblk, S5_GB, k, p), eye)
        return t.reshape(nblk, S5_GB * p, S5_GB * k)

    bbd = jnp.concatenate([blockdiag_in(bb_re), blockdiag_in(bb_im)], axis=2).astype(BF16)
    cbd = jnp.concatenate([blockdiag_out(c_re), -blockdiag_out(c_im)], axis=1).astype(BF16)
    half = S5_GB * p
    apow = jnp.concatenate([pow_re.reshape(seg_len, nblk, half), pow_im.reshape(seg_len, nblk, half)],
                           axis=2).transpose(1, 0, 2)
    new_row = jnp.arange(tb)
    old_row = (new_row % S5_SEG) * seg_len + new_row // S5_SEG
    perm = (old_row[:, None] == jnp.arange(tb)[None, :]).astype(BF16)
    row = lambda b, i: (b, i, 0)
    const2 = lambda b, i: (0, 0)
    const3 = lambda b, i: (0, 0, 0)
    return pl.pallas_call(
        _s5_kernel,
        grid=(bsz, s // tb),
        in_specs=[pl.BlockSpec((1, tb, d), row),
                  pl.BlockSpec((1, 3, d), lambda b, i: (b, 0, 0)),
                  pl.BlockSpec((1, d), const2),
                  pl.BlockSpec((tb, tb), const2), pl.BlockSpec((tb, tb), const2),
                  pl.BlockSpec((d, d), const2),
                  pl.BlockSpec(bbd.shape, const3),
                  pl.BlockSpec(cbd.shape, const3),
                  pl.BlockSpec(apow.shape, const3),
                  pl.BlockSpec((1, d), const2),
                  pl.BlockSpec((d, d), const2), pl.BlockSpec((d, d), const2)],
        out_specs=pl.BlockSpec((1, tb, d), row),
        out_shape=jax.ShapeDtypeStruct(x.shape, F32),
        scratch_shapes=[pltpu.VMEM((nblk, 1, 2 * half), F32),
                        pltpu.VMEM((nblk, tb, 2 * half), F32),
                        pltpu.VMEM((nblk, tb, 2 * half), F32),
                        pltpu.VMEM((nblk, tb, 2 * half), BF16)],
        compiler_params=_params(("parallel", "arbitrary")),
        name="s5_layer",
    )(x, mod, nw.reshape(1, d), perm, perm.T, w_in.astype(BF16), bbd, cbd, apow, d_skip.reshape(1, d),
      w_glu.astype(BF16), w_o.astype(BF16))


TOK_E, TOK_RANK, TOK_W = 0, 2, 4
W_PIECES = 3
MOE_TILE = 512
MOE_WIN = 192
ROW_ALIGN = 16
COMBINE_GROUP = 2
COMBINE_MAX_WIN = -(-((2 * MOE_TILE + N_EXPERTS * (ROW_ALIGN - 1)) // MOE_WIN + N_EXPERTS + 1)
                    // COMBINE_GROUP) * COMBINE_GROUP


def _router_kernel(x_ref, mod_ref, nw_ref, wr_ref, tri_ref, sel_ref,
                   hb_ref, tok_ref, tokt_ref, tokw_ref, before_ref, total_ref, carry_ref):
    blk = pl.program_id(0)
    h = _norm_mod(x_ref[0], nw_ref[...], mod_ref[0])
    hb_ref[0] = h.astype(BF16)
    logits = _dot_f32ish(h, wr_ref[0], wr_ref[1])
    lane = lax.broadcasted_iota(jnp.int32, logits.shape, 1).astype(F32)
    logits = jnp.where(lane < N_EXPERTS, logits, -jnp.inf)
    m1 = jnp.max(logits, axis=1, keepdims=True)
    i1 = jnp.min(jnp.where(logits == m1, lane, float(LANES)), axis=1, keepdims=True)
    rest = jnp.where(lane == i1, -jnp.inf, logits)
    m2 = jnp.max(rest, axis=1, keepdims=True)
    i2 = jnp.min(jnp.where(rest == m2, lane, float(LANES)), axis=1, keepdims=True)
    e2 = jnp.exp(m2 - m1)
    w1 = 1.0 / (1.0 + e2)
    w2 = e2 / (1.0 + e2)

    @pl.when(blk == 0)
    def _():
        carry_ref[...] = jnp.zeros_like(carry_ref)

    routed = jnp.where(lane == i1, 1.0, jnp.where(lane == i2, 1.0, 0.0))
    before = carry_ref[...]
    rank = jnp.dot(tri_ref[...], routed.astype(BF16), preferred_element_type=F32) + before
    r1 = jnp.sum(jnp.where(lane == i1, rank, 0.0), axis=1, keepdims=True)
    r2 = jnp.sum(jnp.where(lane == i2, rank, 0.0), axis=1, keepdims=True)
    fields = ((TOK_E, i1), (TOK_E + 1, i2), (TOK_RANK, r1), (TOK_RANK + 1, r2), (TOK_W, w1), (TOK_W + 1, w2))
    tok = jnp.zeros_like(logits)
    for ln, val in fields:
        tok = jnp.where(lane == ln, val, tok)
    tok_ref[0] = tok
    tokt_ref[0] = sum(_nt_dot(sel_ref[...], p) for p in _split_bf16(tok, 3))
    pieces = jnp.zeros_like(logits)
    for k, w in enumerate((w1, w2)):
        rest = w
        for n in range(W_PIECES):
            piece = rest.astype(BF16).astype(F32)
            rest = rest - piece
            pieces = jnp.where(lane == k * W_PIECES + n, piece, pieces)
    tokw_ref[0] = pieces.astype(BF16)
    before_ref[0] = before
    total = before + jnp.sum(routed, axis=0, keepdims=True)
    carry_ref[...] = total
    total_ref[...] = total


def moe_router(x, mod, nw, w_router):
    bsz, s, d = x.shape
    nb = MOE_TILE
    per_b = s // nb
    nblk = bsz * per_b
    wr = jnp.stack(_split_bf16(jnp.zeros((d, LANES), F32).at[:, :N_EXPERTS].set(w_router), 2))
    tri = jnp.tril(jnp.ones((nb, nb), BF16), k=-1)
    sel = jnp.eye(8, LANES, dtype=BF16)
    const2 = lambda i: (0, 0)
    return pl.pallas_call(
        _router_kernel,
        grid=(nblk,),
        in_specs=[pl.BlockSpec((1, nb, d), lambda i: (i // per_b, i % per_b, 0)),
                  pl.BlockSpec((1, 3, d), lambda i: (i // per_b, 0, 0)),
                  pl.BlockSpec((1, d), const2),
                  pl.BlockSpec((2, d, LANES), lambda i: (0, 0, 0)),
                  pl.BlockSpec((nb, nb), const2),
                  pl.BlockSpec((8, LANES), const2)],
        out_specs=[pl.BlockSpec((1, nb, d), lambda i: (i, 0, 0)),
                   pl.BlockSpec((1, nb, LANES), lambda i: (i, 0, 0)),
                   pl.BlockSpec((1, 8, nb), lambda i: (i, 0, 0)),
                   pl.BlockSpec((1, nb, LANES), lambda i: (i, 0, 0)),
                   pl.BlockSpec((1, 1, LANES), lambda i: (i, 0, 0)),
                   pl.BlockSpec((1, LANES), const2)],
        out_shape=[jax.ShapeDtypeStruct((nblk, nb, d), BF16),
                   jax.ShapeDtypeStruct((nblk, nb, LANES), F32),
                   jax.ShapeDtypeStruct((nblk, 8, nb), F32),
                   jax.ShapeDtypeStruct((nblk, nb, LANES), BF16),
                   jax.ShapeDtypeStruct((nblk, 1, LANES), F32),
                   jax.ShapeDtypeStruct((1, LANES), F32)],
        scratch_shapes=[pltpu.VMEM((1, LANES), F32)],
        compiler_params=_params(("arbitrary",)),
        name="moe_router",
    )(x, mod, nw.reshape(1, d), wr, tri, sel)


def _moe_plan(before, total, n_tiles):
    tm = MOE_TILE
    ntile = (total + tm - 1) // tm
    tile_end = jnp.cumsum(ntile)
    tile_start = tile_end - ntile
    n_valid = tile_end[-1]
    r = jnp.arange(n_tiles, dtype=jnp.int32)
    texp = jnp.minimum(jnp.sum((tile_end[None, :] <= r[:, None]).astype(jnp.int32), axis=1), N_EXPERTS - 1)
    r0 = (r - tile_start[texp]) * tm
    after = jnp.concatenate([before[1:], total[None, :]], axis=0)
    lo = before[:, texp].T
    hi = after[:, texp].T
    meets = (lo < (r0 + tm)[:, None]) & (hi > r0[:, None]) & (r < n_valid)[:, None]
    count = jnp.sum(meets, axis=1).astype(jnp.int32)
    first_blk = jnp.sum(jnp.cumsum(meets, axis=1) == 0, axis=1).astype(jnp.int32)
    start = jnp.clip(lo - r0[:, None], 0, tm)
    end = jnp.clip(hi - r0[:, None], 0, tm)
    first = (start // ROW_ALIGN) * ROW_ALIGN
    nwin = jnp.where(meets, (end - first + MOE_WIN - 1) // MOE_WIN, 0)
    flat = lambda a: a.reshape(-1).astype(jnp.int32)
    return dict(texp=texp.astype(jnp.int32), n_valid=n_valid.astype(jnp.int32).reshape(1),
                off=(tile_start * tm).astype(jnp.int32),
                g_blk0=jnp.where(count > 0, first_blk, 0), g_count=count,
                g_first=flat(first), g_nwin=flat(nwin))


def _slot_positions(tok_fields, k, off_ref, along_lanes):
    if along_lanes:
        e = tok_fields[TOK_E + k:TOK_E + k + 1, :]
        rank = tok_fields[TOK_RANK + k:TOK_RANK + k + 1, :]
    else:
        e = tok_fields[:, TOK_E + k:TOK_E + k + 1]
        rank = tok_fields[:, TOK_RANK + k:TOK_RANK + k + 1]
    pos = rank
    for ex in range(N_EXPERTS):
        pos = pos + jnp.where(e == float(ex), off_ref[ex].astype(F32), 0.0)
    return pos


def _window_rows(want, tile, tm, shape, axis):
    base = pl.multiple_of(jnp.minimum(want, tm - MOE_WIN), ROW_ALIGN)
    local = base + lax.broadcasted_iota(jnp.int32, shape, axis)
    rowid = jnp.where(local >= want, tile * tm + local, -1).astype(F32)
    return base, rowid


def _gather_kernel(blk0_ref, count_ref, first_ref, nwin_ref, off_ref, tokt_ref, tokw_ref, hb_ref,
                   hs_ref, ws_ref, acc_ref, wacc_ref):
    r = pl.program_id(0)
    tm = hs_ref.shape[0]
    nblk = hb_ref.shape[0]
    b0 = blk0_ref[r]
    lane = lax.broadcasted_iota(jnp.int32, (1, LANES), 1)
    acc_ref[...] = jnp.zeros_like(acc_ref)
    wacc_ref[...] = jnp.zeros_like(wacc_ref)

    def pair(k, carry):
        blk = b0 + k
        tokt = tokt_ref[blk]
        pos = [_slot_positions(tokt, s, off_ref, True) for s in range(2)]
        p = r * nblk + blk

        def window(i, wcarry):
            base, rowid = _window_rows(first_ref[p] + i * MOE_WIN, r, tm, (MOE_WIN, 1), 0)
            onehots = [jnp.where(pos[s] == rowid, 1.0, 0.0).astype(BF16) for s in range(2)]
            rows = pl.ds(base, MOE_WIN)
            acc_ref[rows, :] += jnp.dot(onehots[0] + onehots[1], hb_ref[blk], preferred_element_type=F32)
            picked = [jnp.dot(onehots[s], tokw_ref[blk], preferred_element_type=F32) for s in range(2)]
            wacc_ref[rows, :] += jnp.where(lane < W_PIECES, picked[0],
                                           jnp.where(lane < 2 * W_PIECES, picked[1], 0.0))
            return wcarry

        lax.fori_loop(0, nwin_ref[p], window, 0)
        return carry

    lax.fori_loop(0, count_ref[r], pair, 0)
    hs_ref[...] = acc_ref[...].astype(hs_ref.dtype)
    ws_ref[...] = wacc_ref[...]


def moe_gather(plan, tokt, tokw, hb, n_tiles):
    nblk, nb, d = hb.shape
    tm = MOE_TILE
    by_tile2 = lambda r, *_: (r, 0)
    resident = lambda arr: pl.BlockSpec(arr.shape, lambda r, *_: (0, 0, 0), pipeline_mode=pl.Buffered(1))
    return pl.pallas_call(
        _gather_kernel,
        grid_spec=pltpu.PrefetchScalarGridSpec(
            num_scalar_prefetch=5,
            grid=(n_tiles,),
            in_specs=[resident(tokt), resident(tokw), resident(hb)],
            out_specs=[pl.BlockSpec((tm, d), by_tile2), pl.BlockSpec((tm, LANES), by_tile2)],
            scratch_shapes=[pltpu.VMEM((tm, d), F32), pltpu.VMEM((tm, LANES), F32)]),
        out_shape=[jax.ShapeDtypeStruct((n_tiles * tm, d), BF16),
                   jax.ShapeDtypeStruct((n_tiles * tm, LANES), F32)],
        compiler_params=_params(("arbitrary",)),
        name="moe_gather",
    )(plan["g_blk0"], plan["g_count"], plan["g_first"], plan["g_nwin"], plan["off"], tokt, tokw, hb)


def _experts_kernel(texp_ref, nvalid_ref, hs_ref, ws_ref, wg_ref, wu_ref, wd_ref, ys_ref, acc_ref, *, tf):
    r = pl.program_id(0)
    valid = r < nvalid_ref[0]

    @pl.when(valid)
    def _():
        hb = hs_ref[...]
        for j in range(wg_ref.shape[2] // tf):
            cols = slice(j * tf, (j + 1) * tf)
            act = (_silu(jnp.dot(hb, wg_ref[0, :, cols], preferred_element_type=F32))
                   * jnp.dot(hb, wu_ref[0, :, cols], preferred_element_type=F32))
            part = jnp.dot(act.astype(BF16), wd_ref[0, cols, :], preferred_element_type=F32)
            if j == 0:
                acc_ref[...] = part
            else:
                acc_ref[...] += part
        w = jnp.sum(ws_ref[...], axis=1, keepdims=True)
        ys_ref[...] = (acc_ref[...] * w).astype(ys_ref.dtype)

    @pl.when(jnp.logical_not(valid))
    def _():
        ys_ref[...] = jnp.zeros_like(ys_ref)


def moe_experts(plan, hs, ws, w_gate, w_up, w_down, tf=256):
    rows, d = hs.shape
    tm = MOE_TILE
    f = w_gate.shape[2]
    rows2 = lambda r, te, nv: (r, 0)
    expert3 = lambda r, te, nv: (te[r], 0, 0)
    return pl.pallas_call(
        functools.partial(_experts_kernel, tf=tf),
        grid_spec=pltpu.PrefetchScalarGridSpec(
            num_scalar_prefetch=2,
            grid=(rows // tm,),
            in_specs=[pl.BlockSpec((tm, d), rows2),
                      pl.BlockSpec((tm, LANES), rows2),
                      pl.BlockSpec((1, d, f), expert3),
                      pl.BlockSpec((1, d, f), expert3),
                      pl.BlockSpec((1, f, d), expert3)],
            out_specs=pl.BlockSpec((tm, d), rows2),
            scratch_shapes=[pltpu.VMEM((tm, d), F32)]),
        out_shape=jax.ShapeDtypeStruct((rows, d), BF16),
        compiler_params=_params(("arbitrary",)),
        name="moe_experts",
    )(plan["texp"], plan["n_valid"], hs, ws, w_gate, w_up, w_down)


def _combine_windows(before, total, off, n_rows):
    nblk = before.shape[0]
    after = jnp.concatenate([before[1:], total[None, :]], axis=0)
    lo = off[None, :] + before
    hi = off[None, :] + after
    first = (lo // ROW_ALIGN) * ROW_ALIGN
    nwin = jnp.where(hi > lo, (hi - first + MOE_WIN - 1) // MOE_WIN, 0)
    ends = jnp.cumsum(nwin, axis=1)
    k = jnp.arange(COMBINE_MAX_WIN, dtype=jnp.int32)
    e_of = jnp.minimum(jnp.sum((ends[:, None, :] <= k[None, :, None]).astype(jnp.int32), axis=2), N_EXPERTS - 1)
    i_of = k[None, :] - jnp.take_along_axis(ends - nwin, e_of, axis=1)
    live = k[None, :] < ends[:, -1:]
    want = jnp.where(live, jnp.take_along_axis(first, e_of, axis=1) + i_of * MOE_WIN, 0)
    start = jnp.clip(want, 0, n_rows - MOE_WIN)
    limit = jnp.where(live, jnp.take_along_axis(hi, e_of, axis=1), 0)
    flat = lambda a: a.reshape(-1).astype(jnp.int32)
    return flat(start), flat(want), flat(limit), ends[:, -1].astype(jnp.int32)


def _combine_kernel(start_ref, want_ref, limit_ref, nwin_ref, off_ref, tok_ref, x_ref, mod_ref, fnw_ref,
                    ys_hbm, o_ref, buf_ref, sem_ref, acc_ref):
    b = pl.program_id(0)
    trips = (nwin_ref[b] + COMBINE_GROUP - 1) // COMBINE_GROUP
    tok = tok_ref[0]
    pos = [_slot_positions(tok, k, off_ref, False) for k in range(2)]

    def window_copies(t, slot):
        copies = []
        for g in range(COMBINE_GROUP):
            start = pl.multiple_of(start_ref[b * COMBINE_MAX_WIN + t * COMBINE_GROUP + g], ROW_ALIGN)
            copies.append(pltpu.make_async_copy(ys_hbm.at[pl.ds(start, MOE_WIN), :],
                                                buf_ref.at[slot, pl.ds(g * MOE_WIN, MOE_WIN), :],
                                                sem_ref.at[slot, g]))
        return copies

    @pl.when(trips > 0)
    def _():
        for c in window_copies(0, 0):
            c.start()

    acc_ref[...] = jnp.zeros_like(acc_ref)

    def trip(t, carry):
        slot = t & 1

        @pl.when(t + 1 < trips)
        def _():
            for c in window_copies(t + 1, 1 - slot):
                c.start()

        for c in window_copies(t, slot):
            c.wait()
        ids = []
        for g in range(COMBINE_GROUP):
            w = b * COMBINE_MAX_WIN + t * COMBINE_GROUP + g
            local = start_ref[w] + lax.broadcasted_iota(jnp.int32, (1, MOE_WIN), 1)
            ids.append(jnp.where(local >= want_ref[w], jnp.where(local < limit_ref[w], local, -1), -1))
        rowid = jnp.concatenate(ids, axis=1).astype(F32)
        onehot = (jnp.where(pos[0] == rowid, 1.0, 0.0) + jnp.where(pos[1] == rowid, 1.0, 0.0))
        acc_ref[...] += jnp.dot(onehot.astype(BF16), buf_ref[slot], preferred_element_type=F32)
        return carry

    lax.fori_loop(0, trips, trip, 0)
    o_ref[0] = _rms(x_ref[0] + mod_ref[0][2:3] * acc_ref[...], fnw_ref[...])


def moe_combine(plan, before, total, tok, ys, x, mod, final_nw):
    bsz, s, d = x.shape
    nb = MOE_TILE
    per_b = s // nb
    start, want, limit, nwin = _combine_windows(before, total, plan["off"], ys.shape[0])
    tok_blk = lambda b, *_: (b, 0, 0)
    x_blk = lambda b, *_: (b // per_b, b % per_b, 0)
    return pl.pallas_call(
        _combine_kernel,
        grid_spec=pltpu.PrefetchScalarGridSpec(
            num_scalar_prefetch=5,
            grid=(bsz * per_b,),
            in_specs=[pl.BlockSpec((1, nb, LANES), tok_blk),
                      pl.BlockSpec((1, nb, d), x_blk),
                      pl.BlockSpec((1, 3, d), lambda b, *_: (b // per_b, 0, 0)),
                      pl.BlockSpec((1, d), lambda b, *_: (0, 0)),
                      pl.BlockSpec(memory_space=pl.ANY)],
            out_specs=pl.BlockSpec((1, nb, d), x_blk),
            scratch_shapes=[pltpu.VMEM((2, COMBINE_GROUP * MOE_WIN, d), BF16),
                            pltpu.SemaphoreType.DMA((2, COMBINE_GROUP)),
                            pltpu.VMEM((nb, d), F32)]),
        out_shape=jax.ShapeDtypeStruct(x.shape, F32),
        compiler_params=_params(("arbitrary",)),
        name="moe_combine",
    )(start, want, limit, nwin, plan["off"], tok, x, mod, final_nw.reshape(1, d), ys)


def moe_layer(x, mod, nw, w_router, w_gate, w_up, w_down, final_nw):
    bsz, s, d = x.shape
    nblk = bsz * s // MOE_TILE
    n_tiles = 2 * nblk + N_EXPERTS
    hb, tok, tokt, tokw, before, total = moe_router(x, mod, nw, w_router)
    before = before[:, 0, :N_EXPERTS].astype(jnp.int32)
    total = total[0, :N_EXPERTS].astype(jnp.int32)
    plan = _moe_plan(before, total, n_tiles)
    hs, ws = moe_gather(plan, tokt, tokw, hb, n_tiles)
    ys = moe_experts(plan, hs, ws, w_gate, w_up, w_down)
    return moe_combine(plan, before, total, tok, ys, x, mod, final_nw)


def kernel(x, c, e_norm_mix, e_mod_mix_w, e_mod_mix_b, e_w_in, e_fox_fb, e_gla_w2, e_gla_b2, e_gla_norm, e_w_o, e_norm_ffn, e_mod_ffn_w, e_mod_ffn_b, e_ffn_gate, e_ffn_up, e_ffn_down, o_norm_mix, o_mod_mix_w, o_mod_mix_b, o_w_in, o_lam_re, o_lam_im, o_log_dt, o_b_re, o_b_im, o_c_re, o_c_im, o_d_skip, o_w_glu, o_w_o, o_norm_ffn, o_mod_ffn_w, o_mod_ffn_b, o_router, o_exp_gate, o_exp_up, o_exp_down, final_norm):
    bsz, s, d = x.shape

    mod = adaln_mod(c, e_mod_mix_w[0], e_mod_mix_b[0])
    fq, fk, fv, gq, gk, gv, gg, small, cum = layer0_inproj(x, mod, e_norm_mix[0], e_w_in[0], e_fox_fb[0])
    fox = fox_attention(fq, fk, fv, cum)
    gla = gla_attention(gq, gk, gv, gg, small, e_gla_w2[0], e_gla_b2[0], e_gla_norm[0])
    mod_ffn = adaln_mod(c, e_mod_ffn_w[0], e_mod_ffn_b[0])
    x = mixer_out_and_ffn(x, fox, gla, e_w_o[0].astype(BF16), mod, mod_ffn, e_norm_ffn[0],
                          e_ffn_gate[0].astype(BF16), e_ffn_up[0].astype(BF16), e_ffn_down[0].astype(BF16))

    mod = adaln_mod(c, o_mod_mix_w[0], o_mod_mix_b[0])
    s5_tb = 256
    bb_re, bb_im, pow_re, pow_im = s5_discretise(o_lam_re[0], o_lam_im[0], o_log_dt[0], o_b_re[0], o_b_im[0],
                                                 s5_tb // S5_SEG)
    x = s5_layer(x, mod, o_norm_mix[0], o_w_in[0], bb_re, bb_im, pow_re, pow_im, o_c_re[0], o_c_im[0],
                 o_d_skip[0], o_w_glu[0], o_w_o[0], s5_tb)
    mod = adaln_mod(c, o_mod_ffn_w[0], o_mod_ffn_b[0])
    return moe_layer(x, mod, o_norm_ffn[0], o_router[0], o_exp_gate[0].astype(BF16),
                     o_exp_up[0].astype(BF16), o_exp_down[0].astype(BF16), final_norm)
```

```python
import functools

import jax
import jax.numpy as jnp
from jax import lax
from jax.experimental import pallas as pl
from jax.experimental.pallas import tpu as pltpu

F32 = jnp.float32
BF16 = jnp.bfloat16
EPS = 1e-6
NEG_BIG = -1e30
LOG2E = 1.4426950408889634
FOX_BIAS_PARTS = 3
FOX_ROWS = 128
FOX_UNROLL = 4

LANES = 128
VMEM_LIMIT = 56 * 1024 * 1024

HEAD_DIM = 64
FOX_HEADS = 8
FOX_DIM = FOX_HEADS * HEAD_DIM
GLA_HEADS = 4
GLA_DK = 64
GLA_DV = 128
GLA_QK = GLA_HEADS * GLA_DK
GLA_V = GLA_HEADS * GLA_DV
GLA_RANK = 16
GLA_TAU = 16.0
GLA_CHUNK = 64
GLA_SUB = 8
S5_GB = 8
N_EXPERTS = 8
SMALL_FF0 = 0
SMALL_LR0 = 8


def _params(sem):
    return pltpu.CompilerParams(dimension_semantics=sem, vmem_limit_bytes=VMEM_LIMIT)


def _log_sigmoid(x):
    return jnp.minimum(x, 0.0) - jnp.log(1.0 + jnp.exp(-jnp.abs(x)))


def _silu(x):
    return x * jax.nn.sigmoid(x)


def _gelu_tanh(x):
    c = 0.7978845608028654
    return 0.5 * x * (1.0 + jnp.tanh(c * (x + 0.044715 * (x * x * x))))


def _rms(x, w):
    return x * lax.rsqrt(jnp.mean(x * x, axis=-1, keepdims=True) + EPS) * w


def _norm_mod(x, nw, mod):
    return _rms(x, nw) * (1.0 + mod[1:2]) + mod[0:1]


def _split_bf16(x, pieces):
    out = []
    for _ in range(pieces):
        p = x.astype(BF16)
        out.append(p)
        x = x - p.astype(F32)
    return out


def _dot_f32ish(a, b_hi, b_lo):
    a_hi, a_lo = _split_bf16(a, 2)
    return (jnp.dot(a_hi, b_hi, preferred_element_type=F32) + jnp.dot(a_hi, b_lo, preferred_element_type=F32)
            + jnp.dot(a_lo, b_hi, preferred_element_type=F32))


def _dot_exact_left(a_bf16, b):
    return sum(jnp.dot(a_bf16, p, preferred_element_type=F32) for p in _split_bf16(b, 3))


def _nt_dot(a, b):
    return lax.dot_general(a, b, (((1,), (1,)), ((), ())), preferred_element_type=F32)


def _mod_kernel(ct_ref, w_ref, b_ref, o_ref):
    s = _silu(ct_ref[...])
    w = w_ref[...]
    for b in range(s.shape[1]):
        o_ref[b:b + 1, :] = jnp.sum(s[:, b:b + 1] * w, axis=0, keepdims=True) + b_ref[...]


def adaln_mod(c, w, b):
    bsz, d = c.shape
    n = w.shape[1]
    tn = n // 3
    out = pl.pallas_call(
        _mod_kernel,
        grid=(n // tn,),
        in_specs=[pl.BlockSpec((d, bsz), lambda j: (0, 0)),
                  pl.BlockSpec((d, tn), lambda j: (0, j)),
                  pl.BlockSpec((1, tn), lambda j: (0, j))],
        out_specs=pl.BlockSpec((bsz, tn), lambda j: (0, j)),
        out_shape=jax.ShapeDtypeStruct((bsz, n), F32),
        compiler_params=_params(("parallel",)),
        name="adaln_mod",
    )(c.T, w, b.reshape(1, n))
    return out.reshape(bsz, 3, d)


def _inproj_kernel(x_ref, mod_ref, nw_ref, wbig_ref, wsm_ref, fb_ref, tri_ref,
                   fq_ref, fk_ref, fv_ref, gq_ref, gk_ref, gv_ref, gg_ref, sm_ref, cum_ref,
                   carry_ref):
    i = pl.program_id(1)
    tm = x_ref.shape[1]
    h = _norm_mod(x_ref[0], nw_ref[...], mod_ref[0])
    hb = h.astype(BF16)
    col = 0
    for ref in (fq_ref, fk_ref, fv_ref, gq_ref, gk_ref, gv_ref, gg_ref):
        n = ref.shape[2]
        y = jnp.dot(hb, wbig_ref[:, col:col + n], preferred_element_type=F32)
        if ref is fq_ref:
            y = y * (LOG2E * HEAD_DIM ** -0.5)
        ref[0] = y.astype(ref.dtype)
        col += n
    small = _dot_f32ish(h, wsm_ref[0], wsm_ref[1])
    sm_ref[0] = small
    logf = _log_sigmoid(small + fb_ref[...])

    @pl.when(i == 0)
    def _():
        carry_ref[...] = jnp.zeros_like(carry_ref)

    cum = _dot_exact_left(tri_ref[...], logf) + carry_ref[...]
    cum_ref[0] = cum
    carry_ref[...] = cum[tm - 1:tm, :]


def layer0_inproj(x, mod, nw, w_in, fox_fb, tm=512):
    bsz, s, d = x.shape
    c0 = 3 * FOX_DIM
    c_ff = c0
    c_g = c_ff + FOX_HEADS
    c_lr = c_g + 2 * GLA_QK + 2 * GLA_V
    wbig = jnp.concatenate([w_in[:, :c0], w_in[:, c_g:c_lr]], axis=1).astype(BF16)
    wsm = jnp.zeros((d, LANES), F32)
    wsm = wsm.at[:, SMALL_FF0:SMALL_FF0 + FOX_HEADS].set(w_in[:, c_ff:c_g])
    wsm = wsm.at[:, SMALL_LR0:SMALL_LR0 + GLA_RANK].set(w_in[:, c_lr:])
    fb = jnp.zeros((1, LANES), F32).at[0, SMALL_FF0:SMALL_FF0 + FOX_HEADS].set(fox_fb)
    wsm = jnp.stack(_split_bf16(wsm, 2))
    tri = jnp.tril(jnp.ones((tm, tm), BF16))
    widths = (FOX_DIM, FOX_DIM, FOX_DIM, GLA_QK, GLA_QK, GLA_V, GLA_V)
    row = lambda b, i: (b, i, 0)
    const2 = lambda b, i: (0, 0)
    outs = pl.pallas_call(
        _inproj_kernel,
        grid=(bsz, s // tm),
        in_specs=[pl.BlockSpec((1, tm, d), row),
                  pl.BlockSpec((1, 3, d), lambda b, i: (b, 0, 0)),
                  pl.BlockSpec((1, d), const2),
                  pl.BlockSpec(wbig.shape, const2),
                  pl.BlockSpec(wsm.shape, lambda b, i: (0, 0, 0)),
                  pl.BlockSpec((1, LANES), const2),
                  pl.BlockSpec((tm, tm), const2)],
        out_specs=[pl.BlockSpec((1, tm, n), row) for n in widths]
                  + [pl.BlockSpec((1, tm, LANES), row)] * 2,
        out_shape=[jax.ShapeDtypeStruct((bsz, s, n), BF16) for n in widths]
                  + [jax.ShapeDtypeStruct((bsz, s, LANES), F32)] * 2,
        scratch_shapes=[pltpu.VMEM((1, LANES), F32)],
        compiler_params=_params(("parallel", "arbitrary")),
        name="layer0_inproj",
    )(x, mod, nw.reshape(1, d), wbig, wsm, fb, tri)
    return outs


def _fox_kernel(q_ref, k_ref, v_ref, c_ref, o_ref, kaug_ref, vaug_ref, m_ref, acc_ref, qaug_ref):
    pair = pl.program_id(1)
    i = pl.program_id(2)
    t = q_ref.shape[1]
    lane = lax.broadcasted_iota(jnp.int32, (1, LANES), 1)
    heads = (lane < HEAD_DIM, lane >= HEAD_DIM)
    spare = (HEAD_DIM, 0)

    @pl.when(i == 0)
    def _():
        k = k_ref[0].astype(F32)
        v = v_ref[0].astype(F32)
        cum = c_ref[0]
        for h in range(2):
            bias = jnp.sum(jnp.where(lane == 2 * pair + h, cum, 0.0), axis=1, keepdims=True) * (-LOG2E)
            ka = jnp.where(heads[h], k, 0.0)
            rest = bias
            for n in range(FOX_BIAS_PARTS):
                piece = rest.astype(BF16).astype(F32)
                rest = rest - piece
                ka = jnp.where(lane == spare[h] + n, piece, ka)
            kaug_ref[h] = ka.astype(BF16)
            vaug_ref[h, :, :LANES] = jnp.where(heads[h], v, 0.0).astype(BF16)
            vaug_ref[h, :, LANES:] = jnp.broadcast_to(jnp.where(heads[h], 1.0, 0.0), v.shape).astype(BF16)

    q = q_ref[0].astype(F32)
    for h in range(2):
        ones_here = jnp.where(lane >= spare[h], jnp.where(lane < spare[h] + FOX_BIAS_PARTS, 1.0, 0.0), 0.0)
        qaug_ref[h] = jnp.where(heads[h], q, ones_here).astype(BF16)
    m_ref[...] = jnp.full_like(m_ref, NEG_BIG)
    acc_ref[...] = jnp.zeros_like(acc_ref)

    def tile(j, diagonal):
        keys = pl.ds(pl.multiple_of(j * t, t), t)
        for r0 in range(0, t, FOX_ROWS):
            rows = slice(r0, r0 + FOX_ROWS)
            alphas = []
            upd = None
            for h in range(2):
                s = _nt_dot(qaug_ref[h, rows, :], kaug_ref[h, keys, :])
                if diagonal:
                    r = r0 + lax.broadcasted_iota(jnp.int32, (FOX_ROWS, t), 0)
                    c = lax.broadcasted_iota(jnp.int32, (FOX_ROWS, t), 1)
                    s = jnp.where(c <= r, s, NEG_BIG)
                m_prev = m_ref[h, rows, :]
                m_next = jnp.maximum(m_prev, jnp.max(s, axis=1, keepdims=True))
                m_ref[h, rows, :] = m_next
                p = jnp.exp2(s - jnp.concatenate([m_next] * (t // LANES), axis=1)).astype(BF16)
                alphas.append(jnp.exp2(m_prev - m_next))
                d = jnp.dot(p, vaug_ref[h, keys, :], preferred_element_type=F32)
                upd = d if upd is None else upd + d
            alpha = jnp.where(heads[0], alphas[0], alphas[1])
            acc_ref[rows, :] = jnp.concatenate([alpha, alpha], axis=1) * acc_ref[rows, :] + upd

    done = 0
    width = FOX_UNROLL
    while width >= 2:
        def group(jj, carry, width=width, done=done):
            for u in range(width):
                tile(done + width * jj + u, False)
            return carry

        trips = (i - done) // width
        lax.fori_loop(0, trips, group, 0)
        done = done + trips * width
        width //= 2

    @pl.when(done < i)
    def _():
        tile(i - 1, False)
        tile(i, True)

    @pl.when(done == i)
    def _():
        tile(i, True)

    o_ref[0] = (acc_ref[:, :LANES] / acc_ref[:, LANES:]).astype(o_ref.dtype)


def fox_attention(fq, fk, fv, cum, t=512):
    bsz, s, _ = fq.shape
    n_pair = FOX_HEADS // 2
    seq = lambda b, p, i: (b, 0, p)
    return pl.pallas_call(
        _fox_kernel,
        grid=(bsz, n_pair, s // t),
        in_specs=[pl.BlockSpec((1, t, LANES), lambda b, p, i: (b, i, p)),
                  pl.BlockSpec((1, s, LANES), seq),
                  pl.BlockSpec((1, s, LANES), seq),
                  pl.BlockSpec((1, s, LANES), lambda b, p, i: (b, 0, 0))],
        out_specs=pl.BlockSpec((1, t, LANES), lambda b, p, i: (b, i, p)),
        out_shape=jax.ShapeDtypeStruct((bsz, s, FOX_DIM), BF16),
        scratch_shapes=[pltpu.VMEM((2, s, LANES), BF16), pltpu.VMEM((2, s, 2 * LANES), BF16),
                        pltpu.VMEM((2, t, LANES), F32), pltpu.VMEM((t, 2 * LANES), F32),
                        pltpu.VMEM((2, t, LANES), BF16)],
        compiler_params=_params(("parallel", "parallel", "arbitrary")),
        name="fox_attention",
    )(fq, fk, fv, cum)


def _gla_pair_chunk(q, k, v, la, st, tri, eye, k_scr, b_scr):
    C, SUB = GLA_CHUNK, GLA_SUB
    nsub = C // SUB
    lane = lax.broadcasted_iota(jnp.int32, (1, LANES), 1)
    head_a = lane < GLA_DK
    b = _dot_exact_left(tri, la) * LOG2E
    b_last = b[C - 1:C, :]
    k_scr[...] = k
    b_scr[...] = b
    rowblk = lax.broadcasted_iota(jnp.int32, (C, 1), 0) // SUB

    refs = [b[m * SUB - 1:m * SUB, :] for m in range(1, nsub)]
    rsel = refs[-1]
    for m in range(nsub - 2, 0, -1):
        rsel = jnp.where(rowblk == m, refs[m - 1], rsel)
    qt = q * jnp.exp2(jnp.minimum(b - rsel, 0.0))
    zero = jnp.zeros_like(q)
    qs = jnp.concatenate([jnp.where(rowblk == m, qt, zero) for m in range(1, nsub)], axis=1)
    ks = jnp.concatenate(
        [jnp.where(rowblk < m, k * jnp.exp2(jnp.minimum(refs[m - 1] - b, 0.0)), zero)
         for m in range(1, nsub)], axis=1)
    lane3 = lax.broadcasted_iota(jnp.int32, (1, (nsub - 1) * LANES), 1)
    head_a3 = (lane3 & (LANES - 1)) < GLA_DK
    zero3 = jnp.zeros_like(ks)
    kstack = jnp.concatenate([jnp.where(head_a3, ks, zero3), jnp.where(head_a3, zero3, ks)], axis=0)
    a_off = _nt_dot(qs.astype(BF16), kstack.astype(BF16))

    row16 = lax.broadcasted_iota(jnp.int32, (SUB, 1), 0)
    blocks = []
    for i in range(nsub):
        qb = q[i * SUB:(i + 1) * SUB, :]
        bb = b[i * SUB:(i + 1) * SUB, :]
        d = jnp.zeros((SUB, LANES), F32)
        for sp in range(SUB):
            srow = i * SUB + sp
            e = qb * k_scr[srow:srow + 1, :] * jnp.exp2(jnp.minimum(bb - b_scr[srow:srow + 1, :], 0.0))
            da = jnp.sum(jnp.where(head_a, e, 0.0), axis=1, keepdims=True)
            db = jnp.sum(jnp.where(head_a, 0.0, e), axis=1, keepdims=True)
            live = row16 >= sp
            da = jnp.where(live, da, 0.0)
            db = jnp.where(live, db, 0.0)
            d = jnp.where(lane == srow, da, jnp.where(lane == GLA_DK + srow, db, d))
        blocks.append(d)
    a_pair = a_off + jnp.concatenate(blocks, axis=0)

    lane_v = lax.broadcasted_iota(jnp.int32, (1, 2 * GLA_DV), 1)
    first_v = lane_v < GLA_DV
    zv = jnp.zeros_like(v)
    vbd = jnp.concatenate([jnp.where(first_v, v, zv), jnp.where(first_v, zv, v)], axis=0)
    qh = q * jnp.exp2(b)
    o = (jnp.dot(a_pair.astype(BF16), vbd, preferred_element_type=F32)
         + _nt_dot(qh.astype(BF16), st.astype(BF16)))

    k_end = k * jnp.exp2(b_last - b)
    v_t = _nt_dot(eye, v).astype(BF16)
    inc = jnp.dot(v_t, k_end.astype(BF16), preferred_element_type=F32)
    row_v = lax.broadcasted_iota(jnp.int32, (2 * GLA_DV, 1), 0)
    same_head = (row_v // GLA_DV) == (lane // GLA_DK)
    st_new = st * jnp.exp2(b_last) + jnp.where(same_head, inc, 0.0)
    return o, st_new


def _gla_kernel(gq_ref, gk_ref, gv_ref, gg_ref, sm_ref, w2_ref, b2_ref, gn_ref, tri_ref, eye_ref,
                o_ref, st_ref, la_ref, k_scr, b_scr):
    i = pl.program_id(1)
    tm = gq_ref.shape[1]
    n_pair = GLA_HEADS // 2
    scale = GLA_DK ** -0.5

    @pl.when(i == 0)
    def _():
        st_ref[...] = jnp.zeros_like(st_ref)

    z = _dot_f32ish(sm_ref[0], w2_ref[0], w2_ref[1]) + b2_ref[...]
    la_ref[...] = _log_sigmoid(z) * (1.0 / GLA_TAU)
    gn = gn_ref[...]

    def chunk(c, carry):
        base = pl.multiple_of(c * GLA_CHUNK, GLA_CHUNK)
        rows = pl.ds(base, GLA_CHUNK)
        for hp in range(n_pair):
            ql = slice(hp * LANES, (hp + 1) * LANES)
            vl = slice(hp * 2 * GLA_DV, (hp + 1) * 2 * GLA_DV)
            q = gq_ref[0, rows, ql].astype(F32) * scale
            k = gk_ref[0, rows, ql].astype(F32)
            v = gv_ref[0, rows, vl]
            o, st_new = _gla_pair_chunk(q, k, v, la_ref[rows, ql], st_ref[hp], tri_ref[...],
                                        eye_ref[...], k_scr.at[hp], b_scr.at[hp])
            st_ref[hp] = st_new
            gate = _silu(gg_ref[0, rows, vl].astype(F32))
            halves = [_rms(o[:, h * GLA_DV:(h + 1) * GLA_DV], gn) for h in range(2)]
            o_ref[0, rows, vl] = (jnp.concatenate(halves, axis=1) * gate).astype(o_ref.dtype)
        return carry

    lax.fori_loop(0, tm // GLA_CHUNK, chunk, 0, unroll=4)


def gla_attention(gq, gk, gv, gg, small, w2, b2, gnorm, tm=512):
    bsz, s, _ = gq.shape
    w2p = jnp.zeros((LANES, GLA_QK), F32).at[SMALL_LR0:SMALL_LR0 + GLA_RANK].set(w2)
    w2p = jnp.stack(_split_bf16(w2p, 2))
    tri = jnp.tril(jnp.ones((GLA_CHUNK, GLA_CHUNK), BF16))
    eye = jnp.eye(2 * GLA_DV, dtype=BF16)
    row = lambda b, i: (b, i, 0)
    const2 = lambda b, i: (0, 0)
    return pl.pallas_call(
        _gla_kernel,
        grid=(bsz, s // tm),
        in_specs=[pl.BlockSpec((1, tm, GLA_QK), row), pl.BlockSpec((1, tm, GLA_QK), row),
                  pl.BlockSpec((1, tm, GLA_V), row), pl.BlockSpec((1, tm, GLA_V), row),
                  pl.BlockSpec((1, tm, LANES), row),
                  pl.BlockSpec(w2p.shape, lambda b, i: (0, 0, 0)), pl.BlockSpec((1, GLA_QK), const2),
                  pl.BlockSpec((1, GLA_DV), const2), pl.BlockSpec(tri.shape, const2),
                  pl.BlockSpec(eye.shape, const2)],
        out_specs=pl.BlockSpec((1, tm, GLA_V), row),
        out_shape=jax.ShapeDtypeStruct((bsz, s, GLA_V), BF16),
        scratch_shapes=[pltpu.VMEM((GLA_HEADS // 2, 2 * GLA_DV, LANES), F32),
                        pltpu.VMEM((tm, GLA_QK), F32),
                        pltpu.VMEM((GLA_HEADS // 2, GLA_CHUNK, LANES), F32),
                        pltpu.VMEM((GLA_HEADS // 2, GLA_CHUNK, LANES), F32)],
        compiler_params=_params(("parallel", "arbitrary")),
        name="gla_attention",
    )(gq, gk, gv, gg, small, w2p, b2.reshape(1, GLA_QK), gnorm.reshape(1, GLA_DV), tri, eye)


def _mix_ffn_kernel(x_ref, a_ref, b_ref, wo_ref, modm_ref, modf_ref, nw_ref, wg_ref, wu_ref, wd_ref,
                    o_ref, acc_ref, *, tf):
    na = a_ref.shape[2]
    y = (jnp.dot(a_ref[0], wo_ref[:na, :], preferred_element_type=F32)
         + jnp.dot(b_ref[0], wo_ref[na:, :], preferred_element_type=F32))
    x1 = x_ref[0] + modm_ref[0][2:3] * y
    hb = _norm_mod(x1, nw_ref[...], modf_ref[0]).astype(BF16)
    for j in range(wg_ref.shape[1] // tf):
        cols = slice(j * tf, (j + 1) * tf)
        act = (_silu(jnp.dot(hb, wg_ref[:, cols], preferred_element_type=F32))
               * jnp.dot(hb, wu_ref[:, cols], preferred_element_type=F32))
        part = jnp.dot(act.astype(BF16), wd_ref[cols, :], preferred_element_type=F32)
        if j == 0:
            acc_ref[...] = part
        else:
            acc_ref[...] += part
    o_ref[0] = x1 + modf_ref[0][2:3] * acc_ref[...]


def mixer_out_and_ffn(x, a, b, w_o, mod_mix, mod_ffn, nw, w_gate, w_up, w_down, tm=512, tf=256):
    bsz, s, d = x.shape
    row = lambda bb, i: (bb, i, 0)
    per_b = lambda bb, i: (bb, 0, 0)
    const2 = lambda bb, i: (0, 0)
    resident = lambda arr: pl.BlockSpec(arr.shape, const2, pipeline_mode=pl.Buffered(1))
    return pl.pallas_call(
        functools.partial(_mix_ffn_kernel, tf=tf),
        grid=(bsz, s // tm),
        in_specs=[pl.BlockSpec((1, tm, d), row),
                  pl.BlockSpec((1, tm, a.shape[2]), row),
                  pl.BlockSpec((1, tm, b.shape[2]), row),
                  resident(w_o),
                  pl.BlockSpec((1, 3, d), per_b),
                  pl.BlockSpec((1, 3, d), per_b),
                  pl.BlockSpec((1, d), const2),
                  resident(w_gate), resident(w_up), resident(w_down)],
        out_specs=pl.BlockSpec((1, tm, d), row),
        out_shape=jax.ShapeDtypeStruct(x.shape, F32),
        scratch_shapes=[pltpu.VMEM((tm, d), F32)],
        compiler_params=_params(("parallel", "parallel")),
        name="mixer_out_and_ffn",
    )(x, a, b, w_o, mod_mix, mod_ffn, nw.reshape(1, d), w_gate, w_up, w_down)


S5_SEG = 8


def _s5_disc_kernel(lre_ref, lim_ref, ldt_ref, bre_ref, bim_ref,
                    bbre_ref, bbim_ref, pre_ref, pim_ref):
    lre = lre_ref[...]
    lim = lim_ref[...]
    dt = jnp.exp(ldt_ref[...])
    mag = jnp.exp(lre * dt)
    a_re = mag * jnp.cos(lim * dt)
    a_im = mag * jnp.sin(lim * dt)
    den = lre * lre + lim * lim
    nr = a_re - 1.0
    ni = a_im
    f_re = (nr * lre + ni * lim) / den
    f_im = (ni * lre - nr * lim) / den
    bre = bre_ref[...]
    bim = bim_ref[...]
    bbre_ref[...] = f_re[:, None, :] * bre - f_im[:, None, :] * bim
    bbim_ref[...] = f_re[:, None, :] * bim + f_im[:, None, :] * bre
    n = (lax.broadcasted_iota(jnp.int32, pre_ref.shape, 0) + 1).astype(F32)
    mag_n = jnp.exp(n * (lre * dt)[None])
    ang_n = n * (lim * dt)[None]
    pre_ref[...] = mag_n * jnp.cos(ang_n)
    pim_ref[...] = mag_n * jnp.sin(ang_n)


def s5_discretise(lam_re, lam_im, log_dt, b_re, b_im, seg_len):
    g, p = lam_re.shape
    k = b_re.shape[2]
    return pl.pallas_call(
        _s5_disc_kernel,
        out_shape=[jax.ShapeDtypeStruct((g, k, p), F32)] * 2 + [jax.ShapeDtypeStruct((seg_len, g, p), F32)] * 2,
        name="s5_discretise",
    )(lam_re, lam_im, log_dt.reshape(g, 1), b_re.transpose(0, 2, 1), b_im.transpose(0, 2, 1))


def _s5_kernel(x_ref, mod_ref, nw_ref, perm_ref, permt_ref, win_ref, bbd_ref, cbd_ref, apow_ref, dsk_ref,
               wglu_ref, wo_ref, o_ref, carry_ref, bu_ref, xs_ref, xsb_ref):
    i = pl.program_id(1)
    tb = x_ref.shape[1]
    nblk = bu_ref.shape[0]
    nch = bu_ref.shape[2] // (2 * LANES)
    seg_len = tb // S5_SEG
    re_l = [slice(ch * LANES, (ch + 1) * LANES) for ch in range(nch)]
    im_l = [slice((nch + ch) * LANES, (nch + ch + 1) * LANES) for ch in range(nch)]

    @pl.when(i == 0)
    def _():
        carry_ref[...] = jnp.zeros_like(carry_ref)

    x = x_ref[0]
    h = _norm_mod(x, nw_ref[...], mod_ref[0]).astype(BF16)
    hp = jnp.dot(perm_ref[...], h, preferred_element_type=F32).astype(BF16)
    u = jnp.dot(hp, win_ref[...], preferred_element_type=F32)
    ub = u.astype(BF16)
    for jb in range(nblk):
        bu_ref[jb] = jnp.dot(ub[:, jb * LANES:(jb + 1) * LANES], bbd_ref[jb], preferred_element_type=F32)

    seg_row = lax.broadcasted_iota(jnp.int32, (S5_SEG, 1), 0)

    def channel_block(jb, blk_carry):
        def bcast(n, lanes):
            return jnp.broadcast_to(apow_ref[jb, n:n + 1, lanes], (S5_SEG, LANES))

        a1 = [(bcast(0, re_l[ch]), bcast(0, im_l[ch])) for ch in range(nch)]

        def step(t, state):
            rows = pl.ds(pl.multiple_of(t * S5_SEG, S5_SEG), S5_SEG)
            out = []
            for ch in range(nch):
                xr, xi = state[2 * ch], state[2 * ch + 1]
                ar, ai = a1[ch]
                nr = ar * xr - ai * xi + bu_ref[jb, rows, re_l[ch]]
                ni = ar * xi + ai * xr + bu_ref[jb, rows, im_l[ch]]
                xs_ref[jb, rows, re_l[ch]] = nr
                xs_ref[jb, rows, im_l[ch]] = ni
                out += [nr, ni]
            return tuple(out)

        zero = jnp.zeros((S5_SEG, LANES), F32)
        ends = lax.fori_loop(0, seg_len, step, (zero,) * (2 * nch), unroll=4)

        starts = []
        for ch in range(nch):
            alr = apow_ref[jb, seg_len - 1:seg_len, re_l[ch]]
            ali = apow_ref[jb, seg_len - 1:seg_len, im_l[ch]]
            er, ei = ends[2 * ch], ends[2 * ch + 1]
            zr = carry_ref[jb, :, re_l[ch]]
            zi = carry_ref[jb, :, im_l[ch]]
            tr = jnp.zeros((S5_SEG, LANES), F32)
            ti = jnp.zeros((S5_SEG, LANES), F32)
            for s in range(S5_SEG):
                tr = jnp.where(seg_row == s, zr, tr)
                ti = jnp.where(seg_row == s, zi, ti)
                zr, zi = (alr * zr - ali * zi + er[s:s + 1], alr * zi + ali * zr + ei[s:s + 1])
            carry_ref[jb, :, re_l[ch]] = zr
            carry_ref[jb, :, im_l[ch]] = zi
            starts.append((jnp.concatenate([tr, tr], axis=0), jnp.concatenate([ti, ti], axis=0)))

        for t2 in range(seg_len // 2):
            rows = slice(t2 * 2 * S5_SEG, (t2 + 1) * 2 * S5_SEG)
            for ch in range(nch):
                pr = jnp.concatenate([bcast(2 * t2, re_l[ch]), bcast(2 * t2 + 1, re_l[ch])], axis=0)
                pi = jnp.concatenate([bcast(2 * t2, im_l[ch]), bcast(2 * t2 + 1, im_l[ch])], axis=0)
                zr, zi = starts[ch]
                xsb_ref[jb, rows, re_l[ch]] = (xs_ref[jb, rows, re_l[ch]] + pr * zr - pi * zi).astype(BF16)
                xsb_ref[jb, rows, im_l[ch]] = (xs_ref[jb, rows, im_l[ch]] + pr * zi + pi * zr).astype(BF16)
        return blk_carry

    lax.fori_loop(0, nblk, channel_block, 0)

    ys = [jnp.dot(xsb_ref[jb], cbd_ref[jb], preferred_element_type=F32) for jb in range(nblk)]
    y = jnp.concatenate(ys, axis=1) + dsk_ref[...] * u
    y = _gelu_tanh(y)
    y = y * jax.nn.sigmoid(jnp.dot(y.astype(BF16), wglu_ref[...], preferred_element_type=F32))
    z = jnp.dot(y.astype(BF16), wo_ref[...], preferred_element_type=F32)
    z_hi = z.astype(BF16)
    z_lo = (z - z_hi.astype(F32)).astype(BF16)
    z = (jnp.dot(permt_ref[...], z_hi, preferred_element_type=F32)
         + jnp.dot(permt_ref[...], z_lo, preferred_element_type=F32))
    o_ref[0] = x + mod_ref[0][2:3] * z


def s5_layer(x, mod, nw, w_in, bb_re, bb_im, pow_re, pow_im, c_re, c_im, d_skip, w_glu, w_o, tb):
    bsz, s, d = x.shape
    g, k, p = bb_re.shape
    nblk = g // S5_GB
    seg_len = pow_re.shape[0]
    assert tb == seg_len * S5_SEG
    eye = jnp.eye(S5_GB, dtype=F32)

    def blockdiag_in(bb):
        t = jnp.einsum('jgkp,gh->jgkhp', bb.reshape(nblk, S5_GB, k, p), eye)
        return t.reshape(nblk, S5_GB * k, S5_GB * p)

    def blockdiag_out(c):
        t = jnp.einsum('jgkp,gh->jgphk', c.reshape(nblk, S5_GB, k, p), eye)
        return t.reshape(nblk, S5_GB * p, S5_GB * k)

    bbd = jnp.concatenate([blockdiag_in(bb_re), blockdiag_in(bb_im)], axis=2).astype(BF16)
    cbd = jnp.concatenate([blockdiag_out(c_re), -blockdiag_out(c_im)], axis=1).astype(BF16)
    half = S5_GB * p
    apow = jnp.concatenate([pow_re.reshape(seg_len, nblk, half), pow_im.reshape(seg_len, nblk, half)],
                           axis=2).transpose(1, 0, 2)
    new_row = jnp.arange(tb)
    old_row = (new_row % S5_SEG) * seg_len + new_row // S5_SEG
    perm = (old_row[:, None] == jnp.arange(tb)[None, :]).astype(BF16)
    row = lambda b, i: (b, i, 0)
    const2 = lambda b, i: (0, 0)
    const3 = lambda b, i: (0, 0, 0)
    return pl.pallas_call(
        _s5_kernel,
        grid=(bsz, s // tb),
        in_specs=[pl.BlockSpec((1, tb, d), row),
                  pl.BlockSpec((1, 3, d), lambda b, i: (b, 0, 0)),
                  pl.BlockSpec((1, d), const2),
                  pl.BlockSpec((tb, tb), const2), pl.BlockSpec((tb, tb), const2),
                  pl.BlockSpec((d, d), const2),
                  pl.BlockSpec(bbd.shape, const3),
                  pl.BlockSpec(cbd.shape, const3),
                  pl.BlockSpec(apow.shape, const3),
                  pl.BlockSpec((1, d), const2),
                  pl.BlockSpec((d, d), const2), pl.BlockSpec((d, d), const2)],
        out_specs=pl.BlockSpec((1, tb, d), row),
        out_shape=jax.ShapeDtypeStruct(x.shape, F32),
        scratch_shapes=[pltpu.VMEM((nblk, 1, 2 * half), F32),
                        pltpu.VMEM((nblk, tb, 2 * half), F32),
                        pltpu.VMEM((nblk, tb, 2 * half), F32),
                        pltpu.VMEM((nblk, tb, 2 * half), BF16)],
        compiler_params=_params(("parallel", "arbitrary")),
        name="s5_layer",
    )(x, mod, nw.reshape(1, d), perm, perm.T, w_in.astype(BF16), bbd, cbd, apow, d_skip.reshape(1, d),
      w_glu.astype(BF16), w_o.astype(BF16))


TOK_E, TOK_RANK, TOK_W = 0, 2, 4
W_PIECES = 3
MOE_TILE = 512
MOE_WIN = 192
ROW_ALIGN = 16
COMBINE_GROUP = 2
COMBINE_MAX_WIN = -(-((2 * MOE_TILE + N_EXPERTS * (ROW_ALIGN - 1)) // MOE_WIN + N_EXPERTS + 1)
                    // COMBINE_GROUP) * COMBINE_GROUP


def _router_kernel(x_ref, mod_ref, nw_ref, wr_ref, tri_ref, sel_ref,
                   hb_ref, tok_ref, tokt_ref, tokw_ref, before_ref, total_ref, carry_ref):
    blk = pl.program_id(0)
    h = _norm_mod(x_ref[0], nw_ref[...], mod_ref[0])
    hb_ref[0] = h.astype(BF16)
    logits = _dot_f32ish(h, wr_ref[0], wr_ref[1])
    lane = lax.broadcasted_iota(jnp.int32, logits.shape, 1).astype(F32)
    logits = jnp.where(lane < N_EXPERTS, logits, -jnp.inf)
    m1 = jnp.max(logits, axis=1, keepdims=True)
    i1 = jnp.min(jnp.where(logits == m1, lane, float(LANES)), axis=1, keepdims=True)
    rest = jnp.where(lane == i1, -jnp.inf, logits)
    m2 = jnp.max(rest, axis=1, keepdims=True)
    i2 = jnp.min(jnp.where(rest == m2, lane, float(LANES)), axis=1, keepdims=True)
    e2 = jnp.exp(m2 - m1)
    w1 = 1.0 / (1.0 + e2)
    w2 = e2 / (1.0 + e2)

    @pl.when(blk == 0)
    def _():
        carry_ref[...] = jnp.zeros_like(carry_ref)

    routed = jnp.where(lane == i1, 1.0, jnp.where(lane == i2, 1.0, 0.0))
    before = carry_ref[...]
    rank = jnp.dot(tri_ref[...], routed.astype(BF16), preferred_element_type=F32) + before
    r1 = jnp.sum(jnp.where(lane == i1, rank, 0.0), axis=1, keepdims=True)
    r2 = jnp.sum(jnp.where(lane == i2, rank, 0.0), axis=1, keepdims=True)
    fields = ((TOK_E, i1), (TOK_E + 1, i2), (TOK_RANK, r1), (TOK_RANK + 1, r2), (TOK_W, w1), (TOK_W + 1, w2))
    tok = jnp.zeros_like(logits)
    for ln, val in fields:
        tok = jnp.where(lane == ln, val, tok)
    tok_ref[0] = tok
    tokt_ref[0] = sum(_nt_dot(sel_ref[...], p) for p in _split_bf16(tok, 3))
    pieces = jnp.zeros_like(logits)
    for k, w in enumerate((w1, w2)):
        rest = w
        for n in range(W_PIECES):
            piece = rest.astype(BF16).astype(F32)
            rest = rest - piece
            pieces = jnp.where(lane == k * W_PIECES + n, piece, pieces)
    tokw_ref[0] = pieces.astype(BF16)
    before_ref[0] = before
    total = before + jnp.sum(routed, axis=0, keepdims=True)
    carry_ref[...] = total
    total_ref[...] = total


def moe_router(x, mod, nw, w_router):
    bsz, s, d = x.shape
    nb = MOE_TILE
    per_b = s // nb
    nblk = bsz * per_b
    wr = jnp.stack(_split_bf16(jnp.zeros((d, LANES), F32).at[:, :N_EXPERTS].set(w_router), 2))
    tri = jnp.tril(jnp.ones((nb, nb), BF16), k=-1)
    sel = jnp.eye(8, LANES, dtype=BF16)
    const2 = lambda i: (0, 0)
    return pl.pallas_call(
        _router_kernel,
        grid=(nblk,),
        in_specs=[pl.BlockSpec((1, nb, d), lambda i: (i // per_b, i % per_b, 0)),
                  pl.BlockSpec((1, 3, d), lambda i: (i // per_b, 0, 0)),
                  pl.BlockSpec((1, d), const2),
                  pl.BlockSpec((2, d, LANES), lambda i: (0, 0, 0)),
                  pl.BlockSpec((nb, nb), const2),
                  pl.BlockSpec((8, LANES), const2)],
        out_specs=[pl.BlockSpec((1, nb, d), lambda i: (i, 0, 0)),
                   pl.BlockSpec((1, nb, LANES), lambda i: (i, 0, 0)),
                   pl.BlockSpec((1, 8, nb), lambda i: (i, 0, 0)),
                   pl.BlockSpec((1, nb, LANES), lambda i: (i, 0, 0)),
                   pl.BlockSpec((1, 1, LANES), lambda i: (i, 0, 0)),
                   pl.BlockSpec((1, LANES), const2)],
        out_shape=[jax.ShapeDtypeStruct((nblk, nb, d), BF16),
                   jax.ShapeDtypeStruct((nblk, nb, LANES), F32),
                   jax.ShapeDtypeStruct((nblk, 8, nb), F32),
                   jax.ShapeDtypeStruct((nblk, nb, LANES), BF16),
                   jax.ShapeDtypeStruct((nblk, 1, LANES), F32),
                   jax.ShapeDtypeStruct((1, LANES), F32)],
        scratch_shapes=[pltpu.VMEM((1, LANES), F32)],
        compiler_params=_params(("arbitrary",)),
        name="moe_router",
    )(x, mod, nw.reshape(1, d), wr, tri, sel)


def _moe_plan(before, total, n_tiles):
    tm = MOE_TILE
    ntile = (total + tm - 1) // tm
    tile_end = jnp.cumsum(ntile)
    tile_start = tile_end - ntile
    n_valid = tile_end[-1]
    r = jnp.arange(n_tiles, dtype=jnp.int32)
    texp = jnp.minimum(jnp.sum((tile_end[None, :] <= r[:, None]).astype(jnp.int32), axis=1), N_EXPERTS - 1)
    r0 = (r - tile_start[texp]) * tm
    after = jnp.concatenate([before[1:], total[None, :]], axis=0)
    lo = before[:, texp].T
    hi = after[:, texp].T
    meets = (lo < (r0 + tm)[:, None]) & (hi > r0[:, None]) & (r < n_valid)[:, None]
    count = jnp.sum(meets, axis=1).astype(jnp.int32)
    first_blk = jnp.sum(jnp.cumsum(meets, axis=1) == 0, axis=1).astype(jnp.int32)
    start = jnp.clip(lo - r0[:, None], 0, tm)
    end = jnp.clip(hi - r0[:, None], 0, tm)
    first = (start // ROW_ALIGN) * ROW_ALIGN
    nwin = jnp.where(meets, (end - first + MOE_WIN - 1) // MOE_WIN, 0)
    flat = lambda a: a.reshape(-1).astype(jnp.int32)
    return dict(texp=texp.astype(jnp.int32), n_valid=n_valid.astype(jnp.int32).reshape(1),
                off=(tile_start * tm).astype(jnp.int32),
                g_blk0=jnp.where(count > 0, first_blk, 0), g_count=count,
                g_first=flat(first), g_nwin=flat(nwin))


def _slot_positions(tok_fields, k, off_ref, along_lanes):
    if along_lanes:
        e = tok_fields[TOK_E + k:TOK_E + k + 1, :]
        rank = tok_fields[TOK_RANK + k:TOK_RANK + k + 1, :]
    else:
        e = tok_fields[:, TOK_E + k:TOK_E + k + 1]
        rank = tok_fields[:, TOK_RANK + k:TOK_RANK + k + 1]
    pos = rank
    for ex in range(N_EXPERTS):
        pos = pos + jnp.where(e == float(ex), off_ref[ex].astype(F32), 0.0)
    return pos


def _window_rows(want, tile, tm, shape, axis):
    base = pl.multiple_of(jnp.minimum(want, tm - MOE_WIN), ROW_ALIGN)
    local = base + lax.broadcasted_iota(jnp.int32, shape, axis)
    rowid = jnp.where(local >= want, tile * tm + local, -1).astype(F32)
    return base, rowid


def _gather_kernel(blk0_ref, count_ref, first_ref, nwin_ref, off_ref, tokt_ref, tokw_ref, hb_ref,
                   hs_ref, ws_ref, acc_ref, wacc_ref):
    r = pl.program_id(0)
    tm = hs_ref.shape[0]
    nblk = hb_ref.shape[0]
    b0 = blk0_ref[r]
    lane = lax.broadcasted_iota(jnp.int32, (1, LANES), 1)
    acc_ref[...] = jnp.zeros_like(acc_ref)
    wacc_ref[...] = jnp.zeros_like(wacc_ref)

    def pair(k, carry):
        blk = b0 + k
        tokt = tokt_ref[blk]
        pos = [_slot_positions(tokt, s, off_ref, True) for s in range(2)]
        p = r * nblk + blk

        def window(i, wcarry):
            base, rowid = _window_rows(first_ref[p] + i * MOE_WIN, r, tm, (MOE_WIN, 1), 0)
            onehots = [jnp.where(pos[s] == rowid, 1.0, 0.0).astype(BF16) for s in range(2)]
            rows = pl.ds(base, MOE_WIN)
            acc_ref[rows, :] += jnp.dot(onehots[0] + onehots[1], hb_ref[blk], preferred_element_type=F32)
            picked = [jnp.dot(onehots[s], tokw_ref[blk], preferred_element_type=F32) for s in range(2)]
            wacc_ref[rows, :] += jnp.where(lane < W_PIECES, picked[0],
                                           jnp.where(lane < 2 * W_PIECES, picked[1], 0.0))
            return wcarry

        lax.fori_loop(0, nwin_ref[p], window, 0)
        return carry

    lax.fori_loop(0, count_ref[r], pair, 0)
    hs_ref[...] = acc_ref[...].astype(hs_ref.dtype)
    ws_ref[...] = wacc_ref[...]


def moe_gather(plan, tokt, tokw, hb, n_tiles):
    nblk, nb, d = hb.shape
    tm = MOE_TILE
    by_tile2 = lambda r, *_: (r, 0)
    resident = lambda arr: pl.BlockSpec(arr.shape, lambda r, *_: (0, 0, 0), pipeline_mode=pl.Buffered(1))
    return pl.pallas_call(
        _gather_kernel,
        grid_spec=pltpu.PrefetchScalarGridSpec(
            num_scalar_prefetch=5,
            grid=(n_tiles,),
            in_specs=[resident(tokt), resident(tokw), resident(hb)],
            out_specs=[pl.BlockSpec((tm, d), by_tile2), pl.BlockSpec((tm, LANES), by_tile2)],
            scratch_shapes=[pltpu.VMEM((tm, d), F32), pltpu.VMEM((tm, LANES), F32)]),
        out_shape=[jax.ShapeDtypeStruct((n_tiles * tm, d), BF16),
                   jax.ShapeDtypeStruct((n_tiles * tm, LANES), F32)],
        compiler_params=_params(("arbitrary",)),
        name="moe_gather",
    )(plan["g_blk0"], plan["g_count"], plan["g_first"], plan["g_nwin"], plan["off"], tokt, tokw, hb)


def _experts_kernel(texp_ref, nvalid_ref, hs_ref, ws_ref, wg_hbm, wu_hbm, wd_hbm, ys_ref,
                    wg_ref, wu_ref, wd_ref, stage_in_ref, stage_out_ref, sem_ref, acc_ref, *, tf):
    r = pl.program_id(0)
    valid = r < nvalid_ref[0]
    e = texp_ref[r]
    fresh = jnp.logical_or(r == 0, texp_ref[jnp.maximum(r - 1, 0)] != e)
    n_chunk = wg_ref.shape[1] // tf

    def chunk_copies(j, slot):
        cols = pl.ds(j * tf, tf)
        return [pltpu.make_async_copy(wg_hbm.at[e, :, cols], stage_in_ref.at[slot, 0], sem_ref.at[slot, 0]),
                pltpu.make_async_copy(wu_hbm.at[e, :, cols], stage_in_ref.at[slot, 1], sem_ref.at[slot, 1]),
                pltpu.make_async_copy(wd_hbm.at[e, cols, :], stage_out_ref.at[slot], sem_ref.at[slot, 2])]

    def chunk(j, hb):
        cols = slice(j * tf, (j + 1) * tf)
        act = (_silu(jnp.dot(hb, wg_ref[:, cols], preferred_element_type=F32))
               * jnp.dot(hb, wu_ref[:, cols], preferred_element_type=F32))
        part = jnp.dot(act.astype(BF16), wd_ref[cols, :], preferred_element_type=F32)
        if j == 0:
            acc_ref[...] = part
        else:
            acc_ref[...] += part

    def finish():
        w = jnp.sum(ws_ref[...], axis=1, keepdims=True)
        ys_ref[...] = (acc_ref[...] * w).astype(ys_ref.dtype)

    @pl.when(jnp.logical_and(valid, fresh))
    def _():
        for c in chunk_copies(0, 0):
            c.start()
        hb = hs_ref[...]
        for j in range(n_chunk):
            slot = j & 1
            if j + 1 < n_chunk:
                for c in chunk_copies(j + 1, 1 - slot):
                    c.start()
            for c in chunk_copies(j, slot):
                c.wait()
            cols = slice(j * tf, (j + 1) * tf)
            wg_ref[:, cols] = stage_in_ref[slot, 0].astype(BF16)
            wu_ref[:, cols] = stage_in_ref[slot, 1].astype(BF16)
            wd_ref[cols, :] = stage_out_ref[slot].astype(BF16)
            chunk(j, hb)
        finish()

    @pl.when(jnp.logical_and(valid, jnp.logical_not(fresh)))
    def _():
        hb = hs_ref[...]
        for j in range(n_chunk):
            chunk(j, hb)
        finish()

    @pl.when(jnp.logical_not(valid))
    def _():
        ys_ref[...] = jnp.zeros_like(ys_ref)


def moe_experts(plan, hs, ws, w_gate, w_up, w_down, tf=256):
    rows, d = hs.shape
    tm = MOE_TILE
    f = w_gate.shape[2]
    rows2 = lambda r, te, nv: (r, 0)
    hbm = pl.BlockSpec(memory_space=pl.ANY)
    return pl.pallas_call(
        functools.partial(_experts_kernel, tf=tf),
        grid_spec=pltpu.PrefetchScalarGridSpec(
            num_scalar_prefetch=2,
            grid=(rows // tm,),
            in_specs=[pl.BlockSpec((tm, d), rows2), pl.BlockSpec((tm, LANES), rows2), hbm, hbm, hbm],
            out_specs=pl.BlockSpec((tm, d), rows2),
            scratch_shapes=[pltpu.VMEM((d, f), BF16), pltpu.VMEM((d, f), BF16), pltpu.VMEM((f, d), BF16),
                            pltpu.VMEM((2, 2, d, tf), F32), pltpu.VMEM((2, tf, d), F32),
                            pltpu.SemaphoreType.DMA((2, 3)), pltpu.VMEM((tm, d), F32)]),
        out_shape=jax.ShapeDtypeStruct((rows, d), BF16),
        compiler_params=_params(("arbitrary",)),
        name="moe_experts",
    )(plan["texp"], plan["n_valid"], hs, ws, w_gate, w_up, w_down)


def _combine_windows(before, total, off, n_rows):
    nblk = before.shape[0]
    after = jnp.concatenate([before[1:], total[None, :]], axis=0)
    lo = off[None, :] + before
    hi = off[None, :] + after
    first = (lo // ROW_ALIGN) * ROW_ALIGN
    nwin = jnp.where(hi > lo, (hi - first + MOE_WIN - 1) // MOE_WIN, 0)
    ends = jnp.cumsum(nwin, axis=1)
    k = jnp.arange(COMBINE_MAX_WIN, dtype=jnp.int32)
    e_of = jnp.minimum(jnp.sum((ends[:, None, :] <= k[None, :, None]).astype(jnp.int32), axis=2), N_EXPERTS - 1)
    i_of = k[None, :] - jnp.take_along_axis(ends - nwin, e_of, axis=1)
    live = k[None, :] < ends[:, -1:]
    want = jnp.where(live, jnp.take_along_axis(first, e_of, axis=1) + i_of * MOE_WIN, 0)
    start = jnp.clip(want, 0, n_rows - MOE_WIN)
    limit = jnp.where(live, jnp.take_along_axis(hi, e_of, axis=1), 0)
    flat = lambda a: a.reshape(-1).astype(jnp.int32)
    trips = ((ends[:, -1] + COMBINE_GROUP - 1) // COMBINE_GROUP).astype(jnp.int32)
    return flat(start), flat(want), flat(limit), trips, (jnp.cumsum(trips) - trips).astype(jnp.int32)


def _combine_kernel(start_ref, want_ref, limit_ref, trips_ref, seq_ref, off_ref, tok_ref, x_ref, mod_ref,
                    fnw_ref, ys_hbm, o_ref, buf_ref, sem_ref, acc_ref):
    b = pl.program_id(0)
    trips = trips_ref[b]
    nxt = jnp.minimum(b + 1, pl.num_programs(0) - 1)
    last_block = b + 1 >= pl.num_programs(0)
    tok = tok_ref[0]
    pos = [_slot_positions(tok, k, off_ref, False) for k in range(2)]

    def window_copies(blk, t, slot):
        copies = []
        for g in range(COMBINE_GROUP):
            start = pl.multiple_of(start_ref[blk * COMBINE_MAX_WIN + t * COMBINE_GROUP + g], ROW_ALIGN)
            copies.append(pltpu.make_async_copy(ys_hbm.at[pl.ds(start, MOE_WIN), :],
                                                buf_ref.at[slot, pl.ds(g * MOE_WIN, MOE_WIN), :],
                                                sem_ref.at[slot, g]))
        return copies

    @pl.when(b == 0)
    def _():
        for c in window_copies(b, 0, 0):
            c.start()

    acc_ref[...] = jnp.zeros_like(acc_ref)

    def trip(t, carry):
        slot = (seq_ref[b] + t) & 1
        more_here = t + 1 < trips

        @pl.when(jnp.logical_or(more_here, jnp.logical_not(last_block)))
        def _():
            for c in window_copies(jnp.where(more_here, b, nxt), jnp.where(more_here, t + 1, 0), 1 - slot):
                c.start()

        for c in window_copies(b, t, slot):
            c.wait()
        ids = []
        for g in range(COMBINE_GROUP):
            w = b * COMBINE_MAX_WIN + t * COMBINE_GROUP + g
            local = start_ref[w] + lax.broadcasted_iota(jnp.int32, (1, MOE_WIN), 1)
            ids.append(jnp.where(local >= want_ref[w], jnp.where(local < limit_ref[w], local, -1), -1))
        rowid = jnp.concatenate(ids, axis=1).astype(F32)
        onehot = (jnp.where(pos[0] == rowid, 1.0, 0.0) + jnp.where(pos[1] == rowid, 1.0, 0.0))
        acc_ref[...] += jnp.dot(onehot.astype(BF16), buf_ref[slot], preferred_element_type=F32)
        return carry

    lax.fori_loop(0, trips, trip, 0)
    o_ref[0] = _rms(x_ref[0] + mod_ref[0][2:3] * acc_ref[...], fnw_ref[...])


def moe_combine(plan, before, total, tok, ys, x, mod, final_nw):
    bsz, s, d = x.shape
    nb = MOE_TILE
    per_b = s // nb
    start, want, limit, trips, seq = _combine_windows(before, total, plan["off"], ys.shape[0])
    tok_blk = lambda b, *_: (b, 0, 0)
    x_blk = lambda b, *_: (b // per_b, b % per_b, 0)
    return pl.pallas_call(
        _combine_kernel,
        grid_spec=pltpu.PrefetchScalarGridSpec(
            num_scalar_prefetch=6,
            grid=(bsz * per_b,),
            in_specs=[pl.BlockSpec((1, nb, LANES), tok_blk),
                      pl.BlockSpec((1, nb, d), x_blk),
                      pl.BlockSpec((1, 3, d), lambda b, *_: (b // per_b, 0, 0)),
                      pl.BlockSpec((1, d), lambda b, *_: (0, 0)),
                      pl.BlockSpec(memory_space=pl.ANY)],
            out_specs=pl.BlockSpec((1, nb, d), x_blk),
            scratch_shapes=[pltpu.VMEM((2, COMBINE_GROUP * MOE_WIN, d), BF16),
                            pltpu.SemaphoreType.DMA((2, COMBINE_GROUP)),
                            pltpu.VMEM((nb, d), F32)]),
        out_shape=jax.ShapeDtypeStruct(x.shape, F32),
        compiler_params=_params(("arbitrary",)),
        name="moe_combine",
    )(start, want, limit, trips, seq, plan["off"], tok, x, mod, final_nw.reshape(1, d), ys)


def moe_layer(x, mod, nw, w_router, w_gate, w_up, w_down, final_nw):
    bsz, s, d = x.shape
    nblk = bsz * s // MOE_TILE
    n_tiles = 2 * nblk + N_EXPERTS
    hb, tok, tokt, tokw, before, total = moe_router(x, mod, nw, w_router)
    before = before[:, 0, :N_EXPERTS].astype(jnp.int32)
    total = total[0, :N_EXPERTS].astype(jnp.int32)
    plan = _moe_plan(before, total, n_tiles)
    hs, ws = moe_gather(plan, tokt, tokw, hb, n_tiles)
    ys = moe_experts(plan, hs, ws, w_gate, w_up, w_down)
    return moe_combine(plan, before, total, tok, ys, x, mod, final_nw)


def kernel(x, c, e_norm_mix, e_mod_mix_w, e_mod_mix_b, e_w_in, e_fox_fb, e_gla_w2, e_gla_b2, e_gla_norm, e_w_o, e_norm_ffn, e_mod_ffn_w, e_mod_ffn_b, e_ffn_gate, e_ffn_up, e_ffn_down, o_norm_mix, o_mod_mix_w, o_mod_mix_b, o_w_in, o_lam_re, o_lam_im, o_log_dt, o_b_re, o_b_im, o_c_re, o_c_im, o_d_skip, o_w_glu, o_w_o, o_norm_ffn, o_mod_ffn_w, o_mod_ffn_b, o_router, o_exp_gate, o_exp_up, o_exp_down, final_norm):
    bsz, s, d = x.shape

    mod = adaln_mod(c, e_mod_mix_w[0], e_mod_mix_b[0])
    fq, fk, fv, gq, gk, gv, gg, small, cum = layer0_inproj(x, mod, e_norm_mix[0], e_w_in[0], e_fox_fb[0])
    fox = fox_attention(fq, fk, fv, cum)
    gla = gla_attention(gq, gk, gv, gg, small, e_gla_w2[0], e_gla_b2[0], e_gla_norm[0])
    mod_ffn = adaln_mod(c, e_mod_ffn_w[0], e_mod_ffn_b[0])
    x = mixer_out_and_ffn(x, fox, gla, e_w_o[0].astype(BF16), mod, mod_ffn, e_norm_ffn[0],
                          e_ffn_gate[0].astype(BF16), e_ffn_up[0].astype(BF16), e_ffn_down[0].astype(BF16))

    mod = adaln_mod(c, o_mod_mix_w[0], o_mod_mix_b[0])
    s5_tb = 256
    bb_re, bb_im, pow_re, pow_im = s5_discretise(o_lam_re[0], o_lam_im[0], o_log_dt[0], o_b_re[0], o_b_im[0],
                                                 s5_tb // S5_SEG)
    x = s5_layer(x, mod, o_norm_mix[0], o_w_in[0], bb_re, bb_im, pow_re, pow_im, o_c_re[0], o_c_im[0],
                 o_d_skip[0], o_w_glu[0], o_w_o[0], s5_tb)
    mod = adaln_mod(c, o_mod_ffn_w[0], o_mod_ffn_b[0])
    return moe_layer(x, mod, o_norm_ffn[0], o_router[0], o_exp_gate[0], o_exp_up[0], o_exp_down[0], final_norm)
```

```python
import functools

import jax
import jax.numpy as jnp
from jax import lax
from jax.experimental import pallas as pl
from jax.experimental.pallas import tpu as pltpu

F32 = jnp.float32
BF16 = jnp.bfloat16
EPS = 1e-6
NEG_BIG = -1e30
LOG2E = 1.4426950408889634
FOX_BIAS_PARTS = 3
FOX_ROWS = 128
FOX_UNROLL = 4

LANES = 128
VMEM_LIMIT = 56 * 1024 * 1024

HEAD_DIM = 64
FOX_HEADS = 8
FOX_DIM = FOX_HEADS * HEAD_DIM
GLA_HEADS = 4
GLA_DK = 64
GLA_DV = 128
GLA_QK = GLA_HEADS * GLA_DK
GLA_V = GLA_HEADS * GLA_DV
GLA_RANK = 16
GLA_TAU = 16.0
GLA_CHUNK = 64
GLA_SUB = 8
S5_GB = 8
N_EXPERTS = 8
SMALL_FF0 = 0
SMALL_LR0 = 8


def _params(sem):
    return pltpu.CompilerParams(dimension_semantics=sem, vmem_limit_bytes=VMEM_LIMIT)


def _log_sigmoid(x):
    return jnp.minimum(x, 0.0) - jnp.log(1.0 + jnp.exp(-jnp.abs(x)))


def _silu(x):
    return x * jax.nn.sigmoid(x)


def _gelu_tanh(x):
    c = 0.7978845608028654
    return 0.5 * x * (1.0 + jnp.tanh(c * (x + 0.044715 * (x * x * x))))


def _rms(x, w):
    return x * lax.rsqrt(jnp.mean(x * x, axis=-1, keepdims=True) + EPS) * w


def _norm_mod(x, nw, mod):
    return _rms(x, nw) * (1.0 + mod[1:2]) + mod[0:1]


def _split_bf16(x, pieces):
    out = []
    for _ in range(pieces):
        p = x.astype(BF16)
        out.append(p)
        x = x - p.astype(F32)
    return out


def _dot_f32ish(a, b_hi, b_lo):
    a_hi, a_lo = _split_bf16(a, 2)
    return (jnp.dot(a_hi, b_hi, preferred_element_type=F32) + jnp.dot(a_hi, b_lo, preferred_element_type=F32)
            + jnp.dot(a_lo, b_hi, preferred_element_type=F32))


def _dot_exact_left(a_bf16, b):
    return sum(jnp.dot(a_bf16, p, preferred_element_type=F32) for p in _split_bf16(b, 3))


def _nt_dot(a, b):
    return lax.dot_general(a, b, (((1,), (1,)), ((), ())), preferred_element_type=F32)


def _mod_kernel(ct_ref, *refs):
    o_ref = refs[-1]
    n_mod = o_ref.shape[0]
    s = _silu(ct_ref[...])
    for m in range(n_mod):
        w = refs[m][...]
        bias = refs[n_mod + m][...]
        for b in range(s.shape[1]):
            o_ref[m, b:b + 1, :] = jnp.sum(s[:, b:b + 1] * w, axis=0, keepdims=True) + bias


def adaln_mods(c, weights, biases):
    bsz, d = c.shape
    n = weights[0].shape[1]
    tn = n // 3
    n_mod = len(weights)
    out = pl.pallas_call(
        _mod_kernel,
        grid=(n // tn,),
        in_specs=[pl.BlockSpec((d, bsz), lambda j: (0, 0))]
                 + [pl.BlockSpec((d, tn), lambda j: (0, j))] * n_mod
                 + [pl.BlockSpec((1, tn), lambda j: (0, j))] * n_mod,
        out_specs=pl.BlockSpec((n_mod, bsz, tn), lambda j: (0, 0, j)),
        out_shape=jax.ShapeDtypeStruct((n_mod, bsz, n), F32),
        compiler_params=_params(("parallel",)),
        name="adaln_mods",
    )(c.T, *weights, *[b.reshape(1, n) for b in biases])
    return [out[m].reshape(bsz, 3, d) for m in range(n_mod)]


def _inproj_kernel(x_ref, mod_ref, nw_ref, wbig_ref, wsm_ref, fb_ref, tri_ref,
                   fq_ref, fk_ref, fv_ref, gq_ref, gk_ref, gv_ref, gg_ref, sm_ref, cum_ref,
                   carry_ref):
    i = pl.program_id(1)
    tm = x_ref.shape[1]
    h = _norm_mod(x_ref[0], nw_ref[...], mod_ref[0])
    hb = h.astype(BF16)
    col = 0
    for ref in (fq_ref, fk_ref, fv_ref, gq_ref, gk_ref, gv_ref, gg_ref):
        n = ref.shape[2]
        y = jnp.dot(hb, wbig_ref[:, col:col + n], preferred_element_type=F32)
        if ref is fq_ref:
            y = y * (LOG2E * HEAD_DIM ** -0.5)
        ref[0] = y.astype(ref.dtype)
        col += n
    small = _dot_f32ish(h, wsm_ref[0], wsm_ref[1])
    sm_ref[0] = small
    logf = _log_sigmoid(small + fb_ref[...])

    @pl.when(i == 0)
    def _():
        carry_ref[...] = jnp.zeros_like(carry_ref)

    cum = _dot_exact_left(tri_ref[...], logf) + carry_ref[...]
    cum_ref[0] = cum
    carry_ref[...] = cum[tm - 1:tm, :]


def layer0_inproj(x, mod, nw, w_in, fox_fb, tm=512):
    bsz, s, d = x.shape
    c0 = 3 * FOX_DIM
    c_ff = c0
    c_g = c_ff + FOX_HEADS
    c_lr = c_g + 2 * GLA_QK + 2 * GLA_V
    wbig = jnp.concatenate([w_in[:, :c0], w_in[:, c_g:c_lr]], axis=1).astype(BF16)
    wsm = jnp.zeros((d, LANES), F32)
    wsm = wsm.at[:, SMALL_FF0:SMALL_FF0 + FOX_HEADS].set(w_in[:, c_ff:c_g])
    wsm = wsm.at[:, SMALL_LR0:SMALL_LR0 + GLA_RANK].set(w_in[:, c_lr:])
    fb = jnp.zeros((1, LANES), F32).at[0, SMALL_FF0:SMALL_FF0 + FOX_HEADS].set(fox_fb)
    wsm = jnp.stack(_split_bf16(wsm, 2))
    tri = jnp.tril(jnp.ones((tm, tm), BF16))
    widths = (FOX_DIM, FOX_DIM, FOX_DIM, GLA_QK, GLA_QK, GLA_V, GLA_V)
    row = lambda b, i: (b, i, 0)
    const2 = lambda b, i: (0, 0)
    outs = pl.pallas_call(
        _inproj_kernel,
        grid=(bsz, s // tm),
        in_specs=[pl.BlockSpec((1, tm, d), row),
                  pl.BlockSpec((1, 3, d), lambda b, i: (b, 0, 0)),
                  pl.BlockSpec((1, d), const2),
                  pl.BlockSpec(wbig.shape, const2),
                  pl.BlockSpec(wsm.shape, lambda b, i: (0, 0, 0)),
                  pl.BlockSpec((1, LANES), const2),
                  pl.BlockSpec((tm, tm), const2)],
        out_specs=[pl.BlockSpec((1, tm, n), row) for n in widths]
                  + [pl.BlockSpec((1, tm, LANES), row)] * 2,
        out_shape=[jax.ShapeDtypeStruct((bsz, s, n), BF16) for n in widths]
                  + [jax.ShapeDtypeStruct((bsz, s, LANES), F32)] * 2,
        scratch_shapes=[pltpu.VMEM((1, LANES), F32)],
        compiler_params=_params(("parallel", "arbitrary")),
        name="layer0_inproj",
    )(x, mod, nw.reshape(1, d), wbig, wsm, fb, tri)
    return outs


def _fox_kernel(q_ref, k_ref, v_ref, c_ref, o_ref, kaug_ref, vaug_ref, m_ref, acc_ref, qaug_ref):
    pair = pl.program_id(1)
    i = pl.program_id(2)
    t = q_ref.shape[1]
    lane = lax.broadcasted_iota(jnp.int32, (1, LANES), 1)
    heads = (lane < HEAD_DIM, lane >= HEAD_DIM)
    spare = (HEAD_DIM, 0)

    @pl.when(i == 0)
    def _():
        k = k_ref[0].astype(F32)
        v = v_ref[0].astype(F32)
        cum = c_ref[0]
        for h in range(2):
            bias = jnp.sum(jnp.where(lane == 2 * pair + h, cum, 0.0), axis=1, keepdims=True) * (-LOG2E)
            ka = jnp.where(heads[h], k, 0.0)
            rest = bias
            for n in range(FOX_BIAS_PARTS):
                piece = rest.astype(BF16).astype(F32)
                rest = rest - piece
                ka = jnp.where(lane == spare[h] + n, piece, ka)
            kaug_ref[h] = ka.astype(BF16)
            vaug_ref[h, :, :LANES] = jnp.where(heads[h], v, 0.0).astype(BF16)
            vaug_ref[h, :, LANES:] = jnp.broadcast_to(jnp.where(heads[h], 1.0, 0.0), v.shape).astype(BF16)

    q = q_ref[0].astype(F32)
    for h in range(2):
        ones_here = jnp.where(lane >= spare[h], jnp.where(lane < spare[h] + FOX_BIAS_PARTS, 1.0, 0.0), 0.0)
        qaug_ref[h] = jnp.where(heads[h], q, ones_here).astype(BF16)
    m_ref[...] = jnp.full_like(m_ref, NEG_BIG)
    acc_ref[...] = jnp.zeros_like(acc_ref)

    def tile(j, diagonal):
        keys = pl.ds(pl.multiple_of(j * t, t), t)
        for r0 in range(0, t, FOX_ROWS):
            rows = slice(r0, r0 + FOX_ROWS)
            alphas = []
            upd = None
            for h in range(2):
                s = _nt_dot(qaug_ref[h, rows, :], kaug_ref[h, keys, :])
                if diagonal:
                    r = r0 + lax.broadcasted_iota(jnp.int32, (FOX_ROWS, t), 0)
                    c = lax.broadcasted_iota(jnp.int32, (FOX_ROWS, t), 1)
                    s = jnp.where(c <= r, s, NEG_BIG)
                m_prev = m_ref[h, rows, :]
                m_next = jnp.maximum(m_prev, jnp.max(s, axis=1, keepdims=True))
                m_ref[h, rows, :] = m_next
                p = jnp.exp2(s - jnp.concatenate([m_next] * (t // LANES), axis=1)).astype(BF16)
                alphas.append(jnp.exp2(m_prev - m_next))
                d = jnp.dot(p, vaug_ref[h, keys, :], preferred_element_type=F32)
                upd = d if upd is None else upd + d
            alpha = jnp.where(heads[0], alphas[0], alphas[1])
            acc_ref[rows, :] = jnp.concatenate([alpha, alpha], axis=1) * acc_ref[rows, :] + upd

    done = 0
    width = FOX_UNROLL
    while width >= 2:
        def group(jj, carry, width=width, done=done):
            for u in range(width):
                tile(done + width * jj + u, False)
            return carry

        trips = (i - done) // width
        lax.fori_loop(0, trips, group, 0)
        done = done + trips * width
        width //= 2

    @pl.when(done < i)
    def _():
        tile(i - 1, False)
        tile(i, True)

    @pl.when(done == i)
    def _():
        tile(i, True)

    o_ref[0] = (acc_ref[:, :LANES] / acc_ref[:, LANES:]).astype(o_ref.dtype)


def fox_attention(fq, fk, fv, cum, t=512):
    bsz, s, _ = fq.shape
    n_pair = FOX_HEADS // 2
    seq = lambda b, p, i: (b, 0, p)
    return pl.pallas_call(
        _fox_kernel,
        grid=(bsz, n_pair, s // t),
        in_specs=[pl.BlockSpec((1, t, LANES), lambda b, p, i: (b, i, p)),
                  pl.BlockSpec((1, s, LANES), seq),
                  pl.BlockSpec((1, s, LANES), seq),
                  pl.BlockSpec((1, s, LANES), lambda b, p, i: (b, 0, 0))],
        out_specs=pl.BlockSpec((1, t, LANES), lambda b, p, i: (b, i, p)),
        out_shape=jax.ShapeDtypeStruct((bsz, s, FOX_DIM), BF16),
        scratch_shapes=[pltpu.VMEM((2, s, LANES), BF16), pltpu.VMEM((2, s, 2 * LANES), BF16),
                        pltpu.VMEM((2, t, LANES), F32), pltpu.VMEM((t, 2 * LANES), F32),
                        pltpu.VMEM((2, t, LANES), BF16)],
        compiler_params=_params(("parallel", "parallel", "arbitrary")),
        name="fox_attention",
    )(fq, fk, fv, cum)


def _gla_pair_chunk(q, k, v, la, st, tri, eye, k_scr, b_scr):
    C, SUB = GLA_CHUNK, GLA_SUB
    nsub = C // SUB
    lane = lax.broadcasted_iota(jnp.int32, (1, LANES), 1)
    head_a = lane < GLA_DK
    b = _dot_exact_left(tri, la) * LOG2E
    b_last = b[C - 1:C, :]
    k_scr[...] = k
    b_scr[...] = b
    rowblk = lax.broadcasted_iota(jnp.int32, (C, 1), 0) // SUB

    refs = [b[m * SUB - 1:m * SUB, :] for m in range(1, nsub)]
    rsel = refs[-1]
    for m in range(nsub - 2, 0, -1):
        rsel = jnp.where(rowblk == m, refs[m - 1], rsel)
    qt = q * jnp.exp2(jnp.minimum(b - rsel, 0.0))
    zero = jnp.zeros_like(q)
    qs = jnp.concatenate([jnp.where(rowblk == m, qt, zero) for m in range(1, nsub)], axis=1)
    ks = jnp.concatenate(
        [jnp.where(rowblk < m, k * jnp.exp2(jnp.minimum(refs[m - 1] - b, 0.0)), zero)
         for m in range(1, nsub)], axis=1)
    lane3 = lax.broadcasted_iota(jnp.int32, (1, (nsub - 1) * LANES), 1)
    head_a3 = (lane3 & (LANES - 1)) < GLA_DK
    zero3 = jnp.zeros_like(ks)
    kstack = jnp.concatenate([jnp.where(head_a3, ks, zero3), jnp.where(head_a3, zero3, ks)], axis=0)
    a_off = _nt_dot(qs.astype(BF16), kstack.astype(BF16))

    row16 = lax.broadcasted_iota(jnp.int32, (SUB, 1), 0)
    blocks = []
    for i in range(nsub):
        qb = q[i * SUB:(i + 1) * SUB, :]
        bb = b[i * SUB:(i + 1) * SUB, :]
        d = jnp.zeros((SUB, LANES), F32)
        for sp in range(SUB):
            srow = i * SUB + sp
            e = qb * k_scr[srow:srow + 1, :] * jnp.exp2(jnp.minimum(bb - b_scr[srow:srow + 1, :], 0.0))
            da = jnp.sum(jnp.where(head_a, e, 0.0), axis=1, keepdims=True)
            db = jnp.sum(jnp.where(head_a, 0.0, e), axis=1, keepdims=True)
            live = row16 >= sp
            da = jnp.where(live, da, 0.0)
            db = jnp.where(live, db, 0.0)
            d = jnp.where(lane == srow, da, jnp.where(lane == GLA_DK + srow, db, d))
        blocks.append(d)
    a_pair = a_off + jnp.concatenate(blocks, axis=0)

    lane_v = lax.broadcasted_iota(jnp.int32, (1, 2 * GLA_DV), 1)
    first_v = lane_v < GLA_DV
    zv = jnp.zeros_like(v)
    vbd = jnp.concatenate([jnp.where(first_v, v, zv), jnp.where(first_v, zv, v)], axis=0)
    qh = q * jnp.exp2(b)
    o = (jnp.dot(a_pair.astype(BF16), vbd, preferred_element_type=F32)
         + _nt_dot(qh.astype(BF16), st.astype(BF16)))

    k_end = k * jnp.exp2(b_last - b)
    v_t = _nt_dot(eye, v).astype(BF16)
    inc = jnp.dot(v_t, k_end.astype(BF16), preferred_element_type=F32)
    row_v = lax.broadcasted_iota(jnp.int32, (2 * GLA_DV, 1), 0)
    same_head = (row_v // GLA_DV) == (lane // GLA_DK)
    st_new = st * jnp.exp2(b_last) + jnp.where(same_head, inc, 0.0)
    return o, st_new


def _gla_kernel(gq_ref, gk_ref, gv_ref, gg_ref, sm_ref, w2_ref, b2_ref, gn_ref, tri_ref, eye_ref,
                o_ref, st_ref, la_ref, k_scr, b_scr):
    i = pl.program_id(1)
    tm = gq_ref.shape[1]
    n_pair = GLA_HEADS // 2
    scale = GLA_DK ** -0.5

    @pl.when(i == 0)
    def _():
        st_ref[...] = jnp.zeros_like(st_ref)

    z = _dot_f32ish(sm_ref[0], w2_ref[0], w2_ref[1]) + b2_ref[...]
    la_ref[...] = _log_sigmoid(z) * (1.0 / GLA_TAU)
    gn = gn_ref[...]

    def chunk(c, carry):
        base = pl.multiple_of(c * GLA_CHUNK, GLA_CHUNK)
        rows = pl.ds(base, GLA_CHUNK)
        for hp in range(n_pair):
            ql = slice(hp * LANES, (hp + 1) * LANES)
            vl = slice(hp * 2 * GLA_DV, (hp + 1) * 2 * GLA_DV)
            q = gq_ref[0, rows, ql].astype(F32) * scale
            k = gk_ref[0, rows, ql].astype(F32)
            v = gv_ref[0, rows, vl]
            o, st_new = _gla_pair_chunk(q, k, v, la_ref[rows, ql], st_ref[hp], tri_ref[...],
                                        eye_ref[...], k_scr.at[hp], b_scr.at[hp])
            st_ref[hp] = st_new
            gate = _silu(gg_ref[0, rows, vl].astype(F32))
            halves = [_rms(o[:, h * GLA_DV:(h + 1) * GLA_DV], gn) for h in range(2)]
            o_ref[0, rows, vl] = (jnp.concatenate(halves, axis=1) * gate).astype(o_ref.dtype)
        return carry

    lax.fori_loop(0, tm // GLA_CHUNK, chunk, 0, unroll=4)


def gla_attention(gq, gk, gv, gg, small, w2, b2, gnorm, tm=512):
    bsz, s, _ = gq.shape
    w2p = jnp.zeros((LANES, GLA_QK), F32).at[SMALL_LR0:SMALL_LR0 + GLA_RANK].set(w2)
    w2p = jnp.stack(_split_bf16(w2p, 2))
    tri = jnp.tril(jnp.ones((GLA_CHUNK, GLA_CHUNK), BF16))
    eye = jnp.eye(2 * GLA_DV, dtype=BF16)
    row = lambda b, i: (b, i, 0)
    const2 = lambda b, i: (0, 0)
    return pl.pallas_call(
        _gla_kernel,
        grid=(bsz, s // tm),
        in_specs=[pl.BlockSpec((1, tm, GLA_QK), row), pl.BlockSpec((1, tm, GLA_QK), row),
                  pl.BlockSpec((1, tm, GLA_V), row), pl.BlockSpec((1, tm, GLA_V), row),
                  pl.BlockSpec((1, tm, LANES), row),
                  pl.BlockSpec(w2p.shape, lambda b, i: (0, 0, 0)), pl.BlockSpec((1, GLA_QK), const2),
                  pl.BlockSpec((1, GLA_DV), const2), pl.BlockSpec(tri.shape, const2),
                  pl.BlockSpec(eye.shape, const2)],
        out_specs=pl.BlockSpec((1, tm, GLA_V), row),
        out_shape=jax.ShapeDtypeStruct((bsz, s, GLA_V), BF16),
        scratch_shapes=[pltpu.VMEM((GLA_HEADS // 2, 2 * GLA_DV, LANES), F32),
                        pltpu.VMEM((tm, GLA_QK), F32),
                        pltpu.VMEM((GLA_HEADS // 2, GLA_CHUNK, LANES), F32),
                        pltpu.VMEM((GLA_HEADS // 2, GLA_CHUNK, LANES), F32)],
        compiler_params=_params(("parallel", "arbitrary")),
        name="gla_attention",
    )(gq, gk, gv, gg, small, w2p, b2.reshape(1, GLA_QK), gnorm.reshape(1, GLA_DV), tri, eye)


def _mix_ffn_kernel(x_ref, a_ref, b_ref, wo_ref, modm_ref, modf_ref, nw_ref, wg_ref, wu_ref, wd_ref,
                    o_ref, acc_ref, *, tf):
    na = a_ref.shape[2]
    y = (jnp.dot(a_ref[0], wo_ref[:na, :], preferred_element_type=F32)
         + jnp.dot(b_ref[0], wo_ref[na:, :], preferred_element_type=F32))
    x1 = x_ref[0] + modm_ref[0][2:3] * y
    hb = _norm_mod(x1, nw_ref[...], modf_ref[0]).astype(BF16)
    for j in range(wg_ref.shape[1] // tf):
        cols = slice(j * tf, (j + 1) * tf)
        act = (_silu(jnp.dot(hb, wg_ref[:, cols], preferred_element_type=F32))
               * jnp.dot(hb, wu_ref[:, cols], preferred_element_type=F32))
        part = jnp.dot(act.astype(BF16), wd_ref[cols, :], preferred_element_type=F32)
        if j == 0:
            acc_ref[...] = part
        else:
            acc_ref[...] += part
    o_ref[0] = x1 + modf_ref[0][2:3] * acc_ref[...]


def mixer_out_and_ffn(x, a, b, w_o, mod_mix, mod_ffn, nw, w_gate, w_up, w_down, tm=512, tf=256):
    bsz, s, d = x.shape
    row = lambda bb, i: (bb, i, 0)
    per_b = lambda bb, i: (bb, 0, 0)
    const2 = lambda bb, i: (0, 0)
    resident = lambda arr: pl.BlockSpec(arr.shape, const2, pipeline_mode=pl.Buffered(1))
    return pl.pallas_call(
        functools.partial(_mix_ffn_kernel, tf=tf),
        grid=(bsz, s // tm),
        in_specs=[pl.BlockSpec((1, tm, d), row),
                  pl.BlockSpec((1, tm, a.shape[2]), row),
                  pl.BlockSpec((1, tm, b.shape[2]), row),
                  resident(w_o),
                  pl.BlockSpec((1, 3, d), per_b),
                  pl.BlockSpec((1, 3, d), per_b),
                  pl.BlockSpec((1, d), const2),
                  resident(w_gate), resident(w_up), resident(w_down)],
        out_specs=pl.BlockSpec((1, tm, d), row),
        out_shape=jax.ShapeDtypeStruct(x.shape, F32),
        scratch_shapes=[pltpu.VMEM((tm, d), F32)],
        compiler_params=_params(("parallel", "parallel")),
        name="mixer_out_and_ffn",
    )(x, a, b, w_o, mod_mix, mod_ffn, nw.reshape(1, d), w_gate, w_up, w_down)


S5_SEG = 8


def _s5_disc_kernel(lre_ref, lim_ref, ldt_ref, bre_ref, bim_ref,
                    bbre_ref, bbim_ref, pre_ref, pim_ref):
    lre = lre_ref[...]
    lim = lim_ref[...]
    dt = jnp.exp(ldt_ref[...])
    mag = jnp.exp(lre * dt)
    a_re = mag * jnp.cos(lim * dt)
    a_im = mag * jnp.sin(lim * dt)
    den = lre * lre + lim * lim
    nr = a_re - 1.0
    ni = a_im
    f_re = (nr * lre + ni * lim) / den
    f_im = (ni * lre - nr * lim) / den
    bre = bre_ref[...]
    bim = bim_ref[...]
    bbre_ref[...] = f_re[:, None, :] * bre - f_im[:, None, :] * bim
    bbim_ref[...] = f_re[:, None, :] * bim + f_im[:, None, :] * bre
    n = (lax.broadcasted_iota(jnp.int32, pre_ref.shape, 0) + 1).astype(F32)
    mag_n = jnp.exp(n * (lre * dt)[None])
    ang_n = n * (lim * dt)[None]
    pre_ref[...] = mag_n * jnp.cos(ang_n)
    pim_ref[...] = mag_n * jnp.sin(ang_n)


def s5_discretise(lam_re, lam_im, log_dt, b_re, b_im, seg_len):
    g, p = lam_re.shape
    k = b_re.shape[2]
    return pl.pallas_call(
        _s5_disc_kernel,
        out_shape=[jax.ShapeDtypeStruct((g, k, p), F32)] * 2 + [jax.ShapeDtypeStruct((seg_len, g, p), F32)] * 2,
        name="s5_discretise",
    )(lam_re, lam_im, log_dt.reshape(g, 1), b_re.transpose(0, 2, 1), b_im.transpose(0, 2, 1))


def _s5_kernel(x_ref, mod_ref, nw_ref, perm_ref, permt_ref, win_ref, bbd_ref, cbd_ref, apow_ref, dsk_ref,
               wglu_ref, wo_ref, o_ref, carry_ref, bu_ref, xs_ref, xsb_ref):
    i = pl.program_id(1)
    tb = x_ref.shape[1]
    nblk = bu_ref.shape[0]
    nch = bu_ref.shape[2] // (2 * LANES)
    seg_len = tb // S5_SEG
    re_l = [slice(ch * LANES, (ch + 1) * LANES) for ch in range(nch)]
    im_l = [slice((nch + ch) * LANES, (nch + ch + 1) * LANES) for ch in range(nch)]

    @pl.when(i == 0)
    def _():
        carry_ref[...] = jnp.zeros_like(carry_ref)

    x = x_ref[0]
    h = _norm_mod(x, nw_ref[...], mod_ref[0]).astype(BF16)
    hp = jnp.dot(perm_ref[...], h, preferred_element_type=F32).astype(BF16)
    u = jnp.dot(hp, win_ref[...], preferred_element_type=F32)
    ub = u.astype(BF16)
    for jb in range(nblk):
        bu_ref[jb] = jnp.dot(ub[:, jb * LANES:(jb + 1) * LANES], bbd_ref[jb], preferred_element_type=F32)

    seg_row = lax.broadcasted_iota(jnp.int32, (S5_SEG, 1), 0)

    def channel_block(jb, blk_carry):
        def bcast(n, lanes):
            return jnp.broadcast_to(apow_ref[jb, n:n + 1, lanes], (S5_SEG, LANES))

        a1 = [(bcast(0, re_l[ch]), bcast(0, im_l[ch])) for ch in range(nch)]

        def step(t, state):
            rows = pl.ds(pl.multiple_of(t * S5_SEG, S5_SEG), S5_SEG)
            out = []
            for ch in range(nch):
                xr, xi = state[2 * ch], state[2 * ch + 1]
                ar, ai = a1[ch]
                nr = ar * xr - ai * xi + bu_ref[jb, rows, re_l[ch]]
                ni = ar * xi + ai * xr + bu_ref[jb, rows, im_l[ch]]
                xs_ref[jb, rows, re_l[ch]] = nr
                xs_ref[jb, rows, im_l[ch]] = ni
                out += [nr, ni]
            return tuple(out)

        zero = jnp.zeros((S5_SEG, LANES), F32)
        ends = lax.fori_loop(0, seg_len, step, (zero,) * (2 * nch), unroll=4)

        starts = []
        for ch in range(nch):
            alr = apow_ref[jb, seg_len - 1:seg_len, re_l[ch]]
            ali = apow_ref[jb, seg_len - 1:seg_len, im_l[ch]]
            er, ei = ends[2 * ch], ends[2 * ch + 1]
            zr = carry_ref[jb, :, re_l[ch]]
            zi = carry_ref[jb, :, im_l[ch]]
            tr = jnp.zeros((S5_SEG, LANES), F32)
            ti = jnp.zeros((S5_SEG, LANES), F32)
            for s in range(S5_SEG):
                tr = jnp.where(seg_row == s, zr, tr)
                ti = jnp.where(seg_row == s, zi, ti)
                zr, zi = (alr * zr - ali * zi + er[s:s + 1], alr * zi + ali * zr + ei[s:s + 1])
            carry_ref[jb, :, re_l[ch]] = zr
            carry_ref[jb, :, im_l[ch]] = zi
            starts.append((jnp.concatenate([tr, tr], axis=0), jnp.concatenate([ti, ti], axis=0)))

        for t2 in range(seg_len // 2):
            rows = slice(t2 * 2 * S5_SEG, (t2 + 1) * 2 * S5_SEG)
            for ch in range(nch):
                pr = jnp.concatenate([bcast(2 * t2, re_l[ch]), bcast(2 * t2 + 1, re_l[ch])], axis=0)
                pi = jnp.concatenate([bcast(2 * t2, im_l[ch]), bcast(2 * t2 + 1, im_l[ch])], axis=0)
                zr, zi = starts[ch]
                xsb_ref[jb, rows, re_l[ch]] = (xs_ref[jb, rows, re_l[ch]] + pr * zr - pi * zi).astype(BF16)
                xsb_ref[jb, rows, im_l[ch]] = (xs_ref[jb, rows, im_l[ch]] + pr * zi + pi * zr).astype(BF16)
        return blk_carry

    lax.fori_loop(0, nblk, channel_block, 0)

    ys = [jnp.dot(xsb_ref[jb], cbd_ref[jb], preferred_element_type=F32) for jb in range(nblk)]
    y = jnp.concatenate(ys, axis=1) + dsk_ref[...] * u
    y = _gelu_tanh(y)
    y = y * jax.nn.sigmoid(jnp.dot(y.astype(BF16), wglu_ref[...], preferred_element_type=F32))
    z = jnp.dot(y.astype(BF16), wo_ref[...], preferred_element_type=F32)
    z_hi = z.astype(BF16)
    z_lo = (z - z_hi.astype(F32)).astype(BF16)
    z = (jnp.dot(permt_ref[...], z_hi, preferred_element_type=F32)
         + jnp.dot(permt_ref[...], z_lo, preferred_element_type=F32))
    o_ref[0] = x + mod_ref[0][2:3] * z


def s5_layer(x, mod, nw, w_in, bb_re, bb_im, pow_re, pow_im, c_re, c_im, d_skip, w_glu, w_o, tb):
    bsz, s, d = x.shape
    g, k, p = bb_re.shape
    nblk = g // S5_GB
    seg_len = pow_re.shape[0]
    assert tb == seg_len * S5_SEG
    eye = jnp.eye(S5_GB, dtype=F32)

    def blockdiag_in(bb):
        t = jnp.einsum('jgkp,gh->jgkhp', bb.reshape(nblk, S5_GB, k, p), eye)
        return t.reshape(nblk, S5_GB * k, S5_GB * p)

    def blockdiag_out(c):
        t = jnp.einsum('jgkp,gh->jgphk', c.reshape(nblk, S5_GB, k, p), eye)
        return t.reshape(nblk, S5_GB * p, S5_GB * k)

    bbd = jnp.concatenate([blockdiag_in(bb_re), blockdiag_in(bb_im)], axis=2).astype(BF16)
    cbd = jnp.concatenate([blockdiag_out(c_re), -blockdiag_out(c_im)], axis=1).astype(BF16)
    half = S5_GB * p
    apow = jnp.concatenate([pow_re.reshape(seg_len, nblk, half), pow_im.reshape(seg_len, nblk, half)],
                           axis=2).transpose(1, 0, 2)
    new_row = jnp.arange(tb)
    old_row = (new_row % S5_SEG) * seg_len + new_row // S5_SEG
    perm = (old_row[:, None] == jnp.arange(tb)[None, :]).astype(BF16)
    row = lambda b, i: (b, i, 0)
    const2 = lambda b, i: (0, 0)
    const3 = lambda b, i: (0, 0, 0)
    return pl.pallas_call(
        _s5_kernel,
        grid=(bsz, s // tb),
        in_specs=[pl.BlockSpec((1, tb, d), row),
                  pl.BlockSpec((1, 3, d), lambda b, i: (b, 0, 0)),
                  pl.BlockSpec((1, d), const2),
                  pl.BlockSpec((tb, tb), const2), pl.BlockSpec((tb, tb), const2),
                  pl.BlockSpec((d, d), const2),
                  pl.BlockSpec(bbd.shape, const3),
                  pl.BlockSpec(cbd.shape, const3),
                  pl.BlockSpec(apow.shape, const3),
                  pl.BlockSpec((1, d), const2),
                  pl.BlockSpec((d, d), const2), pl.BlockSpec((d, d), const2)],
        out_specs=pl.BlockSpec((1, tb, d), row),
        out_shape=jax.ShapeDtypeStruct(x.shape, F32),
        scratch_shapes=[pltpu.VMEM((nblk, 1, 2 * half), F32),
                        pltpu.VMEM((nblk, tb, 2 * half), F32),
                        pltpu.VMEM((nblk, tb, 2 * half), F32),
                        pltpu.VMEM((nblk, tb, 2 * half), BF16)],
        compiler_params=_params(("parallel", "arbitrary")),
        name="s5_layer",
    )(x, mod, nw.reshape(1, d), perm, perm.T, w_in.astype(BF16), bbd, cbd, apow, d_skip.reshape(1, d),
      w_glu.astype(BF16), w_o.astype(BF16))


TOK_E, TOK_RANK, TOK_W = 0, 2, 4
W_PIECES = 3
MOE_TILE = 512
MOE_WIN = 192
ROW_ALIGN = 16
COMBINE_GROUP = 2
COMBINE_MAX_WIN = -(-((2 * MOE_TILE + N_EXPERTS * (ROW_ALIGN - 1)) // MOE_WIN + N_EXPERTS + 1)
                    // COMBINE_GROUP) * COMBINE_GROUP


def _router_kernel(x_ref, mod_ref, nw_ref, wr_ref, tri_ref, sel_ref,
                   hb_ref, tok_ref, tokt_ref, tokw_ref, before_ref, total_ref, carry_ref):
    blk = pl.program_id(0)
    h = _norm_mod(x_ref[0], nw_ref[...], mod_ref[0])
    hb_ref[0] = h.astype(BF16)
    logits = _dot_f32ish(h, wr_ref[0], wr_ref[1])
    lane = lax.broadcasted_iota(jnp.int32, logits.shape, 1).astype(F32)
    logits = jnp.where(lane < N_EXPERTS, logits, -jnp.inf)
    m1 = jnp.max(logits, axis=1, keepdims=True)
    i1 = jnp.min(jnp.where(logits == m1, lane, float(LANES)), axis=1, keepdims=True)
    rest = jnp.where(lane == i1, -jnp.inf, logits)
    m2 = jnp.max(rest, axis=1, keepdims=True)
    i2 = jnp.min(jnp.where(rest == m2, lane, float(LANES)), axis=1, keepdims=True)
    e2 = jnp.exp(m2 - m1)
    w1 = 1.0 / (1.0 + e2)
    w2 = e2 / (1.0 + e2)

    @pl.when(blk == 0)
    def _():
        carry_ref[...] = jnp.zeros_like(carry_ref)

    routed = jnp.where(lane == i1, 1.0, jnp.where(lane == i2, 1.0, 0.0))
    before = carry_ref[...]
    rank = jnp.dot(tri_ref[...], routed.astype(BF16), preferred_element_type=F32) + before
    r1 = jnp.sum(jnp.where(lane == i1, rank, 0.0), axis=1, keepdims=True)
    r2 = jnp.sum(jnp.where(lane == i2, rank, 0.0), axis=1, keepdims=True)
    fields = ((TOK_E, i1), (TOK_E + 1, i2), (TOK_RANK, r1), (TOK_RANK + 1, r2), (TOK_W, w1), (TOK_W + 1, w2))
    tok = jnp.zeros_like(logits)
    for ln, val in fields:
        tok = jnp.where(lane == ln, val, tok)
    tok_ref[0] = tok
    tokt_ref[0] = sum(_nt_dot(sel_ref[...], p) for p in _split_bf16(tok, 3))
    pieces = jnp.zeros_like(logits)
    for k, w in enumerate((w1, w2)):
        rest = w
        for n in range(W_PIECES):
            piece = rest.astype(BF16).astype(F32)
            rest = rest - piece
            pieces = jnp.where(lane == k * W_PIECES + n, piece, pieces)
    tokw_ref[0] = pieces.astype(BF16)
    before_ref[0] = before
    total = before + jnp.sum(routed, axis=0, keepdims=True)
    carry_ref[...] = total
    total_ref[...] = total


def moe_router(x, mod, nw, w_router):
    bsz, s, d = x.shape
    nb = MOE_TILE
    per_b = s // nb
    nblk = bsz * per_b
    wr = jnp.stack(_split_bf16(jnp.zeros((d, LANES), F32).at[:, :N_EXPERTS].set(w_router), 2))
    tri = jnp.tril(jnp.ones((nb, nb), BF16), k=-1)
    sel = jnp.eye(8, LANES, dtype=BF16)
    const2 = lambda i: (0, 0)
    return pl.pallas_call(
        _router_kernel,
        grid=(nblk,),
        in_specs=[pl.BlockSpec((1, nb, d), lambda i: (i // per_b, i % per_b, 0)),
                  pl.BlockSpec((1, 3, d), lambda i: (i // per_b, 0, 0)),
                  pl.BlockSpec((1, d), const2),
                  pl.BlockSpec((2, d, LANES), lambda i: (0, 0, 0)),
                  pl.BlockSpec((nb, nb), const2),
                  pl.BlockSpec((8, LANES), const2)],
        out_specs=[pl.BlockSpec((1, nb, d), lambda i: (i, 0, 0)),
                   pl.BlockSpec((1, nb, LANES), lambda i: (i, 0, 0)),
                   pl.BlockSpec((1, 8, nb), lambda i: (i, 0, 0)),
                   pl.BlockSpec((1, nb, LANES), lambda i: (i, 0, 0)),
                   pl.BlockSpec((1, 1, LANES), lambda i: (i, 0, 0)),
                   pl.BlockSpec((1, LANES), const2)],
        out_shape=[jax.ShapeDtypeStruct((nblk, nb, d), BF16),
                   jax.ShapeDtypeStruct((nblk, nb, LANES), F32),
                   jax.ShapeDtypeStruct((nblk, 8, nb), F32),
                   jax.ShapeDtypeStruct((nblk, nb, LANES), BF16),
                   jax.ShapeDtypeStruct((nblk, 1, LANES), F32),
                   jax.ShapeDtypeStruct((1, LANES), F32)],
        scratch_shapes=[pltpu.VMEM((1, LANES), F32)],
        compiler_params=_params(("arbitrary",)),
        name="moe_router",
    )(x, mod, nw.reshape(1, d), wr, tri, sel)


def _moe_plan(before, total, n_tiles):
    tm = MOE_TILE
    ntile = (total + tm - 1) // tm
    tile_end = jnp.cumsum(ntile)
    tile_start = tile_end - ntile
    n_valid = tile_end[-1]
    r = jnp.arange(n_tiles, dtype=jnp.int32)
    texp = jnp.minimum(jnp.sum((tile_end[None, :] <= r[:, None]).astype(jnp.int32), axis=1), N_EXPERTS - 1)
    r0 = (r - tile_start[texp]) * tm
    after = jnp.concatenate([before[1:], total[None, :]], axis=0)
    lo = before[:, texp].T
    hi = after[:, texp].T
    meets = (lo < (r0 + tm)[:, None]) & (hi > r0[:, None]) & (r < n_valid)[:, None]
    count = jnp.sum(meets, axis=1).astype(jnp.int32)
    first_blk = jnp.sum(jnp.cumsum(meets, axis=1) == 0, axis=1).astype(jnp.int32)
    start = jnp.clip(lo - r0[:, None], 0, tm)
    end = jnp.clip(hi - r0[:, None], 0, tm)
    first = (start // ROW_ALIGN) * ROW_ALIGN
    nwin = jnp.where(meets, (end - first + MOE_WIN - 1) // MOE_WIN, 0)
    flat = lambda a: a.reshape(-1).astype(jnp.int32)
    return dict(texp=texp.astype(jnp.int32), n_valid=n_valid.astype(jnp.int32).reshape(1),
                off=(tile_start * tm).astype(jnp.int32),
                g_blk0=jnp.where(count > 0, first_blk, 0), g_count=count,
                g_first=flat(first), g_nwin=flat(nwin))


def _slot_positions(tok_fields, k, off_ref, along_lanes):
    if along_lanes:
        e = tok_fields[TOK_E + k:TOK_E + k + 1, :]
        rank = tok_fields[TOK_RANK + k:TOK_RANK + k + 1, :]
    else:
        e = tok_fields[:, TOK_E + k:TOK_E + k + 1]
        rank = tok_fields[:, TOK_RANK + k:TOK_RANK + k + 1]
    pos = rank
    for ex in range(N_EXPERTS):
        pos = pos + jnp.where(e == float(ex), off_ref[ex].astype(F32), 0.0)
    return pos


def _window_rows(want, tile, tm, shape, axis):
    base = pl.multiple_of(jnp.minimum(want, tm - MOE_WIN), ROW_ALIGN)
    local = base + lax.broadcasted_iota(jnp.int32, shape, axis)
    rowid = jnp.where(local >= want, tile * tm + local, -1).astype(F32)
    return base, rowid


def _gather_kernel(blk0_ref, count_ref, first_ref, nwin_ref, off_ref, tokt_ref, tokw_ref, hb_ref,
                   hs_ref, ws_ref, acc_ref, wacc_ref):
    r = pl.program_id(0)
    tm = hs_ref.shape[0]
    nblk = hb_ref.shape[0]
    b0 = blk0_ref[r]
    lane = lax.broadcasted_iota(jnp.int32, (1, LANES), 1)
    acc_ref[...] = jnp.zeros_like(acc_ref)
    wacc_ref[...] = jnp.zeros_like(wacc_ref)

    def pair(k, carry):
        blk = b0 + k
        tokt = tokt_ref[blk]
        pos = [_slot_positions(tokt, s, off_ref, True) for s in range(2)]
        p = r * nblk + blk

        def window(i, wcarry):
            base, rowid = _window_rows(first_ref[p] + i * MOE_WIN, r, tm, (MOE_WIN, 1), 0)
            onehots = [jnp.where(pos[s] == rowid, 1.0, 0.0).astype(BF16) for s in range(2)]
            rows = pl.ds(base, MOE_WIN)
            acc_ref[rows, :] += jnp.dot(onehots[0] + onehots[1], hb_ref[blk], preferred_element_type=F32)
            picked = [jnp.dot(onehots[s], tokw_ref[blk], preferred_element_type=F32) for s in range(2)]
            wacc_ref[rows, :] += jnp.where(lane < W_PIECES, picked[0],
                                           jnp.where(lane < 2 * W_PIECES, picked[1], 0.0))
            return wcarry

        lax.fori_loop(0, nwin_ref[p], window, 0)
        return carry

    lax.fori_loop(0, count_ref[r], pair, 0)
    hs_ref[...] = acc_ref[...].astype(hs_ref.dtype)
    ws_ref[...] = wacc_ref[...]


def moe_gather(plan, tokt, tokw, hb, n_tiles):
    nblk, nb, d = hb.shape
    tm = MOE_TILE
    by_tile2 = lambda r, *_: (r, 0)
    resident = lambda arr: pl.BlockSpec(arr.shape, lambda r, *_: (0, 0, 0), pipeline_mode=pl.Buffered(1))
    return pl.pallas_call(
        _gather_kernel,
        grid_spec=pltpu.PrefetchScalarGridSpec(
            num_scalar_prefetch=5,
            grid=(n_tiles,),
            in_specs=[resident(tokt), resident(tokw), resident(hb)],
            out_specs=[pl.BlockSpec((tm, d), by_tile2), pl.BlockSpec((tm, LANES), by_tile2)],
            scratch_shapes=[pltpu.VMEM((tm, d), F32), pltpu.VMEM((tm, LANES), F32)]),
        out_shape=[jax.ShapeDtypeStruct((n_tiles * tm, d), BF16),
                   jax.ShapeDtypeStruct((n_tiles * tm, LANES), F32)],
        compiler_params=_params(("arbitrary",)),
        name="moe_gather",
    )(plan["g_blk0"], plan["g_count"], plan["g_first"], plan["g_nwin"], plan["off"], tokt, tokw, hb)


def _experts_kernel(texp_ref, nvalid_ref, hs_ref, ws_ref, wg_hbm, wu_hbm, wd_hbm, ys_ref,
                    wg_ref, wu_ref, wd_ref, stage_in_ref, stage_out_ref, sem_ref, acc_ref, *, tf):
    r = pl.program_id(0)
    valid = r < nvalid_ref[0]
    e = texp_ref[r]
    fresh = jnp.logical_or(r == 0, texp_ref[jnp.maximum(r - 1, 0)] != e)
    n_chunk = wg_ref.shape[1] // tf

    def chunk_copies(j, slot):
        cols = pl.ds(j * tf, tf)
        return [pltpu.make_async_copy(wg_hbm.at[e, :, cols], stage_in_ref.at[slot, 0], sem_ref.at[slot, 0]),
                pltpu.make_async_copy(wu_hbm.at[e, :, cols], stage_in_ref.at[slot, 1], sem_ref.at[slot, 1]),
                pltpu.make_async_copy(wd_hbm.at[e, cols, :], stage_out_ref.at[slot], sem_ref.at[slot, 2])]

    def chunk(j, hb):
        cols = slice(j * tf, (j + 1) * tf)
        act = (_silu(jnp.dot(hb, wg_ref[:, cols], preferred_element_type=F32))
               * jnp.dot(hb, wu_ref[:, cols], preferred_element_type=F32))
        part = jnp.dot(act.astype(BF16), wd_ref[cols, :], preferred_element_type=F32)
        if j == 0:
            acc_ref[...] = part
        else:
            acc_ref[...] += part

    def finish():
        w = jnp.sum(ws_ref[...], axis=1, keepdims=True)
        ys_ref[...] = (acc_ref[...] * w).astype(ys_ref.dtype)

    @pl.when(jnp.logical_and(valid, fresh))
    def _():
        for c in chunk_copies(0, 0):
            c.start()
        hb = hs_ref[...]
        for j in range(n_chunk):
            slot = j & 1
            if j + 1 < n_chunk:
                for c in chunk_copies(j + 1, 1 - slot):
                    c.start()
            for c in chunk_copies(j, slot):
                c.wait()
            cols = slice(j * tf, (j + 1) * tf)
            wg_ref[:, cols] = stage_in_ref[slot, 0].astype(BF16)
            wu_ref[:, cols] = stage_in_ref[slot, 1].astype(BF16)
            wd_ref[cols, :] = stage_out_ref[slot].astype(BF16)
            chunk(j, hb)
        finish()

    @pl.when(jnp.logical_and(valid, jnp.logical_not(fresh)))
    def _():
        hb = hs_ref[...]
        for j in range(n_chunk):
            chunk(j, hb)
        finish()

    @pl.when(jnp.logical_not(valid))
    def _():
        ys_ref[...] = jnp.zeros_like(ys_ref)


def moe_experts(plan, hs, ws, w_gate, w_up, w_down, tf=256):
    rows, d = hs.shape
    tm = MOE_TILE
    f = w_gate.shape[2]
    rows2 = lambda r, te, nv: (r, 0)
    hbm = pl.BlockSpec(memory_space=pl.ANY)
    return pl.pallas_call(
        functools.partial(_experts_kernel, tf=tf),
        grid_spec=pltpu.PrefetchScalarGridSpec(
            num_scalar_prefetch=2,
            grid=(rows // tm,),
            in_specs=[pl.BlockSpec((tm, d), rows2), pl.BlockSpec((tm, LANES), rows2), hbm, hbm, hbm],
            out_specs=pl.BlockSpec((tm, d), rows2),
            scratch_shapes=[pltpu.VMEM((d, f), BF16), pltpu.VMEM((d, f), BF16), pltpu.VMEM((f, d), BF16),
                            pltpu.VMEM((2, 2, d, tf), F32), pltpu.VMEM((2, tf, d), F32),
                            pltpu.SemaphoreType.DMA((2, 3)), pltpu.VMEM((tm, d), F32)]),
        out_shape=jax.ShapeDtypeStruct((rows, d), BF16),
        compiler_params=_params(("arbitrary",)),
        name="moe_experts",
    )(plan["texp"], plan["n_valid"], hs, ws, w_gate, w_up, w_down)


def _combine_windows(before, total, off, n_rows):
    nblk = before.shape[0]
    after = jnp.concatenate([before[1:], total[None, :]], axis=0)
    lo = off[None, :] + before
    hi = off[None, :] + after
    first = (lo // ROW_ALIGN) * ROW_ALIGN
    nwin = jnp.where(hi > lo, (hi - first + MOE_WIN - 1) // MOE_WIN, 0)
    ends = jnp.cumsum(nwin, axis=1)
    k = jnp.arange(COMBINE_MAX_WIN, dtype=jnp.int32)
    e_of = jnp.minimum(jnp.sum((ends[:, None, :] <= k[None, :, None]).astype(jnp.int32), axis=2), N_EXPERTS - 1)
    i_of = k[None, :] - jnp.take_along_axis(ends - nwin, e_of, axis=1)
    live = k[None, :] < ends[:, -1:]
    want = jnp.where(live, jnp.take_along_axis(first, e_of, axis=1) + i_of * MOE_WIN, 0)
    start = jnp.clip(want, 0, n_rows - MOE_WIN)
    limit = jnp.where(live, jnp.take_along_axis(hi, e_of, axis=1), 0)
    flat = lambda a: a.reshape(-1).astype(jnp.int32)
    trips = ((ends[:, -1] + COMBINE_GROUP - 1) // COMBINE_GROUP).astype(jnp.int32)
    return flat(start), flat(want), flat(limit), trips, (jnp.cumsum(trips) - trips).astype(jnp.int32)


def _combine_kernel(start_ref, want_ref, limit_ref, trips_ref, seq_ref, off_ref, tok_ref, x_ref, mod_ref,
                    fnw_ref, ys_hbm, o_ref, buf_ref, sem_ref, acc_ref):
    b = pl.program_id(0)
    trips = trips_ref[b]
    nxt = jnp.minimum(b + 1, pl.num_programs(0) - 1)
    last_block = b + 1 >= pl.num_programs(0)
    tok = tok_ref[0]
    pos = [_slot_positions(tok, k, off_ref, False) for k in range(2)]

    def window_copies(blk, t, slot):
        copies = []
        for g in range(COMBINE_GROUP):
            start = pl.multiple_of(start_ref[blk * COMBINE_MAX_WIN + t * COMBINE_GROUP + g], ROW_ALIGN)
            copies.append(pltpu.make_async_copy(ys_hbm.at[pl.ds(start, MOE_WIN), :],
                                                buf_ref.at[slot, pl.ds(g * MOE_WIN, MOE_WIN), :],
                                                sem_ref.at[slot, g]))
        return copies

    @pl.when(b == 0)
    def _():
        for c in window_copies(b, 0, 0):
            c.start()

    acc_ref[...] = jnp.zeros_like(acc_ref)

    def trip(t, carry):
        slot = (seq_ref[b] + t) & 1
        more_here = t + 1 < trips

        @pl.when(jnp.logical_or(more_here, jnp.logical_not(last_block)))
        def _():
            for c in window_copies(jnp.where(more_here, b, nxt), jnp.where(more_here, t + 1, 0), 1 - slot):
                c.start()

        for c in window_copies(b, t, slot):
            c.wait()
        ids = []
        for g in range(COMBINE_GROUP):
            w = b * COMBINE_MAX_WIN + t * COMBINE_GROUP + g
            local = start_ref[w] + lax.broadcasted_iota(jnp.int32, (1, MOE_WIN), 1)
            ids.append(jnp.where(local >= want_ref[w], jnp.where(local < limit_ref[w], local, -1), -1))
        rowid = jnp.concatenate(ids, axis=1).astype(F32)
        onehot = (jnp.where(pos[0] == rowid, 1.0, 0.0) + jnp.where(pos[1] == rowid, 1.0, 0.0))
        acc_ref[...] += jnp.dot(onehot.astype(BF16), buf_ref[slot], preferred_element_type=F32)
        return carry

    lax.fori_loop(0, trips, trip, 0)
    o_ref[0] = _rms(x_ref[0] + mod_ref[0][2:3] * acc_ref[...], fnw_ref[...])


def moe_combine(plan, before, total, tok, ys, x, mod, final_nw):
    bsz, s, d = x.shape
    nb = MOE_TILE
    per_b = s // nb
    start, want, limit, trips, seq = _combine_windows(before, total, plan["off"], ys.shape[0])
    tok_blk = lambda b, *_: (b, 0, 0)
    x_blk = lambda b, *_: (b // per_b, b % per_b, 0)
    return pl.pallas_call(
        _combine_kernel,
        grid_spec=pltpu.PrefetchScalarGridSpec(
            num_scalar_prefetch=6,
            grid=(bsz * per_b,),
            in_specs=[pl.BlockSpec((1, nb, LANES), tok_blk),
                      pl.BlockSpec((1, nb, d), x_blk),
                      pl.BlockSpec((1, 3, d), lambda b, *_: (b // per_b, 0, 0)),
                      pl.BlockSpec((1, d), lambda b, *_: (0, 0)),
                      pl.BlockSpec(memory_space=pl.ANY)],
            out_specs=pl.BlockSpec((1, nb, d), x_blk),
            scratch_shapes=[pltpu.VMEM((2, COMBINE_GROUP * MOE_WIN, d), BF16),
                            pltpu.SemaphoreType.DMA((2, COMBINE_GROUP)),
                            pltpu.VMEM((nb, d), F32)]),
        out_shape=jax.ShapeDtypeStruct(x.shape, F32),
        compiler_params=_params(("arbitrary",)),
        name="moe_combine",
    )(start, want, limit, trips, seq, plan["off"], tok, x, mod, final_nw.reshape(1, d), ys)


def moe_layer(x, mod, nw, w_router, w_gate, w_up, w_down, final_nw):
    bsz, s, d = x.shape
    nblk = bsz * s // MOE_TILE
    n_tiles = 2 * nblk + N_EXPERTS
    hb, tok, tokt, tokw, before, total = moe_router(x, mod, nw, w_router)
    before = before[:, 0, :N_EXPERTS].astype(jnp.int32)
    total = total[0, :N_EXPERTS].astype(jnp.int32)
    plan = _moe_plan(before, total, n_tiles)
    hs, ws = moe_gather(plan, tokt, tokw, hb, n_tiles)
    ys = moe_experts(plan, hs, ws, w_gate, w_up, w_down)
    return moe_combine(plan, before, total, tok, ys, x, mod, final_nw)


def kernel(x, c, e_norm_mix, e_mod_mix_w, e_mod_mix_b, e_w_in, e_fox_fb, e_gla_w2, e_gla_b2, e_gla_norm, e_w_o, e_norm_ffn, e_mod_ffn_w, e_mod_ffn_b, e_ffn_gate, e_ffn_up, e_ffn_down, o_norm_mix, o_mod_mix_w, o_mod_mix_b, o_w_in, o_lam_re, o_lam_im, o_log_dt, o_b_re, o_b_im, o_c_re, o_c_im, o_d_skip, o_w_glu, o_w_o, o_norm_ffn, o_mod_ffn_w, o_mod_ffn_b, o_router, o_exp_gate, o_exp_up, o_exp_down, final_norm):
    bsz, s, d = x.shape

    mod, mod_ffn, mod_s5, mod_moe = adaln_mods(
        c, [e_mod_mix_w[0], e_mod_ffn_w[0], o_mod_mix_w[0], o_mod_ffn_w[0]],
        [e_mod_mix_b[0], e_mod_ffn_b[0], o_mod_mix_b[0], o_mod_ffn_b[0]])

    fq, fk, fv, gq, gk, gv, gg, small, cum = layer0_inproj(x, mod, e_norm_mix[0], e_w_in[0], e_fox_fb[0])
    fox = fox_attention(fq, fk, fv, cum)
    gla = gla_attention(gq, gk, gv, gg, small, e_gla_w2[0], e_gla_b2[0], e_gla_norm[0])
    x = mixer_out_and_ffn(x, fox, gla, e_w_o[0].astype(BF16), mod, mod_ffn, e_norm_ffn[0],
                          e_ffn_gate[0].astype(BF16), e_ffn_up[0].astype(BF16), e_ffn_down[0].astype(BF16))

    s5_tb = 256
    bb_re, bb_im, pow_re, pow_im = s5_discretise(o_lam_re[0], o_lam_im[0], o_log_dt[0], o_b_re[0], o_b_im[0],
                                                 s5_tb // S5_SEG)
    x = s5_layer(x, mod_s5, o_norm_mix[0], o_w_in[0], bb_re, bb_im, pow_re, pow_im, o_c_re[0], o_c_im[0],
                 o_d_skip[0], o_w_glu[0], o_w_o[0], s5_tb)
    return moe_layer(x, mod_moe, o_norm_ffn[0], o_router[0], o_exp_gate[0], o_exp_up[0], o_exp_down[0],
                     final_norm)
```

```python
import functools

import jax
import jax.numpy as jnp
from jax import lax
from jax.experimental import pallas as pl
from jax.experimental.pallas import tpu as pltpu

F32 = jnp.float32
BF16 = jnp.bfloat16
EPS = 1e-6
NEG_BIG = -1e30
LOG2E = 1.4426950408889634
FOX_BIAS_PARTS = 3
FOX_ROWS = 128
FOX_UNROLL = 4

LANES = 128
VMEM_LIMIT = 56 * 1024 * 1024

HEAD_DIM = 64
FOX_HEADS = 8
FOX_DIM = FOX_HEADS * HEAD_DIM
GLA_HEADS = 4
GLA_DK = 64
GLA_DV = 128
GLA_QK = GLA_HEADS * GLA_DK
GLA_V = GLA_HEADS * GLA_DV
GLA_RANK = 16
GLA_TAU = 16.0
GLA_CHUNK = 64
GLA_SUB = 8
S5_GB = 8
N_EXPERTS = 8
SMALL_FF0 = 0
SMALL_LR0 = 8


def _params(sem):
    return pltpu.CompilerParams(dimension_semantics=sem, vmem_limit_bytes=VMEM_LIMIT)


def _log_sigmoid(x):
    return jnp.minimum(x, 0.0) - jnp.log(1.0 + jnp.exp(-jnp.abs(x)))


def _silu(x):
    return x * jax.nn.sigmoid(x)


def _gelu_tanh(x):
    c = 0.7978845608028654
    return 0.5 * x * (1.0 + jnp.tanh(c * (x + 0.044715 * (x * x * x))))


def _rms(x, w):
    return x * lax.rsqrt(jnp.mean(x * x, axis=-1, keepdims=True) + EPS) * w


def _norm_mod(x, nw, mod):
    return _rms(x, nw) * (1.0 + mod[1:2]) + mod[0:1]


def _split_bf16(x, pieces):
    out = []
    for _ in range(pieces):
        p = x.astype(BF16)
        out.append(p)
        x = x - p.astype(F32)
    return out


def _dot_f32ish(a, b_hi, b_lo):
    a_hi, a_lo = _split_bf16(a, 2)
    return (jnp.dot(a_hi, b_hi, preferred_element_type=F32) + jnp.dot(a_hi, b_lo, preferred_element_type=F32)
            + jnp.dot(a_lo, b_hi, preferred_element_type=F32))


def _dot_exact_left(a_bf16, b):
    return sum(jnp.dot(a_bf16, p, preferred_element_type=F32) for p in _split_bf16(b, 3))


def _nt_dot(a, b):
    return lax.dot_general(a, b, (((1,), (1,)), ((), ())), preferred_element_type=F32)


def _mod_kernel(ct_ref, *refs):
    o_ref = refs[-1]
    n_mod = o_ref.shape[0]
    s = _silu(ct_ref[...])
    for m in range(n_mod):
        w = refs[m][...]
        bias = refs[n_mod + m][...]
        for b in range(s.shape[1]):
            o_ref[m, b:b + 1, :] = jnp.sum(s[:, b:b + 1] * w, axis=0, keepdims=True) + bias


def adaln_mods(c, weights, biases):
    bsz, d = c.shape
    n = weights[0].shape[1]
    tn = n // 3
    n_mod = len(weights)
    out = pl.pallas_call(
        _mod_kernel,
        grid=(n // tn,),
        in_specs=[pl.BlockSpec((d, bsz), lambda j: (0, 0))]
                 + [pl.BlockSpec((d, tn), lambda j: (0, j))] * n_mod
                 + [pl.BlockSpec((1, tn), lambda j: (0, j))] * n_mod,
        out_specs=pl.BlockSpec((n_mod, bsz, tn), lambda j: (0, 0, j)),
        out_shape=jax.ShapeDtypeStruct((n_mod, bsz, n), F32),
        compiler_params=_params(("parallel",)),
        name="adaln_mods",
    )(c.T, *weights, *[b.reshape(1, n) for b in biases])
    return [out[m].reshape(bsz, 3, d) for m in range(n_mod)]


def _inproj_kernel(x_ref, mod_ref, nw_ref, wbig_ref, wsm_ref, fb_ref, tri_ref,
                   fq_ref, fk_ref, fv_ref, gq_ref, gk_ref, gv_ref, gg_ref, sm_ref, cum_ref,
                   carry_ref):
    i = pl.program_id(1)
    tm = x_ref.shape[1]
    h = _norm_mod(x_ref[0], nw_ref[...], mod_ref[0])
    hb = h.astype(BF16)
    col = 0
    for ref in (fq_ref, fk_ref, fv_ref, gq_ref, gk_ref, gv_ref, gg_ref):
        n = ref.shape[2]
        y = jnp.dot(hb, wbig_ref[:, col:col + n], preferred_element_type=F32)
        if ref is fq_ref:
            y = y * (LOG2E * HEAD_DIM ** -0.5)
        ref[0] = y.astype(ref.dtype)
        col += n
    small = _dot_f32ish(h, wsm_ref[0], wsm_ref[1])
    sm_ref[0] = small
    logf = _log_sigmoid(small + fb_ref[...])

    @pl.when(i == 0)
    def _():
        carry_ref[...] = jnp.zeros_like(carry_ref)

    cum = _dot_exact_left(tri_ref[...], logf) + carry_ref[...]
    cum_ref[0] = cum
    carry_ref[...] = cum[tm - 1:tm, :]


def layer0_inproj(x, mod, nw, w_in, fox_fb, tm=512):
    bsz, s, d = x.shape
    c0 = 3 * FOX_DIM
    c_ff = c0
    c_g = c_ff + FOX_HEADS
    c_lr = c_g + 2 * GLA_QK + 2 * GLA_V
    wbig = jnp.concatenate([w_in[:, :c0], w_in[:, c_g:c_lr]], axis=1).astype(BF16)
    wsm = jnp.zeros((d, LANES), F32)
    wsm = wsm.at[:, SMALL_FF0:SMALL_FF0 + FOX_HEADS].set(w_in[:, c_ff:c_g])
    wsm = wsm.at[:, SMALL_LR0:SMALL_LR0 + GLA_RANK].set(w_in[:, c_lr:])
    fb = jnp.zeros((1, LANES), F32).at[0, SMALL_FF0:SMALL_FF0 + FOX_HEADS].set(fox_fb)
    wsm = jnp.stack(_split_bf16(wsm, 2))
    tri = jnp.tril(jnp.ones((tm, tm), BF16))
    widths = (FOX_DIM, FOX_DIM, FOX_DIM, GLA_QK, GLA_QK, GLA_V, GLA_V)
    row = lambda b, i: (b, i, 0)
    const2 = lambda b, i: (0, 0)
    outs = pl.pallas_call(
        _inproj_kernel,
        grid=(bsz, s // tm),
        in_specs=[pl.BlockSpec((1, tm, d), row),
                  pl.BlockSpec((1, 3, d), lambda b, i: (b, 0, 0)),
                  pl.BlockSpec((1, d), const2),
                  pl.BlockSpec(wbig.shape, const2),
                  pl.BlockSpec(wsm.shape, lambda b, i: (0, 0, 0)),
                  pl.BlockSpec((1, LANES), const2),
                  pl.BlockSpec((tm, tm), const2)],
        out_specs=[pl.BlockSpec((1, tm, n), row) for n in widths]
                  + [pl.BlockSpec((1, tm, LANES), row)] * 2,
        out_shape=[jax.ShapeDtypeStruct((bsz, s, n), BF16) for n in widths]
                  + [jax.ShapeDtypeStruct((bsz, s, LANES), F32)] * 2,
        scratch_shapes=[pltpu.VMEM((1, LANES), F32)],
        compiler_params=_params(("parallel", "arbitrary")),
        name="layer0_inproj",
    )(x, mod, nw.reshape(1, d), wbig, wsm, fb, tri)
    return outs


def _fox_kernel(q_ref, k_ref, v_ref, c_ref, o_ref, kaug_ref, vaug_ref, m_ref, acc_ref, qaug_ref):
    pair = pl.program_id(1)
    i = pl.program_id(2)
    t = q_ref.shape[1]
    lane = lax.broadcasted_iota(jnp.int32, (1, LANES), 1)
    heads = (lane < HEAD_DIM, lane >= HEAD_DIM)
    spare = (HEAD_DIM, 0)

    @pl.when(i == 0)
    def _():
        k = k_ref[0].astype(F32)
        v = v_ref[0].astype(F32)
        cum = c_ref[0]
        for h in range(2):
            bias = jnp.sum(jnp.where(lane == 2 * pair + h, cum, 0.0), axis=1, keepdims=True) * (-LOG2E)
            ka = jnp.where(heads[h], k, 0.0)
            rest = bias
            for n in range(FOX_BIAS_PARTS):
                piece = rest.astype(BF16).astype(F32)
                rest = rest - piece
                ka = jnp.where(lane == spare[h] + n, piece, ka)
            kaug_ref[h] = ka.astype(BF16)
            vaug_ref[h, :, :LANES] = jnp.where(heads[h], v, 0.0).astype(BF16)
            vaug_ref[h, :, LANES:] = jnp.broadcast_to(jnp.where(heads[h], 1.0, 0.0), v.shape).astype(BF16)

    q = q_ref[0].astype(F32)
    for h in range(2):
        ones_here = jnp.where(lane >= spare[h], jnp.where(lane < spare[h] + FOX_BIAS_PARTS, 1.0, 0.0), 0.0)
        qaug_ref[h] = jnp.where(heads[h], q, ones_here).astype(BF16)
    m_ref[...] = jnp.full_like(m_ref, NEG_BIG)
    acc_ref[...] = jnp.zeros_like(acc_ref)

    def tile(j, diagonal):
        keys = pl.ds(pl.multiple_of(j * t, t), t)
        for r0 in range(0, t, FOX_ROWS):
            rows = slice(r0, r0 + FOX_ROWS)
            alphas = []
            upd = None
            for h in range(2):
                s = _nt_dot(qaug_ref[h, rows, :], kaug_ref[h, keys, :])
                if diagonal:
                    r = r0 + lax.broadcasted_iota(jnp.int32, (FOX_ROWS, t), 0)
                    c = lax.broadcasted_iota(jnp.int32, (FOX_ROWS, t), 1)
                    s = jnp.where(c <= r, s, NEG_BIG)
                m_prev = m_ref[h, rows, :]
                m_next = jnp.maximum(m_prev, jnp.max(s, axis=1, keepdims=True))
                m_ref[h, rows, :] = m_next
                p = jnp.exp2(s - jnp.concatenate([m_next] * (t // LANES), axis=1)).astype(BF16)
                alphas.append(jnp.exp2(m_prev - m_next))
                d = jnp.dot(p, vaug_ref[h, keys, :], preferred_element_type=F32)
                upd = d if upd is None else upd + d
            alpha = jnp.where(heads[0], alphas[0], alphas[1])
            acc_ref[rows, :] = jnp.concatenate([alpha, alpha], axis=1) * acc_ref[rows, :] + upd

    done = 0
    width = FOX_UNROLL
    while width >= 2:
        def group(jj, carry, width=width, done=done):
            for u in range(width):
                tile(done + width * jj + u, False)
            return carry

        trips = (i - done) // width
        lax.fori_loop(0, trips, group, 0)
        done = done + trips * width
        width //= 2

    @pl.when(done < i)
    def _():
        tile(i - 1, False)
        tile(i, True)

    @pl.when(done == i)
    def _():
        tile(i, True)

    o_ref[0] = (acc_ref[:, :LANES] / acc_ref[:, LANES:]).astype(o_ref.dtype)


def fox_attention(fq, fk, fv, cum, t=512):
    bsz, s, _ = fq.shape
    n_pair = FOX_HEADS // 2
    seq = lambda b, p, i: (b, 0, p)
    return pl.pallas_call(
        _fox_kernel,
        grid=(bsz, n_pair, s // t),
        in_specs=[pl.BlockSpec((1, t, LANES), lambda b, p, i: (b, i, p)),
                  pl.BlockSpec((1, s, LANES), seq),
                  pl.BlockSpec((1, s, LANES), seq),
                  pl.BlockSpec((1, s, LANES), lambda b, p, i: (b, 0, 0))],
        out_specs=pl.BlockSpec((1, t, LANES), lambda b, p, i: (b, i, p)),
        out_shape=jax.ShapeDtypeStruct((bsz, s, FOX_DIM), BF16),
        scratch_shapes=[pltpu.VMEM((2, s, LANES), BF16), pltpu.VMEM((2, s, 2 * LANES), BF16),
                        pltpu.VMEM((2, t, LANES), F32), pltpu.VMEM((t, 2 * LANES), F32),
                        pltpu.VMEM((2, t, LANES), BF16)],
        compiler_params=_params(("parallel", "parallel", "arbitrary")),
        name="fox_attention",
    )(fq, fk, fv, cum)


def _gla_pair_chunk(q, k, v, la, st, tri, eye, k_scr, b_scr):
    C, SUB = GLA_CHUNK, GLA_SUB
    nsub = C // SUB
    lane = lax.broadcasted_iota(jnp.int32, (1, LANES), 1)
    head_a = lane < GLA_DK
    b = _dot_exact_left(tri, la) * LOG2E
    b_last = b[C - 1:C, :]
    k_scr[...] = k
    b_scr[...] = b
    rowblk = lax.broadcasted_iota(jnp.int32, (C, 1), 0) // SUB

    refs = [b[m * SUB - 1:m * SUB, :] for m in range(1, nsub)]
    rsel = refs[-1]
    for m in range(nsub - 2, 0, -1):
        rsel = jnp.where(rowblk == m, refs[m - 1], rsel)
    qt = q * jnp.exp2(jnp.minimum(b - rsel, 0.0))
    zero = jnp.zeros_like(q)
    qs = jnp.concatenate([jnp.where(rowblk == m, qt, zero) for m in range(1, nsub)], axis=1)
    ks = jnp.concatenate(
        [jnp.where(rowblk < m, k * jnp.exp2(jnp.minimum(refs[m - 1] - b, 0.0)), zero)
         for m in range(1, nsub)], axis=1)
    lane3 = lax.broadcasted_iota(jnp.int32, (1, (nsub - 1) * LANES), 1)
    head_a3 = (lane3 & (LANES - 1)) < GLA_DK
    zero3 = jnp.zeros_like(ks)
    kstack = jnp.concatenate([jnp.where(head_a3, ks, zero3), jnp.where(head_a3, zero3, ks)], axis=0)
    a_off = _nt_dot(qs.astype(BF16), kstack.astype(BF16))

    row16 = lax.broadcasted_iota(jnp.int32, (SUB, 1), 0)
    blocks = []
    for i in range(nsub):
        qb = q[i * SUB:(i + 1) * SUB, :]
        bb = b[i * SUB:(i + 1) * SUB, :]
        d = jnp.zeros((SUB, LANES), F32)
        for sp in range(SUB):
            srow = i * SUB + sp
            e = qb * k_scr[srow:srow + 1, :] * jnp.exp2(jnp.minimum(bb - b_scr[srow:srow + 1, :], 0.0))
            da = jnp.sum(jnp.where(head_a, e, 0.0), axis=1, keepdims=True)
            db = jnp.sum(jnp.where(head_a, 0.0, e), axis=1, keepdims=True)
            live = row16 >= sp
            da = jnp.where(live, da, 0.0)
            db = jnp.where(live, db, 0.0)
            d = jnp.where(lane == srow, da, jnp.where(lane == GLA_DK + srow, db, d))
        blocks.append(d)
    a_pair = a_off + jnp.concatenate(blocks, axis=0)

    lane_v = lax.broadcasted_iota(jnp.int32, (1, 2 * GLA_DV), 1)
    first_v = lane_v < GLA_DV
    zv = jnp.zeros_like(v)
    vbd = jnp.concatenate([jnp.where(first_v, v, zv), jnp.where(first_v, zv, v)], axis=0)
    qh = q * jnp.exp2(b)
    o = (jnp.dot(a_pair.astype(BF16), vbd, preferred_element_type=F32)
         + _nt_dot(qh.astype(BF16), st.astype(BF16)))

    k_end = k * jnp.exp2(b_last - b)
    v_t = _nt_dot(eye, v).astype(BF16)
    inc = jnp.dot(v_t, k_end.astype(BF16), preferred_element_type=F32)
    row_v = lax.broadcasted_iota(jnp.int32, (2 * GLA_DV, 1), 0)
    same_head = (row_v // GLA_DV) == (lane // GLA_DK)
    st_new = st * jnp.exp2(b_last) + jnp.where(same_head, inc, 0.0)
    return o, st_new


def _gla_kernel(gq_ref, gk_ref, gv_ref, gg_ref, sm_ref, w2_ref, b2_ref, gn_ref, tri_ref, eye_ref,
                o_ref, st_ref, la_ref, k_scr, b_scr):
    i = pl.program_id(1)
    tm = gq_ref.shape[1]
    n_pair = GLA_HEADS // 2
    scale = GLA_DK ** -0.5

    @pl.when(i == 0)
    def _():
        st_ref[...] = jnp.zeros_like(st_ref)

    z = _dot_f32ish(sm_ref[0], w2_ref[0], w2_ref[1]) + b2_ref[...]
    la_ref[...] = _log_sigmoid(z) * (1.0 / GLA_TAU)
    gn = gn_ref[...]

    def chunk(c, carry):
        base = pl.multiple_of(c * GLA_CHUNK, GLA_CHUNK)
        rows = pl.ds(base, GLA_CHUNK)
        for hp in range(n_pair):
            ql = slice(hp * LANES, (hp + 1) * LANES)
            vl = slice(hp * 2 * GLA_DV, (hp + 1) * 2 * GLA_DV)
            q = gq_ref[0, rows, ql].astype(F32) * scale
            k = gk_ref[0, rows, ql].astype(F32)
            v = gv_ref[0, rows, vl]
            o, st_new = _gla_pair_chunk(q, k, v, la_ref[rows, ql], st_ref[hp], tri_ref[...],
                                        eye_ref[...], k_scr.at[hp], b_scr.at[hp])
            st_ref[hp] = st_new
            gate = _silu(gg_ref[0, rows, vl].astype(F32))
            halves = [_rms(o[:, h * GLA_DV:(h + 1) * GLA_DV], gn) for h in range(2)]
            o_ref[0, rows, vl] = (jnp.concatenate(halves, axis=1) * gate).astype(o_ref.dtype)
        return carry

    lax.fori_loop(0, tm // GLA_CHUNK, chunk, 0, unroll=4)


def gla_attention(gq, gk, gv, gg, small, w2, b2, gnorm, tm=512):
    bsz, s, _ = gq.shape
    w2p = jnp.zeros((LANES, GLA_QK), F32).at[SMALL_LR0:SMALL_LR0 + GLA_RANK].set(w2)
    w2p = jnp.stack(_split_bf16(w2p, 2))
    tri = jnp.tril(jnp.ones((GLA_CHUNK, GLA_CHUNK), BF16))
    eye = jnp.eye(2 * GLA_DV, dtype=BF16)
    row = lambda b, i: (b, i, 0)
    const2 = lambda b, i: (0, 0)
    return pl.pallas_call(
        _gla_kernel,
        grid=(bsz, s // tm),
        in_specs=[pl.BlockSpec((1, tm, GLA_QK), row), pl.BlockSpec((1, tm, GLA_QK), row),
                  pl.BlockSpec((1, tm, GLA_V), row), pl.BlockSpec((1, tm, GLA_V), row),
                  pl.BlockSpec((1, tm, LANES), row),
                  pl.BlockSpec(w2p.shape, lambda b, i: (0, 0, 0)), pl.BlockSpec((1, GLA_QK), const2),
                  pl.BlockSpec((1, GLA_DV), const2), pl.BlockSpec(tri.shape, const2),
                  pl.BlockSpec(eye.shape, const2)],
        out_specs=pl.BlockSpec((1, tm, GLA_V), row),
        out_shape=jax.ShapeDtypeStruct((bsz, s, GLA_V), BF16),
        scratch_shapes=[pltpu.VMEM((GLA_HEADS // 2, 2 * GLA_DV, LANES), F32),
                        pltpu.VMEM((tm, GLA_QK), F32),
                        pltpu.VMEM((GLA_HEADS // 2, GLA_CHUNK, LANES), F32),
                        pltpu.VMEM((GLA_HEADS // 2, GLA_CHUNK, LANES), F32)],
        compiler_params=_params(("parallel", "arbitrary")),
        name="gla_attention",
    )(gq, gk, gv, gg, small, w2p, b2.reshape(1, GLA_QK), gnorm.reshape(1, GLA_DV), tri, eye)


def _mix_ffn_kernel(x_ref, a_ref, b_ref, wo_ref, modm_ref, modf_ref, nw_ref, wg_ref, wu_ref, wd_ref,
                    o_ref, acc_ref, *, tf):
    na = a_ref.shape[2]
    y = (jnp.dot(a_ref[0], wo_ref[:na, :], preferred_element_type=F32)
         + jnp.dot(b_ref[0], wo_ref[na:, :], preferred_element_type=F32))
    x1 = x_ref[0] + modm_ref[0][2:3] * y
    hb = _norm_mod(x1, nw_ref[...], modf_ref[0]).astype(BF16)
    for j in range(wg_ref.shape[1] // tf):
        cols = slice(j * tf, (j + 1) * tf)
        act = (_silu(jnp.dot(hb, wg_ref[:, cols], preferred_element_type=F32))
               * jnp.dot(hb, wu_ref[:, cols], preferred_element_type=F32))
        part = jnp.dot(act.astype(BF16), wd_ref[cols, :], preferred_element_type=F32)
        if j == 0:
            acc_ref[...] = part
        else:
            acc_ref[...] += part
    o_ref[0] = x1 + modf_ref[0][2:3] * acc_ref[...]


def mixer_out_and_ffn(x, a, b, w_o, mod_mix, mod_ffn, nw, w_gate, w_up, w_down, tm=512, tf=256):
    bsz, s, d = x.shape
    row = lambda bb, i: (bb, i, 0)
    per_b = lambda bb, i: (bb, 0, 0)
    const2 = lambda bb, i: (0, 0)
    resident = lambda arr: pl.BlockSpec(arr.shape, const2, pipeline_mode=pl.Buffered(1))
    return pl.pallas_call(
        functools.partial(_mix_ffn_kernel, tf=tf),
        grid=(bsz, s // tm),
        in_specs=[pl.BlockSpec((1, tm, d), row),
                  pl.BlockSpec((1, tm, a.shape[2]), row),
                  pl.BlockSpec((1, tm, b.shape[2]), row),
                  resident(w_o),
                  pl.BlockSpec((1, 3, d), per_b),
                  pl.BlockSpec((1, 3, d), per_b),
                  pl.BlockSpec((1, d), const2),
                  resident(w_gate), resident(w_up), resident(w_down)],
        out_specs=pl.BlockSpec((1, tm, d), row),
        out_shape=jax.ShapeDtypeStruct(x.shape, F32),
        scratch_shapes=[pltpu.VMEM((tm, d), F32)],
        compiler_params=_params(("parallel", "parallel")),
        name="mixer_out_and_ffn",
    )(x, a, b, w_o, mod_mix, mod_ffn, nw.reshape(1, d), w_gate, w_up, w_down)


S5_SEG = 8


def _s5_disc_kernel(lre_ref, lim_ref, ldt_ref, bre_ref, bim_ref,
                    bbre_ref, bbim_ref, pre_ref, pim_ref):
    lre = lre_ref[...]
    lim = lim_ref[...]
    dt = jnp.exp(ldt_ref[...])
    mag = jnp.exp(lre * dt)
    a_re = mag * jnp.cos(lim * dt)
    a_im = mag * jnp.sin(lim * dt)
    den = lre * lre + lim * lim
    nr = a_re - 1.0
    ni = a_im
    f_re = (nr * lre + ni * lim) / den
    f_im = (ni * lre - nr * lim) / den
    bre = bre_ref[...]
    bim = bim_ref[...]
    bbre_ref[...] = f_re[:, None, :] * bre - f_im[:, None, :] * bim
    bbim_ref[...] = f_re[:, None, :] * bim + f_im[:, None, :] * bre
    n = (lax.broadcasted_iota(jnp.int32, pre_ref.shape, 0) + 1).astype(F32)
    mag_n = jnp.exp(n * (lre * dt)[None])
    ang_n = n * (lim * dt)[None]
    pre_ref[...] = mag_n * jnp.cos(ang_n)
    pim_ref[...] = mag_n * jnp.sin(ang_n)


def s5_discretise(lam_re, lam_im, log_dt, b_re, b_im, seg_len):
    g, p = lam_re.shape
    k = b_re.shape[2]
    return pl.pallas_call(
        _s5_disc_kernel,
        out_shape=[jax.ShapeDtypeStruct((g, k, p), F32)] * 2 + [jax.ShapeDtypeStruct((seg_len, g, p), F32)] * 2,
        name="s5_discretise",
    )(lam_re, lam_im, log_dt.reshape(g, 1), b_re.transpose(0, 2, 1), b_im.transpose(0, 2, 1))


def _s5_kernel(x_ref, mod_ref, nw_ref, perm_ref, permt_ref, win_ref, bbd_ref, cbd_ref, apow_ref, dsk_ref,
               wglu_ref, wo_ref, o_ref, carry_ref, bu_ref, xs_ref, xsb_ref):
    i = pl.program_id(1)
    tb = x_ref.shape[1]
    nblk = bu_ref.shape[0]
    nch = bu_ref.shape[2] // (2 * LANES)
    seg_len = tb // S5_SEG
    re_l = [slice(ch * LANES, (ch + 1) * LANES) for ch in range(nch)]
    im_l = [slice((nch + ch) * LANES, (nch + ch + 1) * LANES) for ch in range(nch)]

    @pl.when(i == 0)
    def _():
        carry_ref[...] = jnp.zeros_like(carry_ref)

    x = x_ref[0]
    h = _norm_mod(x, nw_ref[...], mod_ref[0]).astype(BF16)
    hp = jnp.dot(perm_ref[...], h, preferred_element_type=F32).astype(BF16)
    u = jnp.dot(hp, win_ref[...], preferred_element_type=F32)
    ub = u.astype(BF16)
    for jb in range(nblk):
        bu_ref[jb] = jnp.dot(ub[:, jb * LANES:(jb + 1) * LANES], bbd_ref[jb], preferred_element_type=F32)

    seg_row = lax.broadcasted_iota(jnp.int32, (S5_SEG, 1), 0)

    def channel_block(jb, blk_carry):
        def bcast(n, lanes):
            return jnp.broadcast_to(apow_ref[jb, n:n + 1, lanes], (S5_SEG, LANES))

        a1 = [(bcast(0, re_l[ch]), bcast(0, im_l[ch])) for ch in range(nch)]

        def step(t, state):
            rows = pl.ds(pl.multiple_of(t * S5_SEG, S5_SEG), S5_SEG)
            out = []
            for ch in range(nch):
                xr, xi = state[2 * ch], state[2 * ch + 1]
                ar, ai = a1[ch]
                nr = ar * xr - ai * xi + bu_ref[jb, rows, re_l[ch]]
                ni = ar * xi + ai * xr + bu_ref[jb, rows, im_l[ch]]
                xs_ref[jb, rows, re_l[ch]] = nr
                xs_ref[jb, rows, im_l[ch]] = ni
                out += [nr, ni]
            return tuple(out)

        zero = jnp.zeros((S5_SEG, LANES), F32)
        ends = lax.fori_loop(0, seg_len, step, (zero,) * (2 * nch), unroll=4)

        starts = []
        for ch in range(nch):
            alr = apow_ref[jb, seg_len - 1:seg_len, re_l[ch]]
            ali = apow_ref[jb, seg_len - 1:seg_len, im_l[ch]]
            er, ei = ends[2 * ch], ends[2 * ch + 1]
            zr = carry_ref[jb, :, re_l[ch]]
            zi = carry_ref[jb, :, im_l[ch]]
            tr = jnp.zeros((S5_SEG, LANES), F32)
            ti = jnp.zeros((S5_SEG, LANES), F32)
            for s in range(S5_SEG):
                tr = jnp.where(seg_row == s, zr, tr)
                ti = jnp.where(seg_row == s, zi, ti)
                zr, zi = (alr * zr - ali * zi + er[s:s + 1], alr * zi + ali * zr + ei[s:s + 1])
            carry_ref[jb, :, re_l[ch]] = zr
            carry_ref[jb, :, im_l[ch]] = zi
            starts.append((jnp.concatenate([tr, tr], axis=0), jnp.concatenate([ti, ti], axis=0)))

        for t2 in range(seg_len // 2):
            rows = slice(t2 * 2 * S5_SEG, (t2 + 1) * 2 * S5_SEG)
            for ch in range(nch):
                pr = jnp.concatenate([bcast(2 * t2, re_l[ch]), bcast(2 * t2 + 1, re_l[ch])], axis=0)
                pi = jnp.concatenate([bcast(2 * t2, im_l[ch]), bcast(2 * t2 + 1, im_l[ch])], axis=0)
                zr, zi = starts[ch]
                xsb_ref[jb, rows, re_l[ch]] = (xs_ref[jb, rows, re_l[ch]] + pr * zr - pi * zi).astype(BF16)
                xsb_ref[jb, rows, im_l[ch]] = (xs_ref[jb, rows, im_l[ch]] + pr * zi + pi * zr).astype(BF16)
        return blk_carry

    lax.fori_loop(0, nblk, channel_block, 0)

    ys = [jnp.dot(xsb_ref[jb], cbd_ref[jb], preferred_element_type=F32) for jb in range(nblk)]
    y = jnp.concatenate(ys, axis=1) + dsk_ref[...] * u
    y = _gelu_tanh(y)
    y = y * jax.nn.sigmoid(jnp.dot(y.astype(BF16), wglu_ref[...], preferred_element_type=F32))
    z = jnp.dot(y.astype(BF16), wo_ref[...], preferred_element_type=F32)
    z_hi = z.astype(BF16)
    z_lo = (z - z_hi.astype(F32)).astype(BF16)
    z = (jnp.dot(permt_ref[...], z_hi, preferred_element_type=F32)
         + jnp.dot(permt_ref[...], z_lo, preferred_element_type=F32))
    o_ref[0] = x + mod_ref[0][2:3] * z


def s5_layer(x, mod, nw, w_in, bb_re, bb_im, pow_re, pow_im, c_re, c_im, d_skip, w_glu, w_o, tb):
    bsz, s, d = x.shape
    g, k, p = bb_re.shape
    nblk = g // S5_GB
    seg_len = pow_re.shape[0]
    assert tb == seg_len * S5_SEG
    eye = jnp.eye(S5_GB, dtype=F32)

    def blockdiag_in(bb):
        t = jnp.einsum('jgkp,gh->jgkhp', bb.reshape(nblk, S5_GB, k, p), eye)
        return t.reshape(nblk, S5_GB * k, S5_GB * p)

    def blockdiag_out(c):
        t = jnp.einsum('jgkp,gh->jgphk', c.reshape(nblk, S5_GB, k, p), eye)
        return t.reshape(nblk, S5_GB * p, S5_GB * k)

    bbd = jnp.concatenate([blockdiag_in(bb_re), blockdiag_in(bb_im)], axis=2).astype(BF16)
    cbd = jnp.concatenate([blockdiag_out(c_re), -blockdiag_out(c_im)], axis=1).astype(BF16)
    half = S5_GB * p
    apow = jnp.concatenate([pow_re.reshape(seg_len, nblk, half), pow_im.reshape(seg_len, nblk, half)],
                           axis=2).transpose(1, 0, 2)
    new_row = jnp.arange(tb)
    old_row = (new_row % S5_SEG) * seg_len + new_row // S5_SEG
    perm = (old_row[:, None] == jnp.arange(tb)[None, :]).astype(BF16)
    row = lambda b, i: (b, i, 0)
    const2 = lambda b, i: (0, 0)
    const3 = lambda b, i: (0, 0, 0)
    return pl.pallas_call(
        _s5_kernel,
        grid=(bsz, s // tb),
        in_specs=[pl.BlockSpec((1, tb, d), row),
                  pl.BlockSpec((1, 3, d), lambda b, i: (b, 0, 0)),
                  pl.BlockSpec((1, d), const2),
                  pl.BlockSpec((tb, tb), const2), pl.BlockSpec((tb, tb), const2),
                  pl.BlockSpec((d, d), const2),
                  pl.BlockSpec(bbd.shape, const3),
                  pl.BlockSpec(cbd.shape, const3),
                  pl.BlockSpec(apow.shape, const3),
                  pl.BlockSpec((1, d), const2),
                  pl.BlockSpec((d, d), const2), pl.BlockSpec((d, d), const2)],
        out_specs=pl.BlockSpec((1, tb, d), row),
        out_shape=jax.ShapeDtypeStruct(x.shape, F32),
        scratch_shapes=[pltpu.VMEM((nblk, 1, 2 * half), F32),
                        pltpu.VMEM((nblk, tb, 2 * half), F32),
                        pltpu.VMEM((nblk, tb, 2 * half), F32),
                        pltpu.VMEM((nblk, tb, 2 * half), BF16)],
        compiler_params=_params(("parallel", "arbitrary")),
        name="s5_layer",
    )(x, mod, nw.reshape(1, d), perm, perm.T, w_in.astype(BF16), bbd, cbd, apow, d_skip.reshape(1, d),
      w_glu.astype(BF16), w_o.astype(BF16))


TOK_E, TOK_RANK, TOK_W = 0, 2, 4
W_PIECES = 3
MOE_TILE = 512
MOE_WIN = 192
ROW_ALIGN = 16
COMBINE_GROUP = 2
COMBINE_MAX_WIN = -(-((2 * MOE_TILE + N_EXPERTS * (ROW_ALIGN - 1)) // MOE_WIN + N_EXPERTS + 1)
                    // COMBINE_GROUP) * COMBINE_GROUP


def _router_kernel(x_ref, mod_ref, nw_ref, wr_ref, tri_ref, sel_ref,
                   hb_ref, tok_ref, tokt_ref, tokw_ref, before_ref, total_ref, carry_ref):
    blk = pl.program_id(0)
    h = _norm_mod(x_ref[0], nw_ref[...], mod_ref[0])
    hb_ref[0] = h.astype(BF16)
    logits = _dot_f32ish(h, wr_ref[0], wr_ref[1])
    lane = lax.broadcasted_iota(jnp.int32, logits.shape, 1).astype(F32)
    logits = jnp.where(lane < N_EXPERTS, logits, -jnp.inf)
    m1 = jnp.max(logits, axis=1, keepdims=True)
    i1 = jnp.min(jnp.where(logits == m1, lane, float(LANES)), axis=1, keepdims=True)
    rest = jnp.where(lane == i1, -jnp.inf, logits)
    m2 = jnp.max(rest, axis=1, keepdims=True)
    i2 = jnp.min(jnp.where(rest == m2, lane, float(LANES)), axis=1, keepdims=True)
    e2 = jnp.exp(m2 - m1)
    w1 = 1.0 / (1.0 + e2)
    w2 = e2 / (1.0 + e2)

    @pl.when(blk == 0)
    def _():
        carry_ref[...] = jnp.zeros_like(carry_ref)

    routed = jnp.where(lane == i1, 1.0, jnp.where(lane == i2, 1.0, 0.0))
    before = carry_ref[...]
    rank = jnp.dot(tri_ref[...], routed.astype(BF16), preferred_element_type=F32) + before
    r1 = jnp.sum(jnp.where(lane == i1, rank, 0.0), axis=1, keepdims=True)
    r2 = jnp.sum(jnp.where(lane == i2, rank, 0.0), axis=1, keepdims=True)
    fields = ((TOK_E, i1), (TOK_E + 1, i2), (TOK_RANK, r1), (TOK_RANK + 1, r2), (TOK_W, w1), (TOK_W + 1, w2))
    tok = jnp.zeros_like(logits)
    for ln, val in fields:
        tok = jnp.where(lane == ln, val, tok)
    tok_ref[0] = tok
    tokt_ref[0] = sum(_nt_dot(sel_ref[...], p) for p in _split_bf16(tok, 3))
    pieces = jnp.zeros_like(logits)
    for k, w in enumerate((w1, w2)):
        rest = w
        for n in range(W_PIECES):
            piece = rest.astype(BF16).astype(F32)
            rest = rest - piece
            pieces = jnp.where(lane == k * W_PIECES + n, piece, pieces)
    tokw_ref[0] = pieces.astype(BF16)
    before_ref[0] = before
    total = before + jnp.sum(routed, axis=0, keepdims=True)
    carry_ref[...] = total
    total_ref[...] = total


def moe_router(x, mod, nw, w_router):
    bsz, s, d = x.shape
    nb = MOE_TILE
    per_b = s // nb
    nblk = bsz * per_b
    wr = jnp.stack(_split_bf16(jnp.zeros((d, LANES), F32).at[:, :N_EXPERTS].set(w_router), 2))
    tri = jnp.tril(jnp.ones((nb, nb), BF16), k=-1)
    sel = jnp.eye(8, LANES, dtype=BF16)
    const2 = lambda i: (0, 0)
    return pl.pallas_call(
        _router_kernel,
        grid=(nblk,),
        in_specs=[pl.BlockSpec((1, nb, d), lambda i: (i // per_b, i % per_b, 0)),
                  pl.BlockSpec((1, 3, d), lambda i: (i // per_b, 0, 0)),
                  pl.BlockSpec((1, d), const2),
                  pl.BlockSpec((2, d, LANES), lambda i: (0, 0, 0)),
                  pl.BlockSpec((nb, nb), const2),
                  pl.BlockSpec((8, LANES), const2)],
        out_specs=[pl.BlockSpec((1, nb, d), lambda i: (i, 0, 0)),
                   pl.BlockSpec((1, nb, LANES), lambda i: (i, 0, 0)),
                   pl.BlockSpec((1, 8, nb), lambda i: (i, 0, 0)),
                   pl.BlockSpec((1, nb, LANES), lambda i: (i, 0, 0)),
                   pl.BlockSpec((1, 1, LANES), lambda i: (i, 0, 0)),
                   pl.BlockSpec((1, LANES), const2)],
        out_shape=[jax.ShapeDtypeStruct((nblk, nb, d), BF16),
                   jax.ShapeDtypeStruct((nblk, nb, LANES), F32),
                   jax.ShapeDtypeStruct((nblk, 8, nb), F32),
                   jax.ShapeDtypeStruct((nblk, nb, LANES), BF16),
                   jax.ShapeDtypeStruct((nblk, 1, LANES), F32),
                   jax.ShapeDtypeStruct((1, LANES), F32)],
        scratch_shapes=[pltpu.VMEM((1, LANES), F32)],
        compiler_params=_params(("arbitrary",)),
        name="moe_router",
    )(x, mod, nw.reshape(1, d), wr, tri, sel)


def _moe_plan(before, total, n_tiles):
    tm = MOE_TILE
    ntile = (total + tm - 1) // tm
    tile_end = jnp.cumsum(ntile)
    tile_start = tile_end - ntile
    n_valid = tile_end[-1]
    r = jnp.arange(n_tiles, dtype=jnp.int32)
    texp = jnp.minimum(jnp.sum((tile_end[None, :] <= r[:, None]).astype(jnp.int32), axis=1), N_EXPERTS - 1)
    r0 = (r - tile_start[texp]) * tm
    after = jnp.concatenate([before[1:], total[None, :]], axis=0)
    lo = before[:, texp].T
    hi = after[:, texp].T
    meets = (lo < (r0 + tm)[:, None]) & (hi > r0[:, None]) & (r < n_valid)[:, None]
    count = jnp.sum(meets, axis=1).astype(jnp.int32)
    first_blk = jnp.sum(jnp.cumsum(meets, axis=1) == 0, axis=1).astype(jnp.int32)
    start = jnp.clip(lo - r0[:, None], 0, tm)
    end = jnp.clip(hi - r0[:, None], 0, tm)
    first = (start // ROW_ALIGN) * ROW_ALIGN
    nwin = jnp.where(meets, (end - first + MOE_WIN - 1) // MOE_WIN, 0)
    flat = lambda a: a.reshape(-1).astype(jnp.int32)
    return dict(texp=texp.astype(jnp.int32), n_valid=n_valid.astype(jnp.int32).reshape(1),
                off=(tile_start * tm).astype(jnp.int32),
                g_blk0=jnp.where(count > 0, first_blk, 0), g_count=count,
                g_first=flat(first), g_nwin=flat(nwin))


def _slot_positions(tok_fields, k, off_ref, along_lanes):
    if along_lanes:
        e = tok_fields[TOK_E + k:TOK_E + k + 1, :]
        rank = tok_fields[TOK_RANK + k:TOK_RANK + k + 1, :]
    else:
        e = tok_fields[:, TOK_E + k:TOK_E + k + 1]
        rank = tok_fields[:, TOK_RANK + k:TOK_RANK + k + 1]
    pos = rank
    for ex in range(N_EXPERTS):
        pos = pos + jnp.where(e == float(ex), off_ref[ex].astype(F32), 0.0)
    return pos


def _window_rows(want, tile, tm, shape, axis):
    base = pl.multiple_of(jnp.minimum(want, tm - MOE_WIN), ROW_ALIGN)
    local = base + lax.broadcasted_iota(jnp.int32, shape, axis)
    rowid = jnp.where(local >= want, tile * tm + local, -1).astype(F32)
    return base, rowid


def _gather_kernel(blk0_ref, count_ref, first_ref, nwin_ref, off_ref, tokt_ref, tokw_ref, hb_ref,
                   hs_ref, ws_ref, acc_ref, wacc_ref):
    r = pl.program_id(0)
    tm = hs_ref.shape[0]
    nblk = hb_ref.shape[0]
    b0 = blk0_ref[r]
    lane = lax.broadcasted_iota(jnp.int32, (1, LANES), 1)
    acc_ref[...] = jnp.zeros_like(acc_ref)
    wacc_ref[...] = jnp.zeros_like(wacc_ref)

    def pair(k, carry):
        blk = b0 + k
        tokt = tokt_ref[blk]
        pos = [_slot_positions(tokt, s, off_ref, True) for s in range(2)]
        p = r * nblk + blk

        def window(i, wcarry):
            base, rowid = _window_rows(first_ref[p] + i * MOE_WIN, r, tm, (MOE_WIN, 1), 0)
            onehots = [jnp.where(pos[s] == rowid, 1.0, 0.0).astype(BF16) for s in range(2)]
            rows = pl.ds(base, MOE_WIN)
            acc_ref[rows, :] += jnp.dot(onehots[0] + onehots[1], hb_ref[blk], preferred_element_type=F32)
            picked = [jnp.dot(onehots[s], tokw_ref[blk], preferred_element_type=F32) for s in range(2)]
            wacc_ref[rows, :] += jnp.where(lane < W_PIECES, picked[0],
                                           jnp.where(lane < 2 * W_PIECES, picked[1], 0.0))
            return wcarry

        lax.fori_loop(0, nwin_ref[p], window, 0)
        return carry

    lax.fori_loop(0, count_ref[r], pair, 0)
    hs_ref[...] = acc_ref[...].astype(hs_ref.dtype)
    ws_ref[...] = wacc_ref[...]


def moe_gather(plan, tokt, tokw, hb, n_tiles):
    nblk, nb, d = hb.shape
    tm = MOE_TILE
    by_tile2 = lambda r, *_: (r, 0)
    resident = lambda arr: pl.BlockSpec(arr.shape, lambda r, *_: (0, 0, 0), pipeline_mode=pl.Buffered(1))
    return pl.pallas_call(
        _gather_kernel,
        grid_spec=pltpu.PrefetchScalarGridSpec(
            num_scalar_prefetch=5,
            grid=(n_tiles,),
            in_specs=[resident(tokt), resident(tokw), resident(hb)],
            out_specs=[pl.BlockSpec((tm, d), by_tile2), pl.BlockSpec((tm, LANES), by_tile2)],
            scratch_shapes=[pltpu.VMEM((tm, d), F32), pltpu.VMEM((tm, LANES), F32)]),
        out_shape=[jax.ShapeDtypeStruct((n_tiles * tm, d), BF16),
                   jax.ShapeDtypeStruct((n_tiles * tm, LANES), F32)],
        compiler_params=_params(("arbitrary",)),
        name="moe_gather",
    )(plan["g_blk0"], plan["g_count"], plan["g_first"], plan["g_nwin"], plan["off"], tokt, tokw, hb)


def _experts_kernel(texp_ref, nvalid_ref, hs_ref, ws_ref, wg_hbm, wu_hbm, wd_hbm, ys_ref,
                    wg_ref, wu_ref, wd_ref, stage_in_ref, stage_out_ref, sem_ref, acc_ref, *, tf):
    r = pl.program_id(0)
    valid = r < nvalid_ref[0]
    e = texp_ref[r]
    fresh = jnp.logical_or(r == 0, texp_ref[jnp.maximum(r - 1, 0)] != e)
    n_chunk = wg_ref.shape[1] // tf

    def chunk_copies(j, slot):
        cols = pl.ds(j * tf, tf)
        return [pltpu.make_async_copy(wg_hbm.at[e, :, cols], stage_in_ref.at[slot, 0], sem_ref.at[slot, 0]),
                pltpu.make_async_copy(wu_hbm.at[e, :, cols], stage_in_ref.at[slot, 1], sem_ref.at[slot, 1]),
                pltpu.make_async_copy(wd_hbm.at[e, cols, :], stage_out_ref.at[slot], sem_ref.at[slot, 2])]

    def chunk(j, hb):
        cols = slice(j * tf, (j + 1) * tf)
        act = (_silu(jnp.dot(hb, wg_ref[:, cols], preferred_element_type=F32))
               * jnp.dot(hb, wu_ref[:, cols], preferred_element_type=F32))
        part = jnp.dot(act.astype(BF16), wd_ref[cols, :], preferred_element_type=F32)
        if j == 0:
            acc_ref[...] = part
        else:
            acc_ref[...] += part

    def finish():
        w = jnp.sum(ws_ref[...], axis=1, keepdims=True)
        ys_ref[...] = (acc_ref[...] * w).astype(ys_ref.dtype)

    @pl.when(jnp.logical_and(valid, fresh))
    def _():
        for which, c in enumerate(chunk_copies(0, 0)):
            c.start(priority=which % 2)
        hb = hs_ref[...]
        for j in range(n_chunk):
            slot = j & 1
            if j + 1 < n_chunk:
                for which, c in enumerate(chunk_copies(j + 1, 1 - slot)):
                    c.start(priority=which % 2)
            for c in chunk_copies(j, slot):
                c.wait()
            cols = slice(j * tf, (j + 1) * tf)
            wg_ref[:, cols] = stage_in_ref[slot, 0].astype(BF16)
            wu_ref[:, cols] = stage_in_ref[slot, 1].astype(BF16)
            wd_ref[cols, :] = stage_out_ref[slot].astype(BF16)
            chunk(j, hb)
        finish()

    @pl.when(jnp.logical_and(valid, jnp.logical_not(fresh)))
    def _():
        hb = hs_ref[...]
        for j in range(n_chunk):
            chunk(j, hb)
        finish()

    @pl.when(jnp.logical_not(valid))
    def _():
        ys_ref[...] = jnp.zeros_like(ys_ref)


def moe_experts(plan, hs, ws, w_gate, w_up, w_down, tf=256):
    rows, d = hs.shape
    tm = MOE_TILE
    f = w_gate.shape[2]
    rows2 = lambda r, te, nv: (r, 0)
    hbm = pl.BlockSpec(memory_space=pl.ANY)
    return pl.pallas_call(
        functools.partial(_experts_kernel, tf=tf),
        grid_spec=pltpu.PrefetchScalarGridSpec(
            num_scalar_prefetch=2,
            grid=(rows // tm,),
            in_specs=[pl.BlockSpec((tm, d), rows2), pl.BlockSpec((tm, LANES), rows2), hbm, hbm, hbm],
            out_specs=pl.BlockSpec((tm, d), rows2),
            scratch_shapes=[pltpu.VMEM((d, f), BF16), pltpu.VMEM((d, f), BF16), pltpu.VMEM((f, d), BF16),
                            pltpu.VMEM((2, 2, d, tf), F32), pltpu.VMEM((2, tf, d), F32),
                            pltpu.SemaphoreType.DMA((2, 3)), pltpu.VMEM((tm, d), F32)]),
        out_shape=jax.ShapeDtypeStruct((rows, d), BF16),
        compiler_params=_params(("arbitrary",)),
        name="moe_experts",
    )(plan["texp"], plan["n_valid"], hs, ws, w_gate, w_up, w_down)


def _combine_windows(before, total, off, n_rows):
    nblk = before.shape[0]
    after = jnp.concatenate([before[1:], total[None, :]], axis=0)
    lo = off[None, :] + before
    hi = off[None, :] + after
    first = (lo // ROW_ALIGN) * ROW_ALIGN
    nwin = jnp.where(hi > lo, (hi - first + MOE_WIN - 1) // MOE_WIN, 0)
    ends = jnp.cumsum(nwin, axis=1)
    k = jnp.arange(COMBINE_MAX_WIN, dtype=jnp.int32)
    e_of = jnp.minimum(jnp.sum((ends[:, None, :] <= k[None, :, None]).astype(jnp.int32), axis=2), N_EXPERTS - 1)
    i_of = k[None, :] - jnp.take_along_axis(ends - nwin, e_of, axis=1)
    live = k[None, :] < ends[:, -1:]
    want = jnp.where(live, jnp.take_along_axis(first, e_of, axis=1) + i_of * MOE_WIN, 0)
    start = jnp.clip(want, 0, n_rows - MOE_WIN)
    limit = jnp.where(live, jnp.take_along_axis(hi, e_of, axis=1), 0)
    flat = lambda a: a.reshape(-1).astype(jnp.int32)
    trips = ((ends[:, -1] + COMBINE_GROUP - 1) // COMBINE_GROUP).astype(jnp.int32)
    return flat(start), flat(want), flat(limit), trips, (jnp.cumsum(trips) - trips).astype(jnp.int32)


def _combine_kernel(start_ref, want_ref, limit_ref, trips_ref, seq_ref, off_ref, tok_ref, x_ref, mod_ref,
                    fnw_ref, ys_hbm, o_ref, buf_ref, sem_ref, acc_ref):
    b = pl.program_id(0)
    trips = trips_ref[b]
    nxt = jnp.minimum(b + 1, pl.num_programs(0) - 1)
    last_block = b + 1 >= pl.num_programs(0)
    tok = tok_ref[0]
    pos = [_slot_positions(tok, k, off_ref, False) for k in range(2)]

    def window_copies(blk, t, slot):
        copies = []
        for g in range(COMBINE_GROUP):
            start = pl.multiple_of(start_ref[blk * COMBINE_MAX_WIN + t * COMBINE_GROUP + g], ROW_ALIGN)
            copies.append(pltpu.make_async_copy(ys_hbm.at[pl.ds(start, MOE_WIN), :],
                                                buf_ref.at[slot, pl.ds(g * MOE_WIN, MOE_WIN), :],
                                                sem_ref.at[slot, g]))
        return copies

    @pl.when(b == 0)
    def _():
        for c in window_copies(b, 0, 0):
            c.start()

    acc_ref[...] = jnp.zeros_like(acc_ref)

    def trip(t, carry):
        slot = (seq_ref[b] + t) & 1
        more_here = t + 1 < trips

        @pl.when(jnp.logical_or(more_here, jnp.logical_not(last_block)))
        def _():
            for c in window_copies(jnp.where(more_here, b, nxt), jnp.where(more_here, t + 1, 0), 1 - slot):
                c.start()

        for c in window_copies(b, t, slot):
            c.wait()
        ids = []
        for g in range(COMBINE_GROUP):
            w = b * COMBINE_MAX_WIN + t * COMBINE_GROUP + g
            local = start_ref[w] + lax.broadcasted_iota(jnp.int32, (1, MOE_WIN), 1)
            ids.append(jnp.where(local >= want_ref[w], jnp.where(local < limit_ref[w], local, -1), -1))
        rowid = jnp.concatenate(ids, axis=1).astype(F32)
        onehot = (jnp.where(pos[0] == rowid, 1.0, 0.0) + jnp.where(pos[1] == rowid, 1.0, 0.0))
        acc_ref[...] += jnp.dot(onehot.astype(BF16), buf_ref[slot], preferred_element_type=F32)
        return carry

    lax.fori_loop(0, trips, trip, 0)
    o_ref[0] = _rms(x_ref[0] + mod_ref[0][2:3] * acc_ref[...], fnw_ref[...])


def moe_combine(plan, before, total, tok, ys, x, mod, final_nw):
    bsz, s, d = x.shape
    nb = MOE_TILE
    per_b = s // nb
    start, want, limit, trips, seq = _combine_windows(before, total, plan["off"], ys.shape[0])
    tok_blk = lambda b, *_: (b, 0, 0)
    x_blk = lambda b, *_: (b // per_b, b % per_b, 0)
    return pl.pallas_call(
        _combine_kernel,
        grid_spec=pltpu.PrefetchScalarGridSpec(
            num_scalar_prefetch=6,
            grid=(bsz * per_b,),
            in_specs=[pl.BlockSpec((1, nb, LANES), tok_blk),
                      pl.BlockSpec((1, nb, d), x_blk),
                      pl.BlockSpec((1, 3, d), lambda b, *_: (b // per_b, 0, 0)),
                      pl.BlockSpec((1, d), lambda b, *_: (0, 0)),
                      pl.BlockSpec(memory_space=pl.ANY)],
            out_specs=pl.BlockSpec((1, nb, d), x_blk),
            scratch_shapes=[pltpu.VMEM((2, COMBINE_GROUP * MOE_WIN, d), BF16),
                            pltpu.SemaphoreType.DMA((2, COMBINE_GROUP)),
                            pltpu.VMEM((nb, d), F32)]),
        out_shape=jax.ShapeDtypeStruct(x.shape, F32),
        compiler_params=_params(("arbitrary",)),
        name="moe_combine",
    )(start, want, limit, trips, seq, plan["off"], tok, x, mod, final_nw.reshape(1, d), ys)


def moe_layer(x, mod, nw, w_router, w_gate, w_up, w_down, final_nw):
    bsz, s, d = x.shape
    nblk = bsz * s // MOE_TILE
    n_tiles = 2 * nblk + N_EXPERTS
    hb, tok, tokt, tokw, before, total = moe_router(x, mod, nw, w_router)
    before = before[:, 0, :N_EXPERTS].astype(jnp.int32)
    total = total[0, :N_EXPERTS].astype(jnp.int32)
    plan = _moe_plan(before, total, n_tiles)
    hs, ws = moe_gather(plan, tokt, tokw, hb, n_tiles)
    ys = moe_experts(plan, hs, ws, w_gate, w_up, w_down)
    return moe_combine(plan, before, total, tok, ys, x, mod, final_nw)


def kernel(x, c, e_norm_mix, e_mod_mix_w, e_mod_mix_b, e_w_in, e_fox_fb, e_gla_w2, e_gla_b2, e_gla_norm, e_w_o, e_norm_ffn, e_mod_ffn_w, e_mod_ffn_b, e_ffn_gate, e_ffn_up, e_ffn_down, o_norm_mix, o_mod_mix_w, o_mod_mix_b, o_w_in, o_lam_re, o_lam_im, o_log_dt, o_b_re, o_b_im, o_c_re, o_c_im, o_d_skip, o_w_glu, o_w_o, o_norm_ffn, o_mod_ffn_w, o_mod_ffn_b, o_router, o_exp_gate, o_exp_up, o_exp_down, final_norm):
    bsz, s, d = x.shape

    mod, mod_ffn, mod_s5, mod_moe = adaln_mods(
        c, [e_mod_mix_w[0], e_mod_ffn_w[0], o_mod_mix_w[0], o_mod_ffn_w[0]],
        [e_mod_mix_b[0], e_mod_ffn_b[0], o_mod_mix_b[0], o_mod_ffn_b[0]])

    fq, fk, fv, gq, gk, gv, gg, small, cum = layer0_inproj(x, mod, e_norm_mix[0], e_w_in[0], e_fox_fb[0])
    fox = fox_attention(fq, fk, fv, cum)
    gla = gla_attention(gq, gk, gv, gg, small, e_gla_w2[0], e_gla_b2[0], e_gla_norm[0])
    x = mixer_out_and_ffn(x, fox, gla, e_w_o[0].astype(BF16), mod, mod_ffn, e_norm_ffn[0],
                          e_ffn_gate[0].astype(BF16), e_ffn_up[0].astype(BF16), e_ffn_down[0].astype(BF16))

    s5_tb = 256
    bb_re, bb_im, pow_re, pow_im = s5_discretise(o_lam_re[0], o_lam_im[0], o_log_dt[0], o_b_re[0], o_b_im[0],
                                                 s5_tb // S5_SEG)
    x = s5_layer(x, mod_s5, o_norm_mix[0], o_w_in[0], bb_re, bb_im, pow_re, pow_im, o_c_re[0], o_c_im[0],
                 o_d_skip[0], o_w_glu[0], o_w_o[0], s5_tb)
    return moe_layer(x, mod_moe, o_norm_ffn[0], o_router[0], o_exp_gate[0], o_exp_up[0], o_exp_down[0],
                     final_norm)
```
